```python
import math, functools
import jax, jax.numpy as jnp
from jax import lax
import numpy as np

D_MODEL = 2048
BATCH = 32
SEQ = 256
DEPTH = 2
DEC_BATCH = 4
DEC_SEQ = 4096
PAST_LEN = 256

GRID_W = 64
N_MIXERS = 2
N_MLA = (DEPTH + 1) // 2
N_POOL = DEPTH // 2
N_HEADS = 16
QK_NOPE = 128
ROPE_DIM = 64
QK_HEAD = QK_NOPE + ROPE_DIM
V_HEAD = 128
Q_LORA = 768
KV_LORA = 256
ROPE_THETA = 10000.0
AXIS_DIM = ROPE_DIM // 2
Q_BLOCK = 128
ATTN_SCALE = QK_HEAD ** -0.5
POOL_WINDOWS = (2, 4, 8, 16)
POOL_GROUPS = len(POOL_WINDOWS)
POOL_GROUP = D_MODEL // POOL_GROUPS
N_EXPERTS = 16
N_GROUPS = 4
EXPERTS_PER_GROUP = N_EXPERTS // N_GROUPS
TOP_K = 2
EXPERT_FF = 512
EPS = 1e-6

kernel_name = 'hybrid_mla_pool_moe_diffusion_step'

F32 = jnp.float32


def rms_norm(x, w):
    xf = x.astype(F32)
    y = xf * lax.rsqrt(jnp.mean(xf * xf, axis=-1, keepdims=True) + EPS)
    return (y * w.astype(F32)).astype(x.dtype)


def adaln(cond, w, b):
    m = jax.nn.silu(cond) @ w + b
    m = m[..., None, :]
    return jnp.split(m, 6, axis=-1)


def axial_rope(L):
    rows = L // GRID_W
    row = jnp.repeat(jnp.arange(rows, dtype=F32), GRID_W)
    col = jnp.tile(jnp.arange(GRID_W, dtype=F32), rows)
    inv = jnp.power(ROPE_THETA, -jnp.arange(0, AXIS_DIM, 2, dtype=F32) / AXIS_DIM)
    ang = jnp.stack([row[:, None] * inv, col[:, None] * inv], axis=1)
    return jnp.cos(ang), jnp.sin(ang)


def apply_rope(x, cos, sin):
    B, L, H, _ = x.shape
    xn = x[..., :QK_NOPE]
    xr = x[..., QK_NOPE:].reshape(B, L, H, 2, 2, AXIS_DIM // 2).astype(F32)
    x1, x2 = xr[..., 0, :], xr[..., 1, :]
    cs, sn = cos[None, :, None], sin[None, :, None]
    rot = jnp.stack([x1 * cs - x2 * sn, x1 * sn + x2 * cs], axis=-2).reshape(B, L, H, ROPE_DIM)
    return jnp.concatenate([xn, rot.astype(x.dtype)], axis=-1)


def mla_compress(h, w_dkv, kv_lora_norm):
    kv = h @ w_dkv
    ckv = rms_norm(kv[..., :KV_LORA], kv_lora_norm)
    krope = kv[..., KV_LORA:]
    return ckv, krope


def mla_queries(h, w_dq, q_lora_norm, w_uq, q_norm):
    B, L, _ = h.shape
    cq = rms_norm(h @ w_dq, q_lora_norm)
    q = (cq @ w_uq).reshape(B, L, N_HEADS, QK_HEAD)
    return rms_norm(q, q_norm)


def mla_expand(ckv, krope, w_uk, w_uv, k_norm):
    B, L, _ = ckv.shape
    k_nope = (ckv @ w_uk).reshape(B, L, N_HEADS, QK_NOPE)
    k_rope = jnp.broadcast_to(krope[:, :, None, :], (B, L, N_HEADS, ROPE_DIM))
    k = rms_norm(jnp.concatenate([k_nope, k_rope], axis=-1), k_norm)
    v = (ckv @ w_uv).reshape(B, L, N_HEADS, V_HEAD)
    return k, v


def attend(q, k, v):
    B, L, H, Dq = q.shape
    nb = L // Q_BLOCK
    qb = q.reshape(B, nb, Q_BLOCK, H, Dq).transpose(1, 0, 2, 3, 4)

    def one_block(qblk):
        s = jnp.einsum('bqhd,bkhd->bhqk', qblk, k, preferred_element_type=F32) * ATTN_SCALE
        p = jax.nn.softmax(s, axis=-1)
        return jnp.einsum('bhqk,bkhd->bqhd', p.astype(v.dtype), v)

    o = lax.map(one_block, qb)
    return o.transpose(1, 0, 2, 3, 4).reshape(B, L, H * V_HEAD)


def multiscale_pool(h, w_pool, pool_scale):
    B, L, D = h.shape
    hf = h.astype(F32)
    csum = jnp.concatenate([jnp.zeros((B, 1, D), F32), jnp.cumsum(hf, axis=1)], axis=1)
    t = jnp.arange(L)
    outs = []
    for g, w in enumerate(POOL_WINDOWS):
        lo = jnp.clip(t - w // 2, 0, L)
        hi = jnp.clip(t + w // 2, 0, L)
        cs = csum[..., g * POOL_GROUP:(g + 1) * POOL_GROUP]
        cnt = (hi - lo).astype(F32)[None, :, None]
        outs.append((cs[:, hi] - cs[:, lo]) / cnt - hf[..., g * POOL_GROUP:(g + 1) * POOL_GROUP])
    pooled = jnp.stack(outs, axis=2).astype(h.dtype)
    y = jnp.einsum('blgc,gcd->blgd', pooled, w_pool).reshape(B, L, D)
    return y * pool_scale


def moe(h, router_w, router_bias, w_gate, w_up, w_down):
    B, L, D = h.shape
    x = h.reshape(B * L, D)
    scores = jax.nn.sigmoid((x @ router_w).astype(F32))
    sel = scores + router_bias.astype(F32)
    grp = sel.reshape(-1, N_GROUPS, EXPERTS_PER_GROUP)
    grp_score = jnp.sum(lax.top_k(grp, 2)[0], axis=-1)
    best = jnp.argmax(grp_score, axis=-1)
    emask = jnp.repeat(jax.nn.one_hot(best, N_GROUPS, dtype=jnp.bool_), EXPERTS_PER_GROUP, axis=-1)
    masked = jnp.where(emask, sel, jnp.float32(-1e30))
    _, idx = lax.top_k(masked, TOP_K)
    wts = jnp.take_along_axis(scores, idx, axis=-1)
    wts = wts / jnp.sum(wts, axis=-1, keepdims=True)
    gates = jnp.sum(jax.nn.one_hot(idx, N_EXPERTS, dtype=F32) * wts[..., None], axis=1)
    out = jnp.zeros_like(x)
    for e in range(N_EXPERTS):
        a = jax.nn.silu(x @ w_gate[e]) * (x @ w_up[e])
        out = out + gates[:, e:e + 1].astype(x.dtype) * (a @ w_down[e])
    return out.reshape(B, L, D)


def setup_inputs(seed: int = 0) -> dict:
    key = jax.random.key(seed)
    ks = jax.random.split(key, 32)

    def nrm(k, shape, fan_in, s=1.0):
        return jax.random.normal(k, shape, F32) * (s * fan_in ** -0.5)

    def gain(k, shape):
        return 1.0 + 0.05 * jax.random.normal(k, shape, F32)

    D = D_MODEL
    return {
        'x_prompt': jax.random.normal(ks[0], (BATCH, SEQ, D), F32),
        'x_sample': jax.random.normal(ks[1], (DEC_BATCH, DEC_SEQ, D), F32),
        'cache_ckv': jax.random.normal(ks[2], (DEC_BATCH, N_MLA, PAST_LEN, KV_LORA), F32),
        'cache_krope': jax.random.normal(ks[3], (DEC_BATCH, N_MLA, PAST_LEN, ROPE_DIM), F32),
        'c': jax.random.normal(ks[4], (DEC_BATCH, D), F32),
        'c_ctx': jax.random.normal(ks[5], (D,), F32),
        'ada_w': nrm(ks[6], (DEPTH, D, 6 * D), D, 0.5),
        'ada_b': 0.02 * jax.random.normal(ks[7], (DEPTH, 6 * D), F32),
        'norm_mix_w': gain(ks[8], (DEPTH, D)),
        'norm_ffn_w': gain(ks[9], (DEPTH, D)),
        'mla_w_dq': nrm(ks[10], (N_MLA, D, Q_LORA), D),
        'mla_q_lora_norm': gain(ks[11], (N_MLA, Q_LORA)),
        'mla_w_uq': nrm(ks[12], (N_MLA, Q_LORA, N_HEADS * QK_HEAD), Q_LORA),
        'mla_w_dkv': nrm(ks[13], (N_MLA, D, KV_LORA + ROPE_DIM), D),
        'mla_kv_lora_norm': gain(ks[14], (N_MLA, KV_LORA)),
        'mla_w_uk': nrm(ks[15], (N_MLA, KV_LORA, N_HEADS * QK_NOPE), KV_LORA),
        'mla_w_uv': nrm(ks[16], (N_MLA, KV_LORA, N_HEADS * V_HEAD), KV_LORA),
        'mla_q_norm': gain(ks[17], (N_MLA, QK_HEAD)),
        'mla_k_norm': gain(ks[18], (N_MLA, QK_HEAD)),
        'mla_w_o': nrm(ks[19], (N_MLA, N_HEADS * V_HEAD, D), N_HEADS * V_HEAD),
        'pool_w': nrm(ks[20], (N_POOL, POOL_GROUPS, POOL_GROUP, POOL_GROUP), POOL_GROUP),
        'pool_scale': gain(ks[21], (N_POOL, D)),
        'router_w': nrm(ks[22], (D, N_EXPERTS), D),
        'router_bias': 0.01 * jax.random.normal(ks[23], (N_EXPERTS,), F32),
        'moe_w_gate': nrm(ks[24], (DEPTH, N_EXPERTS, D, EXPERT_FF), D),
        'moe_w_up': nrm(ks[25], (DEPTH, N_EXPERTS, D, EXPERT_FF), D),
        'moe_w_down': nrm(ks[26], (DEPTH, N_EXPERTS, EXPERT_FF, D), EXPERT_FF),
    }


def reference(x_prompt, x_sample, cache_ckv, cache_krope, c, c_ctx, ada_w, ada_b,
              norm_mix_w, norm_ffn_w, mla_w_dq, mla_q_lora_norm, mla_w_uq, mla_w_dkv,
              mla_kv_lora_norm, mla_w_uk, mla_w_uv, mla_q_norm, mla_k_norm, mla_w_o,
              pool_w, pool_scale, router_w, router_bias, moe_w_gate, moe_w_up, moe_w_down):
    xp = x_prompt
    ckv_list, krope_list = [], []
    for i in range(DEPTH):
        j = i // N_MIXERS
        sh1, sc1, g1, sh2, sc2, g2 = adaln(c_ctx, ada_w[i], ada_b[i])
        h = rms_norm(xp, norm_mix_w[i]) * (1 + sc1) + sh1
        if i % N_MIXERS == 0:
            ckv, krope = mla_compress(h, mla_w_dkv[j], mla_kv_lora_norm[j])
            ckv_list.append(ckv)
            krope_list.append(krope)
            q = mla_queries(h, mla_w_dq[j], mla_q_lora_norm[j], mla_w_uq[j], mla_q_norm[j])
            k, v = mla_expand(ckv, krope, mla_w_uk[j], mla_w_uv[j], mla_k_norm[j])
            mix = attend(q, k, v) @ mla_w_o[j]
        else:
            mix = multiscale_pool(h, pool_w[j], pool_scale[j])
        xp = xp + g1 * mix
        h = rms_norm(xp, norm_ffn_w[i]) * (1 + sc2) + sh2
        xp = xp + g2 * moe(h, router_w, router_bias, moe_w_gate[i], moe_w_up[i], moe_w_down[i])
    state_ckv = jnp.stack(ckv_list, axis=1)
    state_krope = jnp.stack(krope_list, axis=1)

    cos, sin = axial_rope(x_sample.shape[1])
    xs = x_sample
    for i in range(DEPTH):
        j = i // N_MIXERS
        sh1, sc1, g1, sh2, sc2, g2 = adaln(c, ada_w[i], ada_b[i])
        h = rms_norm(xs, norm_mix_w[i]) * (1 + sc1) + sh1
        if i % N_MIXERS == 0:
            k_c, v_c = mla_expand(cache_ckv[:, j], cache_krope[:, j], mla_w_uk[j], mla_w_uv[j], mla_k_norm[j])
            q = apply_rope(mla_queries(h, mla_w_dq[j], mla_q_lora_norm[j], mla_w_uq[j], mla_q_norm[j]), cos, sin)
            ckv, krope = mla_compress(h, mla_w_dkv[j], mla_kv_lora_norm[j])
            k_l, v_l = mla_expand(ckv, krope, mla_w_uk[j], mla_w_uv[j], mla_k_norm[j])
            k_l = apply_rope(k_l, cos, sin)
            k_all = jnp.concatenate([k_c, k_l], axis=1)
            v_all = jnp.concatenate([v_c, v_l], axis=1)
            mix = attend(q, k_all, v_all) @ mla_w_o[j]
        else:
            mix = multiscale_pool(h, pool_w[j], pool_scale[j])
        xs = xs + g1 * mix
        h = rms_norm(xs, norm_ffn_w[i]) * (1 + sc2) + sh2
        xs = xs + g2 * moe(h, router_w, router_bias, moe_w_gate[i], moe_w_up[i], moe_w_down[i])

    return (xp, xs, state_ckv, state_krope)
```

```python
import functools

import jax
import jax.numpy as jnp
from jax import lax
from jax.experimental import pallas as pl
from jax.experimental.pallas import tpu as pltpu

F32 = jnp.float32
BF16 = jnp.bfloat16
U32 = jnp.uint32
I32 = jnp.int32

GRID_W = 64
N_HEADS = 16
QK_NOPE = 128
ROPE_DIM = 64
QK_HEAD = QK_NOPE + ROPE_DIM
V_HEAD = 128
KV_LORA = 256
ROPE_THETA = 10000.0
AXIS_DIM = ROPE_DIM // 2
ATTN_SCALE = QK_HEAD ** -0.5
POOL_WINDOWS = (2, 4, 8, 16)
N_EXPERTS = 16
N_GROUPS = 4
EXPERTS_PER_GROUP = N_EXPERTS // N_GROUPS
EPS = 1e-6

LANES = 128
HEAD_PAD = 2 * LANES
TM = 256
HALO = 16
N_PAIRS = 6
N_CLASSES = N_GROUPS * N_PAIRS
PAIR_LO = (0, 0, 0, 1, 1, 2)
PAIR_HI = (1, 2, 3, 2, 3, 3)
GATHER_ROWS = 512
VMEM_LIMIT = 52 * 1024 * 1024


def _params(n_axes):
    return pltpu.CompilerParams(
        dimension_semantics=("arbitrary",) * n_axes, vmem_limit_bytes=VMEM_LIMIT)


def _rms(x):
    return x * lax.rsqrt(jnp.mean(x * x, axis=-1, keepdims=True) + EPS)


def _pack_bf16_pairs(x):
    n = x.shape[1] // 2
    lo = pltpu.bitcast(x[:, :n].astype(BF16).astype(F32), U32)
    hi = pltpu.bitcast(x[:, n:].astype(BF16).astype(F32), U32)
    return (lo >> 16) | hi


def _unpack_lo(w):
    return pltpu.bitcast(w << 16, F32)


def _unpack_hi(w):
    return pltpu.bitcast(w & jnp.uint32(0xFFFF0000), F32)


def _unpack_f32(w):
    return jnp.concatenate([_unpack_lo(w), _unpack_hi(w)], axis=1)


def _ada_kernel(cond_ref, w_ref, b_ref, o_ref):
    c = cond_ref[...]
    s = c * jax.nn.sigmoid(c)
    o_ref[...] = jnp.dot(s, w_ref[...], preferred_element_type=F32,
                         precision=lax.Precision.HIGHEST) + b_ref[...]


def _adaln(conds, ada_w, ada_b):
    depth, d, n = ada_w.shape
    tn = 1024
    return pl.pallas_call(
        _ada_kernel,
        grid=(depth, n // tn),
        in_specs=[
            pl.BlockSpec((8, d), lambda l, j: (0, 0)),
            pl.BlockSpec((None, d, tn), lambda l, j: (l, 0, j)),
            pl.BlockSpec((None, 1, tn), lambda l, j: (l, 0, j)),
        ],
        out_specs=pl.BlockSpec((None, 8, tn), lambda l, j: (l, 0, j)),
        out_shape=jax.ShapeDtypeStruct((depth, 8, n), F32),
        compiler_params=_params(2),
        name="adaln",
    )(conds, ada_w, ada_b.reshape(depth, 1, n))


def _mla_in_kernel(x_ref, mod_ref, nw_ref, wdq_ref, qln_ref, wuq_ref, wdkv_ref, kvn_ref,
                   qnn_ref, tq_ref, q_ref, ckv_ref, kr_ref):
    x = x_ref[...]
    h = _rms(x) * nw_ref[...] * (1.0 + mod_ref[1:2, :]) + mod_ref[0:1, :]
    hb = h.astype(BF16)
    kv = jnp.dot(hb, wdkv_ref[...], preferred_element_type=F32)
    ckv_ref[...] = _rms(kv[:, :KV_LORA]) * kvn_ref[...]
    kr_ref[...] = kv[:, KV_LORA:]
    cq = jnp.dot(hb, wdq_ref[...], preferred_element_type=F32)
    cqn = (_rms(cq) * qln_ref[...]).astype(BF16)
    q = jnp.dot(cqn, wuq_ref[...], preferred_element_type=F32)
    tq = tq_ref[...]
    qnn = qnn_ref[...]
    rope_lane = lax.broadcasted_iota(I32, (1, LANES), 1) < ROPE_DIM
    for hd in range(N_HEADS):
        a = q[:, hd * HEAD_PAD: hd * HEAD_PAD + LANES]
        b = q[:, hd * HEAD_PAD + LANES: (hd + 1) * HEAD_PAD]
        ss = (jnp.sum(a * a, axis=-1, keepdims=True)
              + jnp.sum(jnp.where(rope_lane, b * b, 0.0), axis=-1, keepdims=True))
        r = lax.rsqrt(ss * (1.0 / QK_HEAD) + EPS)
        bq = b * tq * r
        rot = bq + pltpu.roll(bq, ROPE_DIM, axis=1)
        q_ref[:, hd * HEAD_PAD: hd * HEAD_PAD + LANES] = (a * r * qnn).astype(BF16)
        q_ref[:, hd * HEAD_PAD + LANES: (hd + 1) * HEAD_PAD] = rot.astype(BF16)


def _mla_in(x, mods, cond_of_tile, table_of_tile, nw, wdq, qln, wuq, wdkv, kvn, qnn, tq):
    t, d = x.shape
    nt = t // TM
    const = lambda i: (0, 0)
    return pl.pallas_call(
        _mla_in_kernel,
        grid=(nt,),
        in_specs=[
            pl.BlockSpec((TM, d), lambda i: (i, 0)),
            pl.BlockSpec((None, 6, d), lambda i: (cond_of_tile(i), 0, 0)),
            pl.BlockSpec((1, d), const),
            pl.BlockSpec(wdq.shape, const),
            pl.BlockSpec((1, wdq.shape[1]), const),
            pl.BlockSpec(wuq.shape, const),
            pl.BlockSpec(wdkv.shape, const),
            pl.BlockSpec((1, KV_LORA), const),
            pl.BlockSpec((1, LANES), const),
            pl.BlockSpec((TM, LANES), lambda i: (table_of_tile(i), 0)),
        ],
        out_specs=[
            pl.BlockSpec((TM, N_HEADS * HEAD_PAD), lambda i: (i, 0)),
            pl.BlockSpec((TM, KV_LORA), lambda i: (i, 0)),
            pl.BlockSpec((TM, LANES), lambda i: (i, 0)),
        ],
        out_shape=[
            jax.ShapeDtypeStruct((t, N_HEADS * HEAD_PAD), BF16),
            jax.ShapeDtypeStruct((t, KV_LORA), F32),
            jax.ShapeDtypeStruct((t, LANES), F32),
        ],
        compiler_params=_params(1),
        name="mla_in",
    )(x, mods, nw, wdq, qln, wuq, wdkv, kvn, qnn, tq)


def _kv_expand_kernel(ckv_ref, kr_ref, tk_ref, wuk_ref, wuv_ref, knn_ref, k_ref, v_ref):
    cb = ckv_ref[...].astype(BF16)
    kn = jnp.dot(cb, wuk_ref[...], preferred_element_type=F32)
    v_ref[...] = jnp.dot(cb, wuv_ref[...], preferred_element_type=F32).astype(BF16)
    kr = kr_ref[...]
    rope_lane = lax.broadcasted_iota(I32, (1, LANES), 1) < ROPE_DIM
    ssr = jnp.sum(jnp.where(rope_lane, kr * kr, 0.0), axis=-1, keepdims=True)
    bk = kr * tk_ref[...]
    rot = jnp.where(rope_lane, bk + pltpu.roll(bk, ROPE_DIM, axis=1), 0.0)
    knn = knn_ref[...]
    for hd in range(N_HEADS):
        a = kn[:, hd * LANES: (hd + 1) * LANES]
        r = lax.rsqrt((jnp.sum(a * a, axis=-1, keepdims=True) + ssr) * (1.0 / QK_HEAD) + EPS)
        k_ref[:, hd * HEAD_PAD: hd * HEAD_PAD + LANES] = (a * r * knn).astype(BF16)
        k_ref[:, hd * HEAD_PAD + LANES: (hd + 1) * HEAD_PAD] = (rot * r).astype(BF16)


def _kv_expand(ckv, kr, tk, table_of_tile, wuk, wuv, knn):
    t = ckv.shape[0]
    const = lambda i: (0, 0)
    return pl.pallas_call(
        _kv_expand_kernel,
        grid=(t // TM,),
        in_specs=[
            pl.BlockSpec((TM, KV_LORA), lambda i: (i, 0)),
            pl.BlockSpec((TM, LANES), lambda i: (i, 0)),
            pl.BlockSpec((TM, LANES), lambda i: (table_of_tile(i), 0)),
            pl.BlockSpec(wuk.shape, const),
            pl.BlockSpec(wuv.shape, const),
            pl.BlockSpec((1, LANES), const),
        ],
        out_specs=[
            pl.BlockSpec((TM, N_HEADS * HEAD_PAD), lambda i: (i, 0)),
            pl.BlockSpec((TM, N_HEADS * V_HEAD), lambda i: (i, 0)),
        ],
        out_shape=[
            jax.ShapeDtypeStruct((t, N_HEADS * HEAD_PAD), BF16),
            jax.ShapeDtypeStruct((t, N_HEADS * V_HEAD), BF16),
        ],
        compiler_params=_params(1),
        name="kv_expand",
    )(ckv, kr, tk, wuk, wuv, knn)


def _attn_kernel(q_ref, k_ref, v_ref, o_ref, *, heads, tk, nk):
    tq = q_ref.shape[0]
    for hh in range(heads):
        q = q_ref[:, hh * HEAD_PAD:(hh + 1) * HEAD_PAD]

        def body(j, carry, hh=hh, q=q):
            m, l, acc = carry
            start = pl.multiple_of(j * tk, tk)
            ks = k_ref[pl.ds(start, tk), hh * HEAD_PAD:(hh + 1) * HEAD_PAD]
            vs = v_ref[pl.ds(start, tk), hh * V_HEAD:(hh + 1) * V_HEAD]
            s = lax.dot_general(q, ks, (((1,), (1,)), ((), ())), preferred_element_type=F32)
            mn = jnp.maximum(m, jnp.max(s, axis=-1, keepdims=True))
            alpha = jnp.exp(m - mn)
            p = jnp.exp(s - mn)
            l = alpha * l + jnp.sum(p, axis=-1, keepdims=True)
            acc = alpha * acc + jnp.dot(p.astype(BF16), vs, preferred_element_type=F32)
            return mn, l, acc

        init = (jnp.full((tq, 1), -jnp.inf, F32), jnp.zeros((tq, 1), F32),
                jnp.zeros((tq, V_HEAD), F32))
        _, l, acc = lax.fori_loop(0, nk, body, init)
        o_ref[:, hh * V_HEAD:(hh + 1) * V_HEAD] = (acc / l).astype(BF16)


def _attention(q, k, v, q_row0, n_batch, q_len, kv_len, tq, heads, tk):
    nq = q_len // tq
    hb = N_HEADS // heads
    qb0 = q_row0 // tq
    return pl.pallas_call(
        functools.partial(_attn_kernel, heads=heads, tk=tk, nk=kv_len // tk),
        grid=(n_batch, hb, nq),
        in_specs=[
            pl.BlockSpec((tq, heads * HEAD_PAD), lambda b, h, i: (qb0 + b * nq + i, h)),
            pl.BlockSpec((kv_len, heads * HEAD_PAD), lambda b, h, i: (b, h)),
            pl.BlockSpec((kv_len, heads * V_HEAD), lambda b, h, i: (b, h)),
        ],
        out_specs=pl.BlockSpec((tq, heads * V_HEAD), lambda b, h, i: (b * nq + i, h)),
        out_shape=jax.ShapeDtypeStruct((n_batch * q_len, N_HEADS * V_HEAD), BF16),
        compiler_params=_params(3),
        name="attention",
    )(q, k, v)


def _route(h2, rwh_ref, rwl_ref, rb_ref, route_ref):
    hi = h2.astype(BF16)
    lo = (h2 - hi.astype(F32)).astype(BF16)
    rwh = rwh_ref[...]
    logits = (jnp.dot(hi, rwh, preferred_element_type=F32)
              + jnp.dot(lo, rwh, preferred_element_type=F32)
              + jnp.dot(hi, rwl_ref[...], preferred_element_type=F32))
    lt = logits.T[:N_EXPERTS, :]
    scores = jax.nn.sigmoid(lt)
    sel = scores + rb_ref[...]
    srow = [sel[e:e + 1, :] for e in range(N_EXPERTS)]
    prow = [scores[e:e + 1, :] for e in range(N_EXPERTS)]

    def top2_sum(a, b, c, d):
        hab, lab = jnp.maximum(a, b), jnp.minimum(a, b)
        hcd, lcd = jnp.maximum(c, d), jnp.minimum(c, d)
        return jnp.maximum(hab, hcd) + jnp.maximum(jnp.minimum(hab, hcd), jnp.maximum(lab, lcd))

    gs = [top2_sum(*srow[4 * g:4 * g + 4]) for g in range(N_GROUPS)]
    best = jnp.zeros_like(gs[0], dtype=I32)
    bestv = gs[0]
    for g in range(1, N_GROUPS):
        upd = gs[g] > bestv
        best = jnp.where(upd, g, best)
        bestv = jnp.where(upd, gs[g], bestv)

    def pick(rows, j):
        out = rows[j]
        for g in range(1, N_GROUPS):
            out = jnp.where(best == g, rows[4 * g + j], out)
        return out

    sv = [pick(srow, j) for j in range(EXPERTS_PER_GROUP)]
    pv = [pick(prow, j) for j in range(EXPERTS_PER_GROUP)]
    i1 = jnp.zeros_like(best)
    v1 = sv[0]
    for j in range(1, EXPERTS_PER_GROUP):
        upd = sv[j] > v1
        i1 = jnp.where(upd, j, i1)
        v1 = jnp.where(upd, sv[j], v1)
    neg = jnp.float32(-jnp.inf)
    i2 = jnp.where(i1 == 0, 1, 0).astype(I32)
    v2 = jnp.where(i1 == 0, sv[1], sv[0])
    for j in range(1, EXPERTS_PER_GROUP):
        cand = jnp.where(i1 == j, neg, sv[j])
        upd = cand > v2
        i2 = jnp.where(upd, j, i2)
        v2 = jnp.where(upd, cand, v2)
    ilo = jnp.minimum(i1, i2)
    ihi = jnp.maximum(i1, i2)
    pair = jnp.where(ilo == 0, ihi - 1, jnp.where(ilo == 1, ihi + 1, N_PAIRS - 1))
    cls = best * N_PAIRS + pair

    def take(vals, idx):
        out = vals[0]
        for j in range(1, EXPERTS_PER_GROUP):
            out = jnp.where(idx == j, vals[j], out)
        return out

    wlo = take(pv, ilo)
    whi = take(pv, ihi)
    den = wlo + whi
    route_ref[...] = jnp.concatenate(
        [cls.astype(F32), wlo / den, whi / den, jnp.zeros((5, cls.shape[1]), F32)], axis=0)


def _finish_sublayer(x_new, mod_ref, nw2_ref, rwh_ref, rwl_ref, rb_ref, x_out_ref, h2p_ref, route_ref):
    x_out_ref[...] = x_new
    h2 = _rms(x_new) * nw2_ref[...] * (1.0 + mod_ref[4:5, :]) + mod_ref[3:4, :]
    h2p_ref[...] = _pack_bf16_pairs(h2)
    _route(h2, rwh_ref, rwl_ref, rb_ref, route_ref)


def _mla_out_kernel(attn_c_ref, attn_l_ref, x_ref, mod_ref, wo_ref, nw2_ref, rwh_ref, rwl_ref, rb_ref,
                    x1_ref, h2p_ref, route_ref, *, n_ctx_tiles):
    attn = jnp.where(pl.program_id(0) < n_ctx_tiles, attn_c_ref[...], attn_l_ref[...])
    mix = jnp.dot(attn, wo_ref[...], preferred_element_type=F32)
    x1 = x_ref[...] + mod_ref[2:3, :] * mix
    _finish_sublayer(x1, mod_ref, nw2_ref, rwh_ref, rwl_ref, rb_ref, x1_ref, h2p_ref, route_ref)


def _sublayer_out_specs(t, d):
    nt = t // TM
    specs = [
        pl.BlockSpec((TM, d), lambda i: (i, 0)),
        pl.BlockSpec((TM, d // 2), lambda i: (i, 0)),
        pl.BlockSpec((None, 8, TM), lambda i: (i, 0, 0)),
    ]
    shapes = [
        jax.ShapeDtypeStruct((t, d), F32),
        jax.ShapeDtypeStruct((t, d // 2), U32),
        jax.ShapeDtypeStruct((nt, 8, TM), F32),
    ]
    return specs, shapes


def _mla_out(attn_ctx, attn_lat, x, mods, cond_of_tile, wo, nw2, rwh, rwl, rb):
    t, d = x.shape
    const = lambda i: (0, 0)
    n_ctx_tiles = attn_ctx.shape[0] // TM
    out_specs, out_shape = _sublayer_out_specs(t, d)
    return pl.pallas_call(
        functools.partial(_mla_out_kernel, n_ctx_tiles=n_ctx_tiles),
        grid=(t // TM,),
        in_specs=[
            pl.BlockSpec((TM, attn_ctx.shape[1]), lambda i: (jnp.minimum(i, n_ctx_tiles - 1), 0)),
            pl.BlockSpec((TM, attn_lat.shape[1]), lambda i: (jnp.maximum(i - n_ctx_tiles, 0), 0)),
            pl.BlockSpec((TM, d), lambda i: (i, 0)),
            pl.BlockSpec((None, 6, d), lambda i: (cond_of_tile(i), 0, 0)),
            pl.BlockSpec(wo.shape, const),
            pl.BlockSpec((1, d), const),
            pl.BlockSpec(rwh.shape, const),
            pl.BlockSpec(rwl.shape, const),
            pl.BlockSpec(rb.shape, const),
        ],
        out_specs=out_specs,
        out_shape=out_shape,
        compiler_params=_params(1),
        name="mla_out",
    )(attn_ctx, attn_lat, x, mods, wo, nw2, rwh, rwl, rb)


def _pool_kernel(x_ref, mo_ref, xp_ref, mop_ref, xn_ref, mon_ref, modp_ref, mod_ref, nw_ref, pw_ref,
                 ps_ref, nw2_ref, rwh_ref, rwl_ref, rb_ref, x2_ref, h2p_ref, route_ref,
                 *, n_ctx_tiles, ctx_seq_tiles, lat_seq_tiles):
    i = pl.program_id(0)
    is_lat = i >= n_ctx_tiles
    seq_tiles = jnp.where(is_lat, lat_seq_tiles, ctx_seq_tiles)
    in_seq = jnp.where(is_lat, i - n_ctx_tiles, i) % seq_tiles
    has_prev = in_seq > 0
    has_next = in_seq < seq_tiles - 1
    g2p = modp_ref[5:6, :]
    nw = nw_ref[...]
    sc1 = 1.0 + mod_ref[1:2, :]
    sh1 = mod_ref[0:1, :]

    def pre(xv, mov):
        xx = xv + g2p * _unpack_f32(mov)
        return xx, _rms(xx) * nw * sc1 + sh1

    xcur, h = pre(x_ref[...], mo_ref[...])
    _, hprev = pre(xp_ref[...], mop_ref[...])
    _, hnext = pre(xn_ref[...], mon_ref[...])
    d = h.shape[1]
    gw = d // len(POOL_WINDOWS)
    hb = h.astype(BF16)
    halo = jnp.concatenate(
        [hprev.astype(BF16), hnext.astype(BF16), jnp.zeros((LANES - 2 * HALO, d), BF16)], axis=0)

    t_mid = lax.broadcasted_iota(I32, (TM, TM), 0)
    e_mid = lax.broadcasted_iota(I32, (TM, TM), 1)
    t_hal = lax.broadcasted_iota(I32, (TM, LANES), 0)
    c_hal = lax.broadcasted_iota(I32, (TM, LANES), 1)
    far = jnp.int32(4 * TM)
    pos_prev = jnp.where(has_prev, c_hal - HALO, -far)
    pos_next = jnp.where(has_next, TM + c_hal - HALO, far)
    pos_hal = jnp.where(c_hal < HALO, pos_prev, jnp.where(c_hal < 2 * HALO, pos_next, far))
    t_col = lax.broadcasted_iota(I32, (TM, 1), 0)
    lo_bound = jnp.where(has_prev, -HALO, 0)
    hi_bound = jnp.where(has_next, TM - 1 + HALO, TM - 1)

    ys = []
    for g, w in enumerate(POOL_WINDOWS):
        half = w // 2
        dm = e_mid - t_mid
        band_mid = jnp.where(dm >= -half, jnp.where(dm <= half - 1, 1.0, 0.0), 0.0).astype(BF16)
        dh = pos_hal - t_hal
        band_hal = jnp.where(dh >= -half, jnp.where(dh <= half - 1, 1.0, 0.0), 0.0).astype(BF16)
        sl = slice(g * gw, (g + 1) * gw)
        wsum = (jnp.dot(band_mid, hb[:, sl], preferred_element_type=F32)
                + jnp.dot(band_hal, halo[:, sl], preferred_element_type=F32))
        cnt = (jnp.minimum(t_col + (half - 1), hi_bound)
               - jnp.maximum(t_col - half, lo_bound) + 1).astype(F32)
        pooled = wsum / cnt - h[:, sl]
        ys.append(jnp.dot(pooled.astype(BF16), pw_ref[g], preferred_element_type=F32))
    y = jnp.concatenate(ys, axis=1) * ps_ref[...]
    x2 = xcur + mod_ref[2:3, :] * y
    _finish_sublayer(x2, mod_ref, nw2_ref, rwh_ref, rwl_ref, rb_ref, x2_ref, h2p_ref, route_ref)


def _pool_layer(x1, mo, mods_prev, mods, cond_of_tile, n_ctx_tiles, ctx_seq_tiles, lat_seq_tiles,
                nw, pw, ps, nw2, rwh, rwl, rb):
    t, d = x1.shape
    const = lambda i: (0, 0)
    hb = TM // HALO
    last = t // HALO - 1
    prev_map = lambda i: (jnp.maximum(i * hb - 1, 0), 0)
    next_map = lambda i: (jnp.minimum((i + 1) * hb, last), 0)
    out_specs, out_shape = _sublayer_out_specs(t, d)
    return pl.pallas_call(
        functools.partial(_pool_kernel, n_ctx_tiles=n_ctx_tiles, ctx_seq_tiles=ctx_seq_tiles,
                          lat_seq_tiles=lat_seq_tiles),
        grid=(t // TM,),
        in_specs=[
            pl.BlockSpec((TM, d), lambda i: (i, 0)),
            pl.BlockSpec((TM, d // 2), lambda i: (i, 0)),
            pl.BlockSpec((HALO, d), prev_map),
            pl.BlockSpec((HALO, d // 2), prev_map),
            pl.BlockSpec((HALO, d), next_map),
            pl.BlockSpec((HALO, d // 2), next_map),
            pl.BlockSpec((None, 6, d), lambda i: (cond_of_tile(i), 0, 0)),
            pl.BlockSpec((None, 6, d), lambda i: (cond_of_tile(i), 0, 0)),
            pl.BlockSpec((1, d), const),
            pl.BlockSpec(pw.shape, lambda i: (0, 0, 0)),
            pl.BlockSpec((1, d), const),
            pl.BlockSpec((1, d), const),
            pl.BlockSpec(rwh.shape, const),
            pl.BlockSpec(rwl.shape, const),
            pl.BlockSpec(rb.shape, const),
        ],
        out_specs=out_specs,
        out_shape=out_shape,
        compiler_params=_params(1),
        name="pool_layer",
    )(x1, mo, x1, mo, x1, mo, mods_prev, mods, nw, pw, ps, nw2, rwh, rwl, rb)


def _final_kernel(x_ref, mo_ref, mod_ref, o_ref):
    o_ref[...] = x_ref[...] + mod_ref[5:6, :] * _unpack_f32(mo_ref[...])


def _final(x, mo, mods, cond_of_tile):
    t, d = x.shape
    return pl.pallas_call(
        _final_kernel,
        grid=(t // TM,),
        in_specs=[
            pl.BlockSpec((TM, d), lambda i: (i, 0)),
            pl.BlockSpec((TM, d // 2), lambda i: (i, 0)),
            pl.BlockSpec((None, 6, d), lambda i: (cond_of_tile(i), 0, 0)),
        ],
        out_specs=pl.BlockSpec((TM, d), lambda i: (i, 0)),
        out_shape=jax.ShapeDtypeStruct((t, d), F32),
        compiler_params=_params(1),
        name="final_residual",
    )(x, mo, mods)


def _gather_kernel(idx_ref, src_ref, dst_ref, sem):
    base = pl.program_id(0) * GATHER_ROWS

    def issue(r, carry):
        row = idx_ref[base + r]
        pltpu.make_async_copy(src_ref.at[pl.ds(row, 1)], dst_ref.at[pl.ds(base + r, 1)], sem).start()
        return carry

    lax.fori_loop(0, GATHER_ROWS, issue, 0)

    def drain(r, carry):
        pltpu.make_async_copy(src_ref.at[pl.ds(0, 1)], dst_ref.at[pl.ds(base, 1)], sem).wait()
        return carry

    lax.fori_loop(0, GATHER_ROWS, drain, 0)


def _gather_rows(src, idx):
    n = idx.shape[0]
    return pl.pallas_call(
        _gather_kernel,
        grid_spec=pltpu.PrefetchScalarGridSpec(
            num_scalar_prefetch=1,
            grid=(n // GATHER_ROWS,),
            in_specs=[pl.BlockSpec(memory_space=pl.ANY)],
            out_specs=pl.BlockSpec(memory_space=pl.ANY),
            scratch_shapes=[pltpu.SemaphoreType.DMA(())],
        ),
        out_shape=jax.ShapeDtypeStruct((n, src.shape[1]), src.dtype),
        compiler_params=_params(1),
        name="gather_rows",
    )(idx, src)


def _moe_kernel(ea_ref, eb_ref, valid_ref, xs_ref, g_ref, wga_ref, wua_ref, wda_ref,
                wgb_ref, wub_ref, wdb_ref, y_ref):
    i = pl.program_id(0)

    @pl.when(valid_ref[i] == 1)
    def _():
        w = xs_ref[...]
        xlo = _unpack_lo(w).astype(BF16)
        xhi = _unpack_hi(w).astype(BF16)
        half = xlo.shape[1]

        def ffn(wg_ref, wu_ref, wd_ref):
            g = (jnp.dot(xlo, wg_ref[:half, :], preferred_element_type=F32)
                 + jnp.dot(xhi, wg_ref[half:, :], preferred_element_type=F32))
            u = (jnp.dot(xlo, wu_ref[:half, :], preferred_element_type=F32)
                 + jnp.dot(xhi, wu_ref[half:, :], preferred_element_type=F32))
            a = (g * jax.nn.sigmoid(g)) * u
            return jnp.dot(a.astype(BF16), wd_ref[...], preferred_element_type=F32)

        gates = g_ref[...]
        y = gates[:, 0:1] * ffn(wga_ref, wua_ref, wda_ref) + gates[:, 1:2] * ffn(wgb_ref, wub_ref, wdb_ref)
        y_ref[...] = _pack_bf16_pairs(y)

    @pl.when(valid_ref[i] == 0)
    def _():
        y_ref[...] = jnp.zeros_like(y_ref)


def _moe_sorted(xs, gates, tile_ea, tile_eb, tile_valid, layer, wg, wu, wd):
    rows, half = xs.shape
    d = 2 * half
    ff = wg.shape[-1]
    wa = lambda i, ea, eb, va: (layer, ea[i], 0, 0)
    wb = lambda i, ea, eb, va: (layer, eb[i], 0, 0)
    row = lambda i, ea, eb, va: (i, 0)
    return pl.pallas_call(
        _moe_kernel,
        grid_spec=pltpu.PrefetchScalarGridSpec(
            num_scalar_prefetch=3,
            grid=(rows // TM,),
            in_specs=[
                pl.BlockSpec((TM, half), row),
                pl.BlockSpec((TM, 2), row),
                pl.BlockSpec((None, None, d, ff), wa),
                pl.BlockSpec((None, None, d, ff), wa),
                pl.BlockSpec((None, None, ff, d), wa),
                pl.BlockSpec((None, None, d, ff), wb),
                pl.BlockSpec((None, None, d, ff), wb),
                pl.BlockSpec((None, None, ff, d), wb),
            ],
            out_specs=pl.BlockSpec((TM, half), row),
        ),
        out_shape=jax.ShapeDtypeStruct((rows, half), U32),
        compiler_params=_params(1),
        name="moe_experts",
    )(tile_ea, tile_eb, tile_valid, xs, gates, wg, wu, wd, wg, wu, wd)


def _moe_layer(h2p, route, layer, wg, wu, wd):
    t = h2p.shape[0]
    n_tiles = t // TM + N_CLASSES
    n_tiles += (-n_tiles) % (GATHER_ROWS // TM)
    cls = route[:, 0, :].reshape(t).astype(I32)
    g_lo = route[:, 1, :].reshape(t)
    g_hi = route[:, 2, :].reshape(t)
    onehot = (cls[:, None] == jnp.arange(N_CLASSES, dtype=I32)[None, :]).astype(I32)
    csum = jnp.cumsum(onehot, axis=0)
    counts = csum[-1]
    rank = jnp.take_along_axis(csum, cls[:, None], axis=1)[:, 0] - 1
    tiles_c = (counts + TM - 1) // TM
    tile_end = jnp.cumsum(tiles_c)
    tile_start = tile_end - tiles_c
    pos = tile_start[cls] * TM + rank
    row_tok = jnp.zeros((n_tiles * TM,), I32).at[pos].set(jnp.arange(t, dtype=I32))
    tile_ids = jnp.arange(n_tiles, dtype=I32)
    total = tile_end[-1]
    tile_valid = (tile_ids < total).astype(I32)
    tile_cls = jnp.searchsorted(tile_end, jnp.minimum(tile_ids, total - 1), side="right").astype(I32)
    pair = tile_cls % N_PAIRS
    group = tile_cls // N_PAIRS
    tile_ea = group * EXPERTS_PER_GROUP + jnp.asarray(PAIR_LO, I32)[pair]
    tile_eb = group * EXPERTS_PER_GROUP + jnp.asarray(PAIR_HI, I32)[pair]
    gates = jnp.stack([g_lo[row_tok], g_hi[row_tok]], axis=1)
    xs = _gather_rows(h2p, row_tok)
    ys = _moe_sorted(xs, gates, tile_ea, tile_eb, tile_valid, layer, wg, wu, wd)
    pos_pad = jnp.concatenate([pos, jnp.zeros(((-t) % GATHER_ROWS,), I32)])
    return _gather_rows(ys, pos_pad)[:t]


def _swap_rope_halves(w):
    shp = w.shape
    return w.reshape(shp[:-1] + (2, 2, AXIS_DIM // 2))[..., ::-1, :].reshape(shp)


def _rope_tables(length, norm_rope, scale):
    rows = length // GRID_W
    row = jnp.repeat(jnp.arange(rows, dtype=F32), GRID_W)
    col = jnp.tile(jnp.arange(GRID_W, dtype=F32), rows)
    inv = jnp.power(ROPE_THETA, -jnp.arange(0, AXIS_DIM, 2, dtype=F32) / AXIS_DIM)
    ang = jnp.stack([row[:, None] * inv, col[:, None] * inv], axis=1)
    cos, sin = jnp.cos(ang), jnp.sin(ang)
    c_full = jnp.stack([cos, cos], axis=2).reshape(length, ROPE_DIM)
    s_full = jnp.stack([-sin, sin], axis=2).reshape(length, ROPE_DIM)
    lat = jnp.concatenate([norm_rope * c_full, _swap_rope_halves(norm_rope) * s_full], axis=1)
    ctx = jnp.concatenate([norm_rope, jnp.zeros((ROPE_DIM,), F32)])
    ctx = jnp.broadcast_to(ctx[None, :], (TM, 2 * ROPE_DIM))
    return jnp.concatenate([ctx, lat], axis=0) * scale


def kernel(x_prompt, x_sample, cache_ckv, cache_krope, c, c_ctx, ada_w, ada_b, norm_mix_w, norm_ffn_w,
           mla_w_dq, mla_q_lora_norm, mla_w_uq, mla_w_dkv, mla_kv_lora_norm, mla_w_uk, mla_w_uv,
           mla_q_norm, mla_k_norm, mla_w_o, pool_w, pool_scale, router_w, router_bias,
           moe_w_gate, moe_w_up, moe_w_down):
    batch, seq, d = x_prompt.shape
    dec_batch, dec_seq, _ = x_sample.shape
    past = cache_ckv.shape[2]
    depth = ada_w.shape[0]
    assert seq % TM == 0 and dec_seq % TM == 0 and past % TM == 0 and past == TM
    assert dec_batch + 1 <= 8 and depth == 2
    t_ctx = batch * seq
    t_lat = dec_batch * dec_seq
    t_all = t_ctx + t_lat
    n_ctx_tiles = t_ctx // TM
    lat_seq_tiles = dec_seq // TM
    ctx_seq_tiles = seq // TM

    def cond_of_tile(i):
        return jnp.where(i < n_ctx_tiles, 0, 1 + (i - n_ctx_tiles) // lat_seq_tiles)

    def table_of_tile(i):
        return jnp.where(i < n_ctx_tiles, 0, 1 + (i - n_ctx_tiles) % lat_seq_tiles)

    x0 = jnp.concatenate([x_prompt.reshape(t_ctx, d), x_sample.reshape(t_lat, d)], axis=0)
    conds = jnp.concatenate([c_ctx[None, :], c, jnp.zeros((7 - dec_batch, d), F32)], axis=0)
    mods = _adaln(conds, ada_w, ada_b).reshape(depth, 8, 6, d)

    rw = jnp.pad(router_w, ((0, 0), (0, LANES - N_EXPERTS)))
    rwh = rw.astype(BF16)
    rwl = (rw - rwh.astype(F32)).astype(BF16)
    rb = jnp.broadcast_to(router_bias.astype(F32)[:, None], (N_EXPERTS, TM))
    wg = moe_w_gate.astype(BF16)
    wu = moe_w_up.astype(BF16)
    wd = moe_w_down.astype(BF16)

    j = 0
    q_norm, k_norm = mla_q_norm[j], mla_k_norm[j]
    w_uq = mla_w_uq[j].reshape(-1, N_HEADS, QK_HEAD)
    w_uq = jnp.concatenate([w_uq, _swap_rope_halves(w_uq[..., QK_NOPE:])], axis=-1)
    w_uq = w_uq.reshape(-1, N_HEADS * HEAD_PAD).astype(BF16)
    w_dkv = mla_w_dkv[j]
    w_dkv = jnp.concatenate([w_dkv, _swap_rope_halves(w_dkv[:, KV_LORA:])], axis=1).astype(BF16)
    tq = _rope_tables(dec_seq, q_norm[QK_NOPE:], ATTN_SCALE)
    tk = _rope_tables(dec_seq, k_norm[QK_NOPE:], 1.0)
    qnn = (q_norm[:QK_NOPE] * ATTN_SCALE)[None, :]
    knn = k_norm[:QK_NOPE][None, :]

    q, ckv, kr = _mla_in(x0, mods[0], cond_of_tile, table_of_tile, norm_mix_w[0][None, :],
                         mla_w_dq[j].astype(BF16), mla_q_lora_norm[j][None, :], w_uq, w_dkv,
                         mla_kv_lora_norm[j][None, :], qnn, tq)
    state_ckv = ckv[:t_ctx].reshape(batch, 1, seq, KV_LORA)
    state_krope = kr[:t_ctx, :ROPE_DIM].reshape(batch, 1, seq, ROPE_DIM)

    w_uk = mla_w_uk[j].astype(BF16)
    w_uv = mla_w_uv[j].astype(BF16)
    k_ctx, v_ctx = _kv_expand(ckv[:t_ctx], kr[:t_ctx], tk, lambda i: 0, w_uk, w_uv, knn)
    kv_len = past + dec_seq
    ckv_lat = jnp.concatenate(
        [cache_ckv[:, j], ckv[t_ctx:].reshape(dec_batch, dec_seq, KV_LORA)], axis=1)
    kr_cache = jnp.pad(cache_krope[:, j], ((0, 0), (0, 0), (0, LANES - ROPE_DIM)))
    kr_lat = jnp.concatenate([kr_cache, kr[t_ctx:].reshape(dec_batch, dec_seq, LANES)], axis=1)
    kv_seq_tiles = kv_len // TM
    k_lat, v_lat = _kv_expand(ckv_lat.reshape(-1, KV_LORA), kr_lat.reshape(-1, LANES), tk,
                              lambda i: i % kv_seq_tiles, w_uk, w_uv, knn)

    attn_ctx = _attention(q, k_ctx, v_ctx, 0, batch, seq, seq, tq=seq, heads=N_HEADS, tk=seq)
    attn_lat = _attention(q, k_lat, v_lat, t_ctx, dec_batch, dec_seq, kv_len, tq=512, heads=2, tk=TM)

    x1, h2p, route = _mla_out(attn_ctx, attn_lat, x0, mods[0], cond_of_tile, mla_w_o[j].astype(BF16),
                              norm_ffn_w[0][None, :], rwh, rwl, rb)
    mo = _moe_layer(h2p, route, 0, wg, wu, wd)

    x2, h2p, route = _pool_layer(x1, mo, mods[0], mods[1], cond_of_tile, n_ctx_tiles, ctx_seq_tiles,
                                 lat_seq_tiles, norm_mix_w[1][None, :], pool_w[0].astype(BF16),
                                 pool_scale[0][None, :], norm_ffn_w[1][None, :], rwh, rwl, rb)
    mo = _moe_layer(h2p, route, 1, wg, wu, wd)
    y = _final(x2, mo, mods[1], cond_of_tile)

    return (y[:t_ctx].reshape(batch, seq, d), y[t_ctx:].reshape(dec_batch, dec_seq, d),
            state_ckv, state_krope)
```

```python
import functools
import math

import jax
import jax.numpy as jnp
from jax import lax
from jax.experimental import pallas as pl
from jax.experimental.pallas import tpu as pltpu

F32 = jnp.float32
BF16 = jnp.bfloat16
U32 = jnp.uint32
I32 = jnp.int32

GRID_W = 64
N_HEADS = 16
QK_NOPE = 128
ROPE_DIM = 64
QK_HEAD = QK_NOPE + ROPE_DIM
V_HEAD = 128
KV_LORA = 256
ROPE_THETA = 10000.0
AXIS_DIM = ROPE_DIM // 2
ATTN_SCALE = QK_HEAD ** -0.5
POOL_WINDOWS = (2, 4, 8, 16)
N_EXPERTS = 16
N_GROUPS = 4
EXPERTS_PER_GROUP = N_EXPERTS // N_GROUPS
EPS = 1e-6

LANES = 128
HEAD_PAD = 2 * LANES
TM = 256
HALO = 16
N_PAIRS = 6
N_CLASSES = N_GROUPS * N_PAIRS
PAIR_LO = (0, 0, 0, 1, 1, 2)
PAIR_HI = (1, 2, 3, 2, 3, 3)
PACK_CHUNKS = 8
GATHER_ROWS = 512
VMEM_LIMIT = 52 * 1024 * 1024
MAX_SAFE_LOGIT = 64.0


def _params(n_axes):
    return pltpu.CompilerParams(
        dimension_semantics=("arbitrary",) * n_axes, vmem_limit_bytes=VMEM_LIMIT)


def _rms(x):
    return x * lax.rsqrt(jnp.mean(x * x, axis=-1, keepdims=True) + EPS)


def _pack_bf16_pairs(x):
    n = x.shape[1] // 2
    lo = pltpu.bitcast(x[:, :n].astype(BF16).astype(F32), U32)
    hi = pltpu.bitcast(x[:, n:].astype(BF16).astype(F32), U32)
    return (lo >> 16) | hi


def _unpack_lo(w):
    return pltpu.bitcast(w << 16, F32)


def _unpack_hi(w):
    return pltpu.bitcast(w & jnp.uint32(0xFFFF0000), F32)


def _unpack_f32(w):
    return jnp.concatenate([_unpack_lo(w), _unpack_hi(w)], axis=1)


def _store_token_major(ref, words):
    rows = words.shape[0]
    for c in range(PACK_CHUNKS):
        ref[pl.ds(c, rows, stride=PACK_CHUNKS), :] = words[:, c * LANES:(c + 1) * LANES]


def _load_token_major(ref):
    rows = ref.shape[0] // PACK_CHUNKS
    return jnp.concatenate(
        [ref[pl.ds(c, rows, stride=PACK_CHUNKS), :] for c in range(PACK_CHUNKS)], axis=1)


def _ada_kernel(cond_ref, w_ref, b_ref, o_ref):
    c = cond_ref[...]
    s = c * jax.nn.sigmoid(c)
    o_ref[...] = jnp.dot(s, w_ref[...], preferred_element_type=F32,
                         precision=lax.Precision.HIGHEST) + b_ref[...]


def _adaln(conds, ada_w, ada_b):
    depth, d, n = ada_w.shape
    tn = 1024
    return pl.pallas_call(
        _ada_kernel,
        grid=(depth, n // tn),
        in_specs=[
            pl.BlockSpec((8, d), lambda l, j: (0, 0)),
            pl.BlockSpec((None, d, tn), lambda l, j: (l, 0, j)),
            pl.BlockSpec((None, 1, tn), lambda l, j: (l, 0, j)),
        ],
        out_specs=pl.BlockSpec((None, 8, tn), lambda l, j: (l, 0, j)),
        out_shape=jax.ShapeDtypeStruct((depth, 8, n), F32),
        compiler_params=_params(2),
        name="adaln",
    )(conds, ada_w, ada_b.reshape(depth, 1, n))


def _mla_in_kernel(x_ref, mod_ref, nw_ref, wdq_ref, qln_ref, wuqt_ref, wdkv_ref, kvn_ref,
                   qnn_ref, tq_ref, qt_ref, ckv_ref, kr_ref):
    x = x_ref[...]
    h = _rms(x) * nw_ref[...] * (1.0 + mod_ref[1:2, :]) + mod_ref[0:1, :]
    hb = h.astype(BF16)
    kv = jnp.dot(hb, wdkv_ref[...], preferred_element_type=F32)
    ckv_ref[...] = _rms(kv[:, :KV_LORA]) * kvn_ref[...]
    kr_ref[...] = kv[:, KV_LORA:]
    cq = jnp.dot(hb, wdq_ref[...], preferred_element_type=F32)
    cqn = (_rms(cq) * qln_ref[...]).astype(BF16)
    qt = lax.dot_general(wuqt_ref[...], cqn, (((1,), (1,)), ((), ())),
                         preferred_element_type=F32)
    tq = tq_ref[...]
    qnn = qnn_ref[...]
    for hd in range(N_HEADS):
        a = qt[hd * HEAD_PAD: hd * HEAD_PAD + LANES, :]
        b = qt[hd * HEAD_PAD + LANES: (hd + 1) * HEAD_PAD, :]
        b0 = b[:ROPE_DIM, :]
        ss = jnp.sum(a * a, axis=0, keepdims=True) + jnp.sum(b0 * b0, axis=0, keepdims=True)
        r = lax.rsqrt(ss * (1.0 / QK_HEAD) + EPS)
        bq = b * tq
        rot = ((bq[:ROPE_DIM, :] + bq[ROPE_DIM:, :]) * r).astype(BF16)
        qt_ref[hd * HEAD_PAD: hd * HEAD_PAD + LANES, :] = (a * r * qnn).astype(BF16)
        qt_ref[hd * HEAD_PAD + LANES: hd * HEAD_PAD + LANES + ROPE_DIM, :] = rot
        qt_ref[hd * HEAD_PAD + LANES + ROPE_DIM: (hd + 1) * HEAD_PAD, :] = rot


def _mla_in(x, mods, cond_of_tile, table_of_tile, nw, wdq, qln, wuqt, wdkv, kvn, qnn, tq):
    t, d = x.shape
    nt = t // TM
    const = lambda i: (0, 0)
    return pl.pallas_call(
        _mla_in_kernel,
        grid=(nt,),
        in_specs=[
            pl.BlockSpec((TM, d), lambda i: (i, 0)),
            pl.BlockSpec((None, 6, d), lambda i: (cond_of_tile(i), 0, 0)),
            pl.BlockSpec((1, d), const),
            pl.BlockSpec(wdq.shape, const),
            pl.BlockSpec((1, wdq.shape[1]), const),
            pl.BlockSpec(wuqt.shape, const),
            pl.BlockSpec(wdkv.shape, const),
            pl.BlockSpec((1, KV_LORA), const),
            pl.BlockSpec((LANES, TM), const),
            pl.BlockSpec((LANES, TM), lambda i: (0, table_of_tile(i))),
        ],
        out_specs=[
            pl.BlockSpec((N_HEADS * HEAD_PAD, TM), lambda i: (0, i)),
            pl.BlockSpec((TM, KV_LORA), lambda i: (i, 0)),
            pl.BlockSpec((TM, LANES), lambda i: (i, 0)),
        ],
        out_shape=[
            jax.ShapeDtypeStruct((N_HEADS * HEAD_PAD, t), BF16),
            jax.ShapeDtypeStruct((t, KV_LORA), F32),
            jax.ShapeDtypeStruct((t, LANES), F32),
        ],
        compiler_params=_params(1),
        name="mla_in",
    )(x, mods, nw, wdq, qln, wuqt, wdkv, kvn, qnn, tq)


def _kv_expand_kernel(ckv_ref, kr_ref, tk_ref, wuk_ref, wuvt_ref, knn_ref, k_ref, vt_ref):
    cb = ckv_ref[...].astype(BF16)
    kn = jnp.dot(cb, wuk_ref[...], preferred_element_type=F32)
    vt_ref[...] = lax.dot_general(wuvt_ref[...], cb, (((1,), (1,)), ((), ())),
                                  preferred_element_type=F32).astype(BF16)
    kr = kr_ref[...]
    rope_lane = lax.broadcasted_iota(I32, (1, LANES), 1) < ROPE_DIM
    ssr = jnp.sum(jnp.where(rope_lane, kr * kr, 0.0), axis=-1, keepdims=True)
    bk = kr * tk_ref[...]
    rot = jnp.where(rope_lane, bk + pltpu.roll(bk, ROPE_DIM, axis=1), 0.0)
    knn = knn_ref[...]
    for hd in range(N_HEADS):
        a = kn[:, hd * LANES: (hd + 1) * LANES]
        r = lax.rsqrt((jnp.sum(a * a, axis=-1, keepdims=True) + ssr) * (1.0 / QK_HEAD) + EPS)
        k_ref[:, hd * HEAD_PAD: hd * HEAD_PAD + LANES] = (a * r * knn).astype(BF16)
        k_ref[:, hd * HEAD_PAD + LANES: (hd + 1) * HEAD_PAD] = (rot * r).astype(BF16)


def _kv_expand(ckv, kr, tk, table_of_tile, wuk, wuvt, knn):
    t = ckv.shape[0]
    const = lambda i: (0, 0)
    return pl.pallas_call(
        _kv_expand_kernel,
        grid=(t // TM,),
        in_specs=[
            pl.BlockSpec((TM, KV_LORA), lambda i: (i, 0)),
            pl.BlockSpec((TM, LANES), lambda i: (i, 0)),
            pl.BlockSpec((TM, LANES), lambda i: (table_of_tile(i), 0)),
            pl.BlockSpec(wuk.shape, const),
            pl.BlockSpec(wuvt.shape, const),
            pl.BlockSpec((1, LANES), const),
        ],
        out_specs=[
            pl.BlockSpec((TM, N_HEADS * HEAD_PAD), lambda i: (i, 0)),
            pl.BlockSpec((None, N_HEADS * V_HEAD, TM), lambda i: (i, 0, 0)),
        ],
        out_shape=[
            jax.ShapeDtypeStruct((t, N_HEADS * HEAD_PAD), BF16),
            jax.ShapeDtypeStruct((t // TM, N_HEADS * V_HEAD, TM), BF16),
        ],
        compiler_params=_params(1),
        name="kv_expand",
    )(ckv, kr, tk, wuk, wuvt, knn)


def _attn_kernel(bounded_ref, qt_ref, k_ref, vt_ref, o_ref, *, heads, nk, group):
    tq = qt_ref.shape[1]
    chains = [(hh, sub) for hh in range(heads) for sub in range(tq // TM)]

    def load_q(part):
        return [qt_ref[hh * HEAD_PAD:(hh + 1) * HEAD_PAD, sub * TM:(sub + 1) * TM] for hh, sub in part]

    def store(part, ls, accs):
        for (hh, sub), l, acc in zip(part, ls, accs):
            o_ref[sub * TM:(sub + 1) * TM, hh * V_HEAD:(hh + 1) * V_HEAD] = (acc / l).T.astype(BF16)

    @pl.when(bounded_ref[0] == 1)
    def _():
        for c0 in range(0, len(chains), group):
            part = chains[c0:c0 + group]
            qts = load_q(part)
            lps = [jnp.zeros((8, TM), F32) for _ in part]
            accs = [None for _ in part]
            p_prev = None
            for j in range(nk + 1):
                p_cur = []
                if j < nk:
                    for n, ((hh, _), qt) in enumerate(zip(part, qts)):
                        ks = k_ref[j * TM:(j + 1) * TM, hh * HEAD_PAD:(hh + 1) * HEAD_PAD]
                        p = jnp.exp2(jnp.dot(ks, qt, preferred_element_type=F32))
                        lps[n] = lps[n] + jnp.sum(p.reshape(TM // 8, 8, TM), axis=0)
                        p_cur.append(p.astype(BF16))
                if j > 0:
                    for n, (hh, _) in enumerate(part):
                        vt = vt_ref[j - 1, hh * V_HEAD:(hh + 1) * V_HEAD, :]
                        pv = jnp.dot(vt, p_prev[n], preferred_element_type=F32)
                        accs[n] = pv if accs[n] is None else accs[n] + pv
                p_prev = p_cur
            store(part, [jnp.sum(lp, axis=0, keepdims=True) for lp in lps], accs)

    @pl.when(bounded_ref[0] == 0)
    def _():
        for chain in chains:
            hh = chain[0]
            qt, = load_q([chain])

            def body(j, carry, hh=hh, qt=qt):
                m, l, acc = carry
                start = pl.multiple_of(j * TM, TM)
                ks = k_ref[pl.ds(start, TM), hh * HEAD_PAD:(hh + 1) * HEAD_PAD]
                s = jnp.dot(ks, qt, preferred_element_type=F32)
                mn = jnp.maximum(m, jnp.max(s, axis=0, keepdims=True))
                alpha = jnp.exp2(m - mn)
                p = jnp.exp2(s - mn)
                l = alpha * l + jnp.sum(p, axis=0, keepdims=True)
                vt = vt_ref[j, hh * V_HEAD:(hh + 1) * V_HEAD, :]
                acc = alpha * acc + jnp.dot(vt, p.astype(BF16), preferred_element_type=F32)
                return mn, l, acc

            init = (jnp.full((1, TM), -jnp.inf, F32), jnp.zeros((1, TM), F32),
                    jnp.zeros((V_HEAD, TM), F32))
            _, l, acc = lax.fori_loop(0, nk, body, init)
            store([chain], [l], [acc])


def _attention(bounded, qt, k, vt, q_col0, n_batch, q_len, kv_len, tq, heads, group):
    nq = q_len // tq
    hb = N_HEADS // heads
    qb0 = q_col0 // tq
    nk = kv_len // TM
    return pl.pallas_call(
        functools.partial(_attn_kernel, heads=heads, nk=nk, group=group),
        grid_spec=pltpu.PrefetchScalarGridSpec(
            num_scalar_prefetch=1,
            grid=(n_batch, hb, nq),
            in_specs=[
                pl.BlockSpec((heads * HEAD_PAD, tq), lambda b, h, i, f: (h, qb0 + b * nq + i)),
                pl.BlockSpec((kv_len, heads * HEAD_PAD), lambda b, h, i, f: (b, h)),
                pl.BlockSpec((nk, heads * V_HEAD, TM), lambda b, h, i, f: (b, h, 0)),
            ],
            out_specs=pl.BlockSpec((tq, heads * V_HEAD), lambda b, h, i, f: (b * nq + i, h)),
        ),
        out_shape=jax.ShapeDtypeStruct((n_batch * q_len, N_HEADS * V_HEAD), BF16),
        compiler_params=_params(3),
        name="attention",
    )(bounded, qt, k, vt)


def _route(h2, rwh_ref, rwl_ref, rb_ref, route_ref):
    hi = h2.astype(BF16)
    lo = (h2 - hi.astype(F32)).astype(BF16)
    rwh = rwh_ref[...]
    logits = (jnp.dot(hi, rwh, preferred_element_type=F32)
              + jnp.dot(lo, rwh, preferred_element_type=F32)
              + jnp.dot(hi, rwl_ref[...], preferred_element_type=F32))
    lt = logits.T[:N_EXPERTS, :]
    scores = jax.nn.sigmoid(lt)
    sel = scores + rb_ref[...]
    srow = [sel[e:e + 1, :] for e in range(N_EXPERTS)]
    prow = [scores[e:e + 1, :] for e in range(N_EXPERTS)]

    def top2_sum(a, b, c, d):
        hab, lab = jnp.maximum(a, b), jnp.minimum(a, b)
        hcd, lcd = jnp.maximum(c, d), jnp.minimum(c, d)
        return jnp.maximum(hab, hcd) + jnp.maximum(jnp.minimum(hab, hcd), jnp.maximum(lab, lcd))

    gs = [top2_sum(*srow[4 * g:4 * g + 4]) for g in range(N_GROUPS)]
    best = jnp.zeros_like(gs[0], dtype=I32)
    bestv = gs[0]
    for g in range(1, N_GROUPS):
        upd = gs[g] > bestv
        best = jnp.where(upd, g, best)
        bestv = jnp.where(upd, gs[g], bestv)

    def pick(rows, j):
        out = rows[j]
        for g in range(1, N_GROUPS):
            out = jnp.where(best == g, rows[4 * g + j], out)
        return out

    sv = [pick(srow, j) for j in range(EXPERTS_PER_GROUP)]
    pv = [pick(prow, j) for j in range(EXPERTS_PER_GROUP)]
    i1 = jnp.zeros_like(best)
    v1 = sv[0]
    for j in range(1, EXPERTS_PER_GROUP):
        upd = sv[j] > v1
        i1 = jnp.where(upd, j, i1)
        v1 = jnp.where(upd, sv[j], v1)
    neg = jnp.float32(-jnp.inf)
    i2 = jnp.where(i1 == 0, 1, 0).astype(I32)
    v2 = jnp.where(i1 == 0, sv[1], sv[0])
    for j in range(1, EXPERTS_PER_GROUP):
        cand = jnp.where(i1 == j, neg, sv[j])
        upd = cand > v2
        i2 = jnp.where(upd, j, i2)
        v2 = jnp.where(upd, cand, v2)
    ilo = jnp.minimum(i1, i2)
    ihi = jnp.maximum(i1, i2)
    pair = jnp.where(ilo == 0, ihi - 1, jnp.where(ilo == 1, ihi + 1, N_PAIRS - 1))
    cls = best * N_PAIRS + pair

    def take(vals, idx):
        out = vals[0]
        for j in range(1, EXPERTS_PER_GROUP):
            out = jnp.where(idx == j, vals[j], out)
        return out

    wlo = take(pv, ilo)
    whi = take(pv, ihi)
    den = wlo + whi
    route_ref[...] = jnp.concatenate(
        [cls.astype(F32), wlo / den, whi / den, jnp.zeros((5, cls.shape[1]), F32)], axis=0)


def _finish_sublayer(x_new, mod_ref, nw2_ref, rwh_ref, rwl_ref, rb_ref, x_out_ref, h2p_ref, route_ref):
    x_out_ref[...] = x_new
    h2 = _rms(x_new) * nw2_ref[...] * (1.0 + mod_ref[4:5, :]) + mod_ref[3:4, :]
    _store_token_major(h2p_ref, _pack_bf16_pairs(h2))
    _route(h2, rwh_ref, rwl_ref, rb_ref, route_ref)


def _mla_out_kernel(attn_c_ref, attn_l_ref, x_ref, mod_ref, wo_ref, nw2_ref, rwh_ref, rwl_ref, rb_ref,
                    x1_ref, h2p_ref, route_ref, *, n_ctx_tiles):
    attn = jnp.where(pl.program_id(0) < n_ctx_tiles, attn_c_ref[...], attn_l_ref[...])
    mix = jnp.dot(attn, wo_ref[...], preferred_element_type=F32)
    x1 = x_ref[...] + mod_ref[2:3, :] * mix
    _finish_sublayer(x1, mod_ref, nw2_ref, rwh_ref, rwl_ref, rb_ref, x1_ref, h2p_ref, route_ref)


def _sublayer_out_specs(t, d):
    nt = t // TM
    specs = [
        pl.BlockSpec((TM, d), lambda i: (i, 0)),
        pl.BlockSpec((TM * PACK_CHUNKS, LANES), lambda i: (i, 0)),
        pl.BlockSpec((None, 8, TM), lambda i: (i, 0, 0)),
    ]
    shapes = [
        jax.ShapeDtypeStruct((t, d), F32),
        jax.ShapeDtypeStruct((t * PACK_CHUNKS, LANES), U32),
        jax.ShapeDtypeStruct((nt, 8, TM), F32),
    ]
    return specs, shapes


def _mla_out(attn_ctx, attn_lat, x, mods, cond_of_tile, wo, nw2, rwh, rwl, rb):
    t, d = x.shape
    assert d == 2 * PACK_CHUNKS * LANES
    const = lambda i: (0, 0)
    n_ctx_tiles = attn_ctx.shape[0] // TM
    out_specs, out_shape = _sublayer_out_specs(t, d)
    return pl.pallas_call(
        functools.partial(_mla_out_kernel, n_ctx_tiles=n_ctx_tiles),
        grid=(t // TM,),
        in_specs=[
            pl.BlockSpec((TM, attn_ctx.shape[1]), lambda i: (jnp.minimum(i, n_ctx_tiles - 1), 0)),
            pl.BlockSpec((TM, attn_lat.shape[1]), lambda i: (jnp.maximum(i - n_ctx_tiles, 0), 0)),
            pl.BlockSpec((TM, d), lambda i: (i, 0)),
            pl.BlockSpec((None, 6, d), lambda i: (cond_of_tile(i), 0, 0)),
            pl.BlockSpec(wo.shape, const),
            pl.BlockSpec((1, d), const),
            pl.BlockSpec(rwh.shape, const),
            pl.BlockSpec(rwl.shape, const),
            pl.BlockSpec(rb.shape, const),
        ],
        out_specs=out_specs,
        out_shape=out_shape,
        compiler_params=_params(1),
        name="mla_out",
    )(attn_ctx, attn_lat, x, mods, wo, nw2, rwh, rwl, rb)


def _pool_kernel(x_ref, mo_ref, xp_ref, mop_ref, xn_ref, mon_ref, modp_ref, mod_ref, nw_ref, pw_ref,
                 ps_ref, nw2_ref, rwh_ref, rwl_ref, rb_ref, x2_ref, h2p_ref, route_ref,
                 *, n_ctx_tiles, ctx_seq_tiles, lat_seq_tiles):
    i = pl.program_id(0)
    is_lat = i >= n_ctx_tiles
    seq_tiles = jnp.where(is_lat, lat_seq_tiles, ctx_seq_tiles)
    in_seq = jnp.where(is_lat, i - n_ctx_tiles, i) % seq_tiles
    has_prev = in_seq > 0
    has_next = in_seq < seq_tiles - 1
    g2p = modp_ref[5:6, :]
    nw = nw_ref[...]
    sc1 = 1.0 + mod_ref[1:2, :]
    sh1 = mod_ref[0:1, :]

    def pre(xv, mo_words):
        xx = xv + g2p * _unpack_f32(mo_words)
        return xx, _rms(xx) * nw * sc1 + sh1

    xcur, h = pre(x_ref[...], _load_token_major(mo_ref))
    _, hprev = pre(xp_ref[...], _load_token_major(mop_ref))
    _, hnext = pre(xn_ref[...], _load_token_major(mon_ref))
    d = h.shape[1]
    gw = d // len(POOL_WINDOWS)
    hb = h.astype(BF16)
    halo = jnp.concatenate(
        [hprev.astype(BF16), hnext.astype(BF16), jnp.zeros((LANES - 2 * HALO, d), BF16)], axis=0)

    t_mid = lax.broadcasted_iota(I32, (TM, TM), 0)
    e_mid = lax.broadcasted_iota(I32, (TM, TM), 1)
    t_hal = lax.broadcasted_iota(I32, (TM, LANES), 0)
    c_hal = lax.broadcasted_iota(I32, (TM, LANES), 1)
    far = jnp.int32(4 * TM)
    pos_prev = jnp.where(has_prev, c_hal - HALO, -far)
    pos_next = jnp.where(has_next, TM + c_hal - HALO, far)
    pos_hal = jnp.where(c_hal < HALO, pos_prev, jnp.where(c_hal < 2 * HALO, pos_next, far))
    t_col = lax.broadcasted_iota(I32, (TM, 1), 0)
    lo_bound = jnp.where(has_prev, -HALO, 0)
    hi_bound = jnp.where(has_next, TM - 1 + HALO, TM - 1)

    ys = []
    for g, w in enumerate(POOL_WINDOWS):
        half = w // 2
        dm = e_mid - t_mid
        band_mid = jnp.where(dm >= -half, jnp.where(dm <= half - 1, 1.0, 0.0), 0.0).astype(BF16)
        dh = pos_hal - t_hal
        band_hal = jnp.where(dh >= -half, jnp.where(dh <= half - 1, 1.0, 0.0), 0.0).astype(BF16)
        sl = slice(g * gw, (g + 1) * gw)
        wsum = (jnp.dot(band_mid, hb[:, sl], preferred_element_type=F32)
                + jnp.dot(band_hal, halo[:, sl], preferred_element_type=F32))
        cnt = (jnp.minimum(t_col + (half - 1), hi_bound)
               - jnp.maximum(t_col - half, lo_bound) + 1).astype(F32)
        pooled = wsum / cnt - h[:, sl]
        ys.append(jnp.dot(pooled.astype(BF16), pw_ref[g], preferred_element_type=F32))
    y = jnp.concatenate(ys, axis=1) * ps_ref[...]
    x2 = xcur + mod_ref[2:3, :] * y
    _finish_sublayer(x2, mod_ref, nw2_ref, rwh_ref, rwl_ref, rb_ref, x2_ref, h2p_ref, route_ref)


def _pool_layer(x1, mo, mods_prev, mods, cond_of_tile, n_ctx_tiles, ctx_seq_tiles, lat_seq_tiles,
                nw, pw, ps, nw2, rwh, rwl, rb):
    t, d = x1.shape
    const = lambda i: (0, 0)
    hb = TM // HALO
    last = t // HALO - 1
    prev_map = lambda i: (jnp.maximum(i * hb - 1, 0), 0)
    next_map = lambda i: (jnp.minimum((i + 1) * hb, last), 0)
    out_specs, out_shape = _sublayer_out_specs(t, d)
    return pl.pallas_call(
        functools.partial(_pool_kernel, n_ctx_tiles=n_ctx_tiles, ctx_seq_tiles=ctx_seq_tiles,
                          lat_seq_tiles=lat_seq_tiles),
        grid=(t // TM,),
        in_specs=[
            pl.BlockSpec((TM, d), lambda i: (i, 0)),
            pl.BlockSpec((TM * PACK_CHUNKS, LANES), lambda i: (i, 0)),
            pl.BlockSpec((HALO, d), prev_map),
            pl.BlockSpec((HALO * PACK_CHUNKS, LANES), prev_map),
            pl.BlockSpec((HALO, d), next_map),
            pl.BlockSpec((HALO * PACK_CHUNKS, LANES), next_map),
            pl.BlockSpec((None, 6, d), lambda i: (cond_of_tile(i), 0, 0)),
            pl.BlockSpec((None, 6, d), lambda i: (cond_of_tile(i), 0, 0)),
            pl.BlockSpec((1, d), const),
            pl.BlockSpec(pw.shape, lambda i: (0, 0, 0)),
            pl.BlockSpec((1, d), const),
            pl.BlockSpec((1, d), const),
            pl.BlockSpec(rwh.shape, const),
            pl.BlockSpec(rwl.shape, const),
            pl.BlockSpec(rb.shape, const),
        ],
        out_specs=out_specs,
        out_shape=out_shape,
        compiler_params=_params(1),
        name="pool_layer",
    )(x1, mo, x1, mo, x1, mo, mods_prev, mods, nw, pw, ps, nw2, rwh, rwl, rb)


def _final_kernel(x_ref, mo_ref, mod_ref, o_ref):
    o_ref[...] = x_ref[...] + mod_ref[5:6, :] * _unpack_f32(_load_token_major(mo_ref))


def _final(x, mo, mods, cond_of_tile):
    t, d = x.shape
    return pl.pallas_call(
        _final_kernel,
        grid=(t // TM,),
        in_specs=[
            pl.BlockSpec((TM, d), lambda i: (i, 0)),
            pl.BlockSpec((TM * PACK_CHUNKS, LANES), lambda i: (i, 0)),
            pl.BlockSpec((None, 6, d), lambda i: (cond_of_tile(i), 0, 0)),
        ],
        out_specs=pl.BlockSpec((TM, d), lambda i: (i, 0)),
        out_shape=jax.ShapeDtypeStruct((t, d), F32),
        compiler_params=_params(1),
        name="final_residual",
    )(x, mo, mods)


def _gather_kernel(idx_ref, src_ref, dst_ref, sem):
    base = pl.program_id(0) * GATHER_ROWS

    def issue(r, carry):
        src_row = pl.multiple_of(idx_ref[base + r] * PACK_CHUNKS, PACK_CHUNKS)
        dst_row = pl.multiple_of(r * PACK_CHUNKS, PACK_CHUNKS)
        pltpu.make_async_copy(src_ref.at[pl.ds(src_row, PACK_CHUNKS)],
                              dst_ref.at[pl.ds(dst_row, PACK_CHUNKS)], sem).start()
        return carry

    lax.fori_loop(0, GATHER_ROWS, issue, 0, unroll=8)

    def drain(r, carry):
        pltpu.make_async_copy(src_ref.at[pl.ds(0, PACK_CHUNKS)],
                              dst_ref.at[pl.ds(0, PACK_CHUNKS)], sem).wait()
        return carry

    lax.fori_loop(0, GATHER_ROWS, drain, 0, unroll=8)


def _gather_rows(src, idx):
    n = idx.shape[0]
    return pl.pallas_call(
        _gather_kernel,
        grid_spec=pltpu.PrefetchScalarGridSpec(
            num_scalar_prefetch=1,
            grid=(n // GATHER_ROWS,),
            in_specs=[pl.BlockSpec(memory_space=pl.ANY)],
            out_specs=pl.BlockSpec((GATHER_ROWS * PACK_CHUNKS, LANES), lambda i, idx: (i, 0)),
            scratch_shapes=[pltpu.SemaphoreType.DMA(())],
        ),
        out_shape=jax.ShapeDtypeStruct((n * PACK_CHUNKS, LANES), src.dtype),
        compiler_params=_params(1),
        name="gather_rows",
    )(idx, src)


def _moe_kernel(ea_ref, eb_ref, valid_ref, xs_ref, g_ref, wga_ref, wua_ref, wda_ref,
                wgb_ref, wub_ref, wdb_ref, y_ref):
    i = pl.program_id(0)

    @pl.when(valid_ref[i] == 1)
    def _():
        w = _load_token_major(xs_ref)
        xlo = _unpack_lo(w).astype(BF16)
        xhi = _unpack_hi(w).astype(BF16)
        half = xlo.shape[1]

        def ffn(wg_ref, wu_ref, wd_ref):
            g = (jnp.dot(xlo, wg_ref[:half, :], preferred_element_type=F32)
                 + jnp.dot(xhi, wg_ref[half:, :], preferred_element_type=F32))
            u = (jnp.dot(xlo, wu_ref[:half, :], preferred_element_type=F32)
                 + jnp.dot(xhi, wu_ref[half:, :], preferred_element_type=F32))
            a = (g * jax.nn.sigmoid(g)) * u
            return jnp.dot(a.astype(BF16), wd_ref[...], preferred_element_type=F32)

        gates = g_ref[...]
        y = gates[:, 0:1] * ffn(wga_ref, wua_ref, wda_ref) + gates[:, 1:2] * ffn(wgb_ref, wub_ref, wdb_ref)
        _store_token_major(y_ref, _pack_bf16_pairs(y))

    @pl.when(valid_ref[i] == 0)
    def _():
        y_ref[...] = jnp.zeros_like(y_ref)


def _moe_sorted(xs, gates, tile_ea, tile_eb, tile_valid, layer, wg, wu, wd):
    rows = xs.shape[0] // PACK_CHUNKS
    d, ff = wg.shape[-2:]
    wa = lambda i, ea, eb, va: (layer, ea[i], 0, 0)
    wb = lambda i, ea, eb, va: (layer, eb[i], 0, 0)
    row = lambda i, ea, eb, va: (i, 0)
    return pl.pallas_call(
        _moe_kernel,
        grid_spec=pltpu.PrefetchScalarGridSpec(
            num_scalar_prefetch=3,
            grid=(rows // TM,),
            in_specs=[
                pl.BlockSpec((TM * PACK_CHUNKS, LANES), row),
                pl.BlockSpec((TM, 2), row),
                pl.BlockSpec((None, None, d, ff), wa),
                pl.BlockSpec((None, None, d, ff), wa),
                pl.BlockSpec((None, None, ff, d), wa),
                pl.BlockSpec((None, None, d, ff), wb),
                pl.BlockSpec((None, None, d, ff), wb),
                pl.BlockSpec((None, None, ff, d), wb),
            ],
            out_specs=pl.BlockSpec((TM * PACK_CHUNKS, LANES), row),
        ),
        out_shape=jax.ShapeDtypeStruct(xs.shape, U32),
        compiler_params=_params(1),
        name="moe_experts",
    )(tile_ea, tile_eb, tile_valid, xs, gates, wg, wu, wd, wg, wu, wd)


def _moe_layer(h2p, route, layer, wg, wu, wd):
    t = h2p.shape[0] // PACK_CHUNKS
    assert t % GATHER_ROWS == 0
    n_tiles = t // TM + N_CLASSES
    n_tiles += (-n_tiles) % (GATHER_ROWS // TM)
    cls = route[:, 0, :].reshape(t).astype(I32)
    g_lo = route[:, 1, :].reshape(t)
    g_hi = route[:, 2, :].reshape(t)
    onehot = (cls[:, None] == jnp.arange(N_CLASSES, dtype=I32)[None, :]).astype(I32)
    csum = jnp.cumsum(onehot, axis=0)
    counts = csum[-1]
    rank = jnp.take_along_axis(csum, cls[:, None], axis=1)[:, 0] - 1
    tiles_c = (counts + TM - 1) // TM
    tile_end = jnp.cumsum(tiles_c)
    tile_start = tile_end - tiles_c
    pos = tile_start[cls] * TM + rank
    row_tok = jnp.zeros((n_tiles * TM,), I32).at[pos].set(jnp.arange(t, dtype=I32))
    tile_ids = jnp.arange(n_tiles, dtype=I32)
    total = tile_end[-1]
    tile_valid = (tile_ids < total).astype(I32)
    tile_cls = jnp.sum((tile_end[None, :] <= jnp.minimum(tile_ids, total - 1)[:, None]).astype(I32), axis=1)
    pair = tile_cls % N_PAIRS
    group = tile_cls // N_PAIRS
    tile_ea = group * EXPERTS_PER_GROUP + jnp.asarray(PAIR_LO, I32)[pair]
    tile_eb = group * EXPERTS_PER_GROUP + jnp.asarray(PAIR_HI, I32)[pair]
    gates = jnp.stack([g_lo[row_tok], g_hi[row_tok]], axis=1)
    xs = _gather_rows(h2p, row_tok)
    ys = _moe_sorted(xs, gates, tile_ea, tile_eb, tile_valid, layer, wg, wu, wd)
    return _gather_rows(ys, pos)


def _swap_rope_halves(w):
    shp = w.shape
    return w.reshape(shp[:-1] + (2, 2, AXIS_DIM // 2))[..., ::-1, :].reshape(shp)


def _rope_tables(length, norm_rope, scale):
    rows = length // GRID_W
    row = jnp.repeat(jnp.arange(rows, dtype=F32), GRID_W)
    col = jnp.tile(jnp.arange(GRID_W, dtype=F32), rows)
    inv = jnp.power(ROPE_THETA, -jnp.arange(0, AXIS_DIM, 2, dtype=F32) / AXIS_DIM)
    ang = jnp.stack([row[:, None] * inv, col[:, None] * inv], axis=1)
    cos, sin = jnp.cos(ang), jnp.sin(ang)
    c_full = jnp.stack([cos, cos], axis=2).reshape(length, ROPE_DIM)
    s_full = jnp.stack([-sin, sin], axis=2).reshape(length, ROPE_DIM)
    lat = jnp.concatenate([norm_rope * c_full, _swap_rope_halves(norm_rope) * s_full], axis=1)
    ctx = jnp.concatenate([norm_rope, jnp.zeros((ROPE_DIM,), F32)])
    ctx = jnp.broadcast_to(ctx[None, :], (TM, 2 * ROPE_DIM))
    return jnp.concatenate([ctx, lat], axis=0) * scale


def kernel(x_prompt, x_sample, cache_ckv, cache_krope, c, c_ctx, ada_w, ada_b, norm_mix_w, norm_ffn_w,
           mla_w_dq, mla_q_lora_norm, mla_w_uq, mla_w_dkv, mla_kv_lora_norm, mla_w_uk, mla_w_uv,
           mla_q_norm, mla_k_norm, mla_w_o, pool_w, pool_scale, router_w, router_bias,
           moe_w_gate, moe_w_up, moe_w_down):
    batch, seq, d = x_prompt.shape
    dec_batch, dec_seq, _ = x_sample.shape
    past = cache_ckv.shape[2]
    depth = ada_w.shape[0]
    assert seq == TM and past == TM and dec_seq % (2 * TM) == 0
    assert dec_batch + 1 <= 8 and depth == 2
    t_ctx = batch * seq
    t_lat = dec_batch * dec_seq
    n_ctx_tiles = t_ctx // TM
    lat_seq_tiles = dec_seq // TM
    ctx_seq_tiles = seq // TM

    def cond_of_tile(i):
        return jnp.where(i < n_ctx_tiles, 0, 1 + (i - n_ctx_tiles) // lat_seq_tiles)

    def table_of_tile(i):
        return jnp.where(i < n_ctx_tiles, 0, 1 + (i - n_ctx_tiles) % lat_seq_tiles)

    x0 = jnp.concatenate([x_prompt.reshape(t_ctx, d), x_sample.reshape(t_lat, d)], axis=0)
    conds = jnp.concatenate([c_ctx[None, :], c, jnp.zeros((7 - dec_batch, d), F32)], axis=0)
    mods = _adaln(conds, ada_w, ada_b).reshape(depth, 8, 6, d)

    rw = jnp.pad(router_w, ((0, 0), (0, LANES - N_EXPERTS)))
    rwh = rw.astype(BF16)
    rwl = (rw - rwh.astype(F32)).astype(BF16)
    rb = jnp.broadcast_to(router_bias.astype(F32)[:, None], (N_EXPERTS, TM))
    wg = moe_w_gate.astype(BF16)
    wu = moe_w_up.astype(BF16)
    wd = moe_w_down.astype(BF16)

    j = 0
    q_norm, k_norm = mla_q_norm[j], mla_k_norm[j]
    w_uq = mla_w_uq[j].reshape(-1, N_HEADS, QK_HEAD)
    w_uq = jnp.concatenate([w_uq, _swap_rope_halves(w_uq[..., QK_NOPE:])], axis=-1)
    w_uqt = w_uq.reshape(-1, N_HEADS * HEAD_PAD).T.astype(BF16)
    w_dkv = mla_w_dkv[j]
    w_dkv = jnp.concatenate([w_dkv, _swap_rope_halves(w_dkv[:, KV_LORA:])], axis=1).astype(BF16)
    q_scale = ATTN_SCALE * math.log2(math.e)
    tq = _rope_tables(dec_seq, q_norm[QK_NOPE:], q_scale).T
    tk = _rope_tables(dec_seq, k_norm[QK_NOPE:], 1.0)
    qnn = jnp.broadcast_to((q_norm[:QK_NOPE] * q_scale)[:, None], (QK_NOPE, TM))
    knn = k_norm[:QK_NOPE][None, :]

    qt, ckv, kr = _mla_in(x0, mods[0], cond_of_tile, table_of_tile, norm_mix_w[0][None, :],
                          mla_w_dq[j].astype(BF16), mla_q_lora_norm[j][None, :], w_uqt, w_dkv,
                          mla_kv_lora_norm[j][None, :], qnn, tq)
    state_ckv = ckv[:t_ctx].reshape(batch, 1, seq, KV_LORA)
    state_krope = kr[:t_ctx, :ROPE_DIM].reshape(batch, 1, seq, ROPE_DIM)

    w_uk = mla_w_uk[j].astype(BF16)
    w_uvt = mla_w_uv[j].T.astype(BF16)
    k_ctx, vt_ctx = _kv_expand(ckv[:t_ctx], kr[:t_ctx], tk, lambda i: 0, w_uk, w_uvt, knn)
    kv_len = past + dec_seq
    ckv_lat = jnp.concatenate(
        [cache_ckv[:, j], ckv[t_ctx:].reshape(dec_batch, dec_seq, KV_LORA)], axis=1)
    kr_cache = jnp.pad(cache_krope[:, j], ((0, 0), (0, 0), (0, LANES - ROPE_DIM)))
    kr_lat = jnp.concatenate([kr_cache, kr[t_ctx:].reshape(dec_batch, dec_seq, LANES)], axis=1)
    kv_seq_tiles = kv_len // TM
    k_lat, vt_lat = _kv_expand(ckv_lat.reshape(-1, KV_LORA), kr_lat.reshape(-1, LANES), tk,
                               lambda i: i % kv_seq_tiles, w_uk, w_uvt, knn)

    logit_bound = q_scale * QK_HEAD * jnp.max(jnp.abs(q_norm)) * jnp.max(jnp.abs(k_norm))
    bounded = (logit_bound <= MAX_SAFE_LOGIT).astype(I32).reshape(1)
    attn_ctx = _attention(bounded, qt, k_ctx, vt_ctx, 0, batch, seq, seq, tq=seq, heads=N_HEADS, group=4)
    attn_lat = _attention(bounded, qt, k_lat, vt_lat, t_ctx, dec_batch, dec_seq, kv_len, tq=2 * TM,
                          heads=2, group=4)

    x1, h2p, route = _mla_out(attn_ctx, attn_lat, x0, mods[0], cond_of_tile, mla_w_o[j].astype(BF16),
                              norm_ffn_w[0][None, :], rwh, rwl, rb)
    mo = _moe_layer(h2p, route, 0, wg, wu, wd)

    x2, h2p, route = _pool_layer(x1, mo, mods[0], mods[1], cond_of_tile, n_ctx_tiles, ctx_seq_tiles,
                                 lat_seq_tiles, norm_mix_w[1][None, :], pool_w[0].astype(BF16),
                                 pool_scale[0][None, :], norm_ffn_w[1][None, :], rwh, rwl, rb)
    mo = _moe_layer(h2p, route, 1, wg, wu, wd)
    y = _final(x2, mo, mods[1], cond_of_tile)

    return (y[:t_ctx].reshape(batch, seq, d), y[t_ctx:].reshape(dec_batch, dec_seq, d),
            state_ckv, state_krope)
```

```python
import functools
import math

import jax
import jax.numpy as jnp
from jax import lax
from jax.experimental import pallas as pl
from jax.experimental.pallas import tpu as pltpu

F32 = jnp.float32
BF16 = jnp.bfloat16
I32 = jnp.int32

GRID_W = 64
N_HEADS = 16
QK_NOPE = 128
ROPE_DIM = 64
QK_HEAD = QK_NOPE + ROPE_DIM
V_HEAD = 128
KV_LORA = 256
ROPE_THETA = 10000.0
AXIS_DIM = ROPE_DIM // 2
ATTN_SCALE = QK_HEAD ** -0.5
POOL_WINDOWS = (2, 4, 8, 16)
N_EXPERTS = 16
N_GROUPS = 4
EXPERTS_PER_GROUP = N_EXPERTS // N_GROUPS
EPS = 1e-6

LANES = 128
HEAD_PAD = 2 * LANES
TM = 256
HALO = 16
N_PAIRS = 6
N_CLASSES = N_GROUPS * N_PAIRS
PAIR_LO = (0, 0, 0, 1, 1, 2)
PAIR_HI = (1, 2, 3, 2, 3, 3)
ROW_CHUNKS = 16
BUF_PITCH = 24
VMEM_LIMIT = 52 * 1024 * 1024
MAX_SAFE_LOGIT = 64.0


def _params(n_axes):
    return pltpu.CompilerParams(
        dimension_semantics=("arbitrary",) * n_axes, vmem_limit_bytes=VMEM_LIMIT)


def _rms(x):
    return x * lax.rsqrt(jnp.mean(x * x, axis=-1, keepdims=True) + EPS)


def _store_token_major(ref, x, pitch=ROW_CHUNKS):
    rows = x.shape[0]
    for c in range(ROW_CHUNKS):
        ref[pl.ds(c, rows, stride=pitch), :] = x[:, c * LANES:(c + 1) * LANES]


def _load_token_major(ref, rows, pitch=ROW_CHUNKS):
    return jnp.concatenate(
        [ref[pl.ds(c, rows, stride=pitch), :] for c in range(ROW_CHUNKS)], axis=1)


def _ada_kernel(cond_ref, w_ref, b_ref, o_ref):
    c = cond_ref[...]
    s = c * jax.nn.sigmoid(c)
    o_ref[...] = jnp.dot(s, w_ref[...], preferred_element_type=F32,
                         precision=lax.Precision.HIGHEST) + b_ref[...]


def _adaln(conds, ada_w, ada_b):
    depth, d, n = ada_w.shape
    tn = 1024
    return pl.pallas_call(
        _ada_kernel,
        grid=(depth, n // tn),
        in_specs=[
            pl.BlockSpec((8, d), lambda l, j: (0, 0)),
            pl.BlockSpec((None, d, tn), lambda l, j: (l, 0, j)),
            pl.BlockSpec((None, 1, tn), lambda l, j: (l, 0, j)),
        ],
        out_specs=pl.BlockSpec((None, 8, tn), lambda l, j: (l, 0, j)),
        out_shape=jax.ShapeDtypeStruct((depth, 8, n), F32),
        compiler_params=_params(2),
        name="adaln",
    )(conds, ada_w, ada_b.reshape(depth, 1, n))


def _mla_in_kernel(xc_ref, xl_ref, mod_ref, nw_ref, wdq_ref, qln_ref, wuqt_ref, wdkv_ref, kvn_ref,
                   qnn_ref, tq_ref, qt_ref, ckv_ref, kr_ref, *, n_ctx_tiles):
    x = jnp.where(pl.program_id(0) < n_ctx_tiles, xc_ref[...], xl_ref[...])
    h = _rms(x) * nw_ref[...] * (1.0 + mod_ref[1:2, :]) + mod_ref[0:1, :]
    hb = h.astype(BF16)
    kv = jnp.dot(hb, wdkv_ref[...], preferred_element_type=F32)
    ckv_ref[...] = _rms(kv[:, :KV_LORA]) * kvn_ref[...]
    kr_ref[...] = kv[:, KV_LORA:]
    cq = jnp.dot(hb, wdq_ref[...], preferred_element_type=F32)
    cqn = (_rms(cq) * qln_ref[...]).astype(BF16)
    qt = lax.dot_general(wuqt_ref[...], cqn, (((1,), (1,)), ((), ())),
                         preferred_element_type=F32)
    tq = tq_ref[...]
    qnn = qnn_ref[...]
    for hd in range(N_HEADS):
        a = qt[hd * HEAD_PAD: hd * HEAD_PAD + LANES, :]
        b = qt[hd * HEAD_PAD + LANES: (hd + 1) * HEAD_PAD, :]
        b0 = b[:ROPE_DIM, :]
        ss = jnp.sum(a * a, axis=0, keepdims=True) + jnp.sum(b0 * b0, axis=0, keepdims=True)
        r = lax.rsqrt(ss * (1.0 / QK_HEAD) + EPS)
        bq = b * tq
        rot = ((bq[:ROPE_DIM, :] + bq[ROPE_DIM:, :]) * r).astype(BF16)
        qt_ref[hd * HEAD_PAD: hd * HEAD_PAD + LANES, :] = (a * r * qnn).astype(BF16)
        qt_ref[hd * HEAD_PAD + LANES: hd * HEAD_PAD + LANES + ROPE_DIM, :] = rot
        qt_ref[hd * HEAD_PAD + LANES + ROPE_DIM: (hd + 1) * HEAD_PAD, :] = rot


def _ctx_lat_specs(n_ctx_tiles, width):
    return [pl.BlockSpec((TM, width), lambda i: (jnp.minimum(i, n_ctx_tiles - 1), 0)),
            pl.BlockSpec((TM, width), lambda i: (jnp.maximum(i - n_ctx_tiles, 0), 0))]


def _mla_in(x_ctx, x_lat, mods, cond_of_tile, table_of_tile, nw, wdq, qln, wuqt, wdkv, kvn, qnn, tq):
    d = x_ctx.shape[1]
    n_ctx_tiles = x_ctx.shape[0] // TM
    t = x_ctx.shape[0] + x_lat.shape[0]
    nt = t // TM
    const = lambda i: (0, 0)
    return pl.pallas_call(
        functools.partial(_mla_in_kernel, n_ctx_tiles=n_ctx_tiles),
        grid=(nt,),
        in_specs=_ctx_lat_specs(n_ctx_tiles, d) + [
            pl.BlockSpec((None, 6, d), lambda i: (cond_of_tile(i), 0, 0)),
            pl.BlockSpec((1, d), const),
            pl.BlockSpec(wdq.shape, const),
            pl.BlockSpec((1, wdq.shape[1]), const),
            pl.BlockSpec(wuqt.shape, const),
            pl.BlockSpec(wdkv.shape, const),
            pl.BlockSpec((1, KV_LORA), const),
            pl.BlockSpec((LANES, TM), const),
            pl.BlockSpec((LANES, TM), lambda i: (0, table_of_tile(i))),
        ],
        out_specs=[
            pl.BlockSpec((N_HEADS * HEAD_PAD, TM), lambda i: (0, i)),
            pl.BlockSpec((TM, KV_LORA), lambda i: (i, 0)),
            pl.BlockSpec((TM, LANES), lambda i: (i, 0)),
        ],
        out_shape=[
            jax.ShapeDtypeStruct((N_HEADS * HEAD_PAD, t), BF16),
            jax.ShapeDtypeStruct((t, KV_LORA), F32),
            jax.ShapeDtypeStruct((t, LANES), F32),
        ],
        compiler_params=_params(1),
        name="mla_in",
    )(x_ctx, x_lat, mods, nw, wdq, qln, wuqt, wdkv, kvn, qnn, tq)


def _kv_expand_kernel(ckv_ref, kr_ref, tk_ref, wuk_ref, wuvt_ref, knn_ref, k_ref, vt_ref):
    cb = ckv_ref[...].astype(BF16)
    kn = jnp.dot(cb, wuk_ref[...], preferred_element_type=F32)
    vt_ref[...] = lax.dot_general(wuvt_ref[...], cb, (((1,), (1,)), ((), ())),
                                  preferred_element_type=F32).astype(BF16)
    kr = kr_ref[...]
    rope_lane = lax.broadcasted_iota(I32, (1, LANES), 1) < ROPE_DIM
    ssr = jnp.sum(jnp.where(rope_lane, kr * kr, 0.0), axis=-1, keepdims=True)
    bk = kr * tk_ref[...]
    rot = jnp.where(rope_lane, bk + pltpu.roll(bk, ROPE_DIM, axis=1), 0.0)
    knn = knn_ref[...]
    for hd in range(N_HEADS):
        a = kn[:, hd * LANES: (hd + 1) * LANES]
        r = lax.rsqrt((jnp.sum(a * a, axis=-1, keepdims=True) + ssr) * (1.0 / QK_HEAD) + EPS)
        k_ref[:, hd * HEAD_PAD: hd * HEAD_PAD + LANES] = (a * r * knn).astype(BF16)
        k_ref[:, hd * HEAD_PAD + LANES: (hd + 1) * HEAD_PAD] = (rot * r).astype(BF16)


def _kv_expand(ckv, kr, tk, table_of_tile, wuk, wuvt, knn):
    t = ckv.shape[0]
    const = lambda i: (0, 0)
    return pl.pallas_call(
        _kv_expand_kernel,
        grid=(t // TM,),
        in_specs=[
            pl.BlockSpec((TM, KV_LORA), lambda i: (i, 0)),
            pl.BlockSpec((TM, LANES), lambda i: (i, 0)),
            pl.BlockSpec((TM, LANES), lambda i: (table_of_tile(i), 0)),
            pl.BlockSpec(wuk.shape, const),
            pl.BlockSpec(wuvt.shape, const),
            pl.BlockSpec((1, LANES), const),
        ],
        out_specs=[
            pl.BlockSpec((TM, N_HEADS * HEAD_PAD), lambda i: (i, 0)),
            pl.BlockSpec((None, N_HEADS * V_HEAD, TM), lambda i: (i, 0, 0)),
        ],
        out_shape=[
            jax.ShapeDtypeStruct((t, N_HEADS * HEAD_PAD), BF16),
            jax.ShapeDtypeStruct((t // TM, N_HEADS * V_HEAD, TM), BF16),
        ],
        compiler_params=_params(1),
        name="kv_expand",
    )(ckv, kr, tk, wuk, wuvt, knn)


def _attn_kernel(bounded_ref, qt_ref, k_ref, vt_ref, o_ref, *, heads, nk, group):
    tq = qt_ref.shape[1]
    chains = [(hh, sub) for hh in range(heads) for sub in range(tq // TM)]

    def load_q(part):
        return [qt_ref[hh * HEAD_PAD:(hh + 1) * HEAD_PAD, sub * TM:(sub + 1) * TM] for hh, sub in part]

    def store(part, ls, accs):
        for (hh, sub), l, acc in zip(part, ls, accs):
            o_ref[sub * TM:(sub + 1) * TM, hh * V_HEAD:(hh + 1) * V_HEAD] = (acc / l).T.astype(BF16)

    @pl.when(bounded_ref[0] == 1)
    def _():
        for c0 in range(0, len(chains), group):
            part = chains[c0:c0 + group]
            qts = load_q(part)
            lps = [jnp.zeros((8, TM), F32) for _ in part]
            accs = [None for _ in part]
            p_prev = None
            for j in range(nk + 1):
                p_cur = []
                if j < nk:
                    for n, ((hh, _), qt) in enumerate(zip(part, qts)):
                        ks = k_ref[j * TM:(j + 1) * TM, hh * HEAD_PAD:(hh + 1) * HEAD_PAD]
                        p = jnp.exp2(jnp.dot(ks, qt, preferred_element_type=F32))
                        lps[n] = lps[n] + jnp.sum(p.reshape(TM // 8, 8, TM), axis=0)
                        p_cur.append(p.astype(BF16))
                if j > 0:
                    for n, (hh, _) in enumerate(part):
                        vt = vt_ref[j - 1, hh * V_HEAD:(hh + 1) * V_HEAD, :]
                        pv = jnp.dot(vt, p_prev[n], preferred_element_type=F32)
                        accs[n] = pv if accs[n] is None else accs[n] + pv
                p_prev = p_cur
            store(part, [jnp.sum(lp, axis=0, keepdims=True) for lp in lps], accs)

    @pl.when(bounded_ref[0] == 0)
    def _():
        for chain in chains:
            hh = chain[0]
            qt, = load_q([chain])

            def body(j, carry, hh=hh, qt=qt):
                m, l, acc = carry
                start = pl.multiple_of(j * TM, TM)
                ks = k_ref[pl.ds(start, TM), hh * HEAD_PAD:(hh + 1) * HEAD_PAD]
                s = jnp.dot(ks, qt, preferred_element_type=F32)
                mn = jnp.maximum(m, jnp.max(s, axis=0, keepdims=True))
                alpha = jnp.exp2(m - mn)
                p = jnp.exp2(s - mn)
                l = alpha * l + jnp.sum(p, axis=0, keepdims=True)
                vt = vt_ref[j, hh * V_HEAD:(hh + 1) * V_HEAD, :]
                acc = alpha * acc + jnp.dot(vt, p.astype(BF16), preferred_element_type=F32)
                return mn, l, acc

            init = (jnp.full((1, TM), -jnp.inf, F32), jnp.zeros((1, TM), F32),
                    jnp.zeros((V_HEAD, TM), F32))
            _, l, acc = lax.fori_loop(0, nk, body, init)
            store([chain], [l], [acc])


def _attention(bounded, qt, k, vt, q_col0, n_batch, q_len, kv_len, tq, heads, group):
    nq = q_len // tq
    hb = N_HEADS // heads
    qb0 = q_col0 // tq
    nk = kv_len // TM
    return pl.pallas_call(
        functools.partial(_attn_kernel, heads=heads, nk=nk, group=group),
        grid_spec=pltpu.PrefetchScalarGridSpec(
            num_scalar_prefetch=1,
            grid=(n_batch, hb, nq),
            in_specs=[
                pl.BlockSpec((heads * HEAD_PAD, tq), lambda b, h, i, f: (h, qb0 + b * nq + i)),
                pl.BlockSpec((kv_len, heads * HEAD_PAD), lambda b, h, i, f: (b, h)),
                pl.BlockSpec((nk, heads * V_HEAD, TM), lambda b, h, i, f: (b, h, 0)),
            ],
            out_specs=pl.BlockSpec((tq, heads * V_HEAD), lambda b, h, i, f: (b * nq + i, h)),
        ),
        out_shape=jax.ShapeDtypeStruct((n_batch * q_len, N_HEADS * V_HEAD), BF16),
        compiler_params=_params(3),
        name="attention",
    )(bounded, qt, k, vt)


def _route(h2, rwh_ref, rwl_ref, rb_ref, route_ref):
    hi = h2.astype(BF16)
    lo = (h2 - hi.astype(F32)).astype(BF16)
    rwh = rwh_ref[...]
    logits = (jnp.dot(hi, rwh, preferred_element_type=F32)
              + jnp.dot(lo, rwh, preferred_element_type=F32)
              + jnp.dot(hi, rwl_ref[...], preferred_element_type=F32))
    lt = logits.T[:N_EXPERTS, :]
    scores = jax.nn.sigmoid(lt)
    sel = scores + rb_ref[...]
    srow = [sel[e:e + 1, :] for e in range(N_EXPERTS)]
    prow = [scores[e:e + 1, :] for e in range(N_EXPERTS)]

    def top2_sum(a, b, c, d):
        hab, lab = jnp.maximum(a, b), jnp.minimum(a, b)
        hcd, lcd = jnp.maximum(c, d), jnp.minimum(c, d)
        return jnp.maximum(hab, hcd) + jnp.maximum(jnp.minimum(hab, hcd), jnp.maximum(lab, lcd))

    gs = [top2_sum(*srow[4 * g:4 * g + 4]) for g in range(N_GROUPS)]
    best = jnp.zeros_like(gs[0], dtype=I32)
    bestv = gs[0]
    for g in range(1, N_GROUPS):
        upd = gs[g] > bestv
        best = jnp.where(upd, g, best)
        bestv = jnp.where(upd, gs[g], bestv)

    def pick(rows, j):
        out = rows[j]
        for g in range(1, N_GROUPS):
            out = jnp.where(best == g, rows[4 * g + j], out)
        return out

    sv = [pick(srow, j) for j in range(EXPERTS_PER_GROUP)]
    pv = [pick(prow, j) for j in range(EXPERTS_PER_GROUP)]
    i1 = jnp.zeros_like(best)
    v1 = sv[0]
    for j in range(1, EXPERTS_PER_GROUP):
        upd = sv[j] > v1
        i1 = jnp.where(upd, j, i1)
        v1 = jnp.where(upd, sv[j], v1)
    neg = jnp.float32(-jnp.inf)
    i2 = jnp.where(i1 == 0, 1, 0).astype(I32)
    v2 = jnp.where(i1 == 0, sv[1], sv[0])
    for j in range(1, EXPERTS_PER_GROUP):
        cand = jnp.where(i1 == j, neg, sv[j])
        upd = cand > v2
        i2 = jnp.where(upd, j, i2)
        v2 = jnp.where(upd, cand, v2)
    ilo = jnp.minimum(i1, i2)
    ihi = jnp.maximum(i1, i2)
    pair = jnp.where(ilo == 0, ihi - 1, jnp.where(ilo == 1, ihi + 1, N_PAIRS - 1))
    cls = best * N_PAIRS + pair

    def take(vals, idx):
        out = vals[0]
        for j in range(1, EXPERTS_PER_GROUP):
            out = jnp.where(idx == j, vals[j], out)
        return out

    wlo = take(pv, ilo)
    whi = take(pv, ihi)
    den = wlo + whi
    route_ref[...] = jnp.concatenate(
        [cls.astype(F32), wlo / den, whi / den, jnp.zeros((5, cls.shape[1]), F32)], axis=0)


def _finish_sublayer(x_new, mod_ref, nw2_ref, rwh_ref, rwl_ref, rb_ref, x_out_ref, h2p_ref, route_ref):
    x_out_ref[...] = x_new
    h2 = _rms(x_new) * nw2_ref[...] * (1.0 + mod_ref[4:5, :]) + mod_ref[3:4, :]
    _store_token_major(h2p_ref, h2)
    _route(h2, rwh_ref, rwl_ref, rb_ref, route_ref)


def _mla_out_kernel(attn_c_ref, attn_l_ref, xc_ref, xl_ref, mod_ref, wo_ref, nw2_ref, rwh_ref, rwl_ref,
                    rb_ref, x1_ref, h2p_ref, route_ref, *, n_ctx_tiles):
    is_ctx = pl.program_id(0) < n_ctx_tiles
    attn = jnp.where(is_ctx, attn_c_ref[...], attn_l_ref[...])
    mix = jnp.dot(attn, wo_ref[...], preferred_element_type=F32)
    x1 = jnp.where(is_ctx, xc_ref[...], xl_ref[...]) + mod_ref[2:3, :] * mix
    _finish_sublayer(x1, mod_ref, nw2_ref, rwh_ref, rwl_ref, rb_ref, x1_ref, h2p_ref, route_ref)


def _sublayer_out_specs(t, d):
    nt = t // TM
    specs = [
        pl.BlockSpec((TM, d), lambda i: (i, 0)),
        pl.BlockSpec((TM * ROW_CHUNKS, LANES), lambda i: (i, 0)),
        pl.BlockSpec((None, 8, TM), lambda i: (i, 0, 0)),
    ]
    shapes = [
        jax.ShapeDtypeStruct((t, d), F32),
        jax.ShapeDtypeStruct((t * ROW_CHUNKS, LANES), F32),
        jax.ShapeDtypeStruct((nt, 8, TM), F32),
    ]
    return specs, shapes


def _mla_out(attn_ctx, attn_lat, x_ctx, x_lat, mods, cond_of_tile, wo, nw2, rwh, rwl, rb):
    d = x_ctx.shape[1]
    t = x_ctx.shape[0] + x_lat.shape[0]
    assert d == ROW_CHUNKS * LANES
    const = lambda i: (0, 0)
    n_ctx_tiles = attn_ctx.shape[0] // TM
    out_specs, out_shape = _sublayer_out_specs(t, d)
    return pl.pallas_call(
        functools.partial(_mla_out_kernel, n_ctx_tiles=n_ctx_tiles),
        grid=(t // TM,),
        in_specs=_ctx_lat_specs(n_ctx_tiles, attn_ctx.shape[1]) + _ctx_lat_specs(n_ctx_tiles, d) + [
            pl.BlockSpec((None, 6, d), lambda i: (cond_of_tile(i), 0, 0)),
            pl.BlockSpec(wo.shape, const),
            pl.BlockSpec((1, d), const),
            pl.BlockSpec(rwh.shape, const),
            pl.BlockSpec(rwl.shape, const),
            pl.BlockSpec(rb.shape, const),
        ],
        out_specs=out_specs,
        out_shape=out_shape,
        compiler_params=_params(1),
        name="mla_out",
    )(attn_ctx, attn_lat, x_ctx, x_lat, mods, wo, nw2, rwh, rwl, rb)


def _pool_kernel(x_ref, mo_ref, xp_ref, mop_ref, xn_ref, mon_ref, modp_ref, mod_ref, nw_ref, pw_ref,
                 ps_ref, nw2_ref, rwh_ref, rwl_ref, rb_ref, x2_ref, h2p_ref, route_ref,
                 *, n_ctx_tiles, ctx_seq_tiles, lat_seq_tiles):
    i = pl.program_id(0)
    is_lat = i >= n_ctx_tiles
    seq_tiles = jnp.where(is_lat, lat_seq_tiles, ctx_seq_tiles)
    in_seq = jnp.where(is_lat, i - n_ctx_tiles, i) % seq_tiles
    has_prev = in_seq > 0
    has_next = in_seq < seq_tiles - 1
    g2p = modp_ref[5:6, :]
    nw = nw_ref[...]
    sc1 = 1.0 + mod_ref[1:2, :]
    sh1 = mod_ref[0:1, :]

    def pre(xv, mo):
        xx = xv + g2p * mo
        return xx, _rms(xx) * nw * sc1 + sh1

    xcur, h = pre(x_ref[...], _load_token_major(mo_ref, TM))
    _, hprev = pre(xp_ref[...], _load_token_major(mop_ref, HALO))
    _, hnext = pre(xn_ref[...], _load_token_major(mon_ref, HALO))
    d = h.shape[1]
    gw = d // len(POOL_WINDOWS)
    hb = h.astype(BF16)
    halo = jnp.concatenate(
        [hprev.astype(BF16), hnext.astype(BF16), jnp.zeros((LANES - 2 * HALO, d), BF16)], axis=0)

    t_mid = lax.broadcasted_iota(I32, (TM, TM), 0)
    e_mid = lax.broadcasted_iota(I32, (TM, TM), 1)
    t_hal = lax.broadcasted_iota(I32, (TM, LANES), 0)
    c_hal = lax.broadcasted_iota(I32, (TM, LANES), 1)
    far = jnp.int32(4 * TM)
    pos_prev = jnp.where(has_prev, c_hal - HALO, -far)
    pos_next = jnp.where(has_next, TM + c_hal - HALO, far)
    pos_hal = jnp.where(c_hal < HALO, pos_prev, jnp.where(c_hal < 2 * HALO, pos_next, far))
    t_col = lax.broadcasted_iota(I32, (TM, 1), 0)
    lo_bound = jnp.where(has_prev, -HALO, 0)
    hi_bound = jnp.where(has_next, TM - 1 + HALO, TM - 1)

    ys = []
    for g, w in enumerate(POOL_WINDOWS):
        half = w // 2
        dm = e_mid - t_mid
        band_mid = jnp.where(dm >= -half, jnp.where(dm <= half - 1, 1.0, 0.0), 0.0).astype(BF16)
        dh = pos_hal - t_hal
        band_hal = jnp.where(dh >= -half, jnp.where(dh <= half - 1, 1.0, 0.0), 0.0).astype(BF16)
        sl = slice(g * gw, (g + 1) * gw)
        wsum = (jnp.dot(band_mid, hb[:, sl], preferred_element_type=F32)
                + jnp.dot(band_hal, halo[:, sl], preferred_element_type=F32))
        cnt = (jnp.minimum(t_col + (half - 1), hi_bound)
               - jnp.maximum(t_col - half, lo_bound) + 1).astype(F32)
        pooled = wsum / cnt - h[:, sl]
        ys.append(jnp.dot(pooled.astype(BF16), pw_ref[g], preferred_element_type=F32))
    y = jnp.concatenate(ys, axis=1) * ps_ref[...]
    x2 = xcur + mod_ref[2:3, :] * y
    _finish_sublayer(x2, mod_ref, nw2_ref, rwh_ref, rwl_ref, rb_ref, x2_ref, h2p_ref, route_ref)


def _pool_layer(x1, mo, mods_prev, mods, cond_of_tile, n_ctx_tiles, ctx_seq_tiles, lat_seq_tiles,
                nw, pw, ps, nw2, rwh, rwl, rb):
    t, d = x1.shape
    const = lambda i: (0, 0)
    hb = TM // HALO
    last = t // HALO - 1
    prev_map = lambda i: (jnp.maximum(i * hb - 1, 0), 0)
    next_map = lambda i: (jnp.minimum((i + 1) * hb, last), 0)
    out_specs, out_shape = _sublayer_out_specs(t, d)
    return pl.pallas_call(
        functools.partial(_pool_kernel, n_ctx_tiles=n_ctx_tiles, ctx_seq_tiles=ctx_seq_tiles,
                          lat_seq_tiles=lat_seq_tiles),
        grid=(t // TM,),
        in_specs=[
            pl.BlockSpec((TM, d), lambda i: (i, 0)),
            pl.BlockSpec((TM * ROW_CHUNKS, LANES), lambda i: (i, 0)),
            pl.BlockSpec((HALO, d), prev_map),
            pl.BlockSpec((HALO * ROW_CHUNKS, LANES), prev_map),
            pl.BlockSpec((HALO, d), next_map),
            pl.BlockSpec((HALO * ROW_CHUNKS, LANES), next_map),
            pl.BlockSpec((None, 6, d), lambda i: (cond_of_tile(i), 0, 0)),
            pl.BlockSpec((None, 6, d), lambda i: (cond_of_tile(i), 0, 0)),
            pl.BlockSpec((1, d), const),
            pl.BlockSpec(pw.shape, lambda i: (0, 0, 0)),
            pl.BlockSpec((1, d), const),
            pl.BlockSpec((1, d), const),
            pl.BlockSpec(rwh.shape, const),
            pl.BlockSpec(rwl.shape, const),
            pl.BlockSpec(rb.shape, const),
        ],
        out_specs=out_specs,
        out_shape=out_shape,
        compiler_params=_params(1),
        name="pool_layer",
    )(x1, mo, x1, mo, x1, mo, mods_prev, mods, nw, pw, ps, nw2, rwh, rwl, rb)


def _final_kernel(x_ref, mo_ref, mod_ref, o_ref):
    o_ref[...] = x_ref[...] + mod_ref[5:6, :] * _load_token_major(mo_ref, TM)


def _final(x, mo, mods, cond_of_tile, tile0, n_tiles):
    d = x.shape[1]
    return pl.pallas_call(
        _final_kernel,
        grid=(n_tiles,),
        in_specs=[
            pl.BlockSpec((TM, d), lambda i: (tile0 + i, 0)),
            pl.BlockSpec((TM * ROW_CHUNKS, LANES), lambda i: (tile0 + i, 0)),
            pl.BlockSpec((None, 6, d), lambda i: (cond_of_tile(tile0 + i), 0, 0)),
        ],
        out_specs=pl.BlockSpec((TM, d), lambda i: (i, 0)),
        out_shape=jax.ShapeDtypeStruct((n_tiles * TM, d), F32),
        compiler_params=_params(1),
        name="final_residual",
    )(x, mo, mods)


def _moe_kernel(ea_ref, eb_ref, valid_ref, src_ref, dst_ref, h_hbm, g_ref, wga_ref, wua_ref, wda_ref,
                wgb_ref, wub_ref, wdb_ref, mo_hbm, xbuf, ybuf, sem_in, sem_out):
    i = pl.program_id(0)
    n = pl.num_programs(0)
    slot = i % 2

    def gather_copy(tok, r, slot):
        return pltpu.make_async_copy(
            h_hbm.at[pl.ds(pl.multiple_of(tok * ROW_CHUNKS, ROW_CHUNKS), ROW_CHUNKS)],
            xbuf.at[slot, pl.ds(pl.multiple_of(r * BUF_PITCH, 8), ROW_CHUNKS)], sem_in.at[slot])

    def scatter_copy(tok, r, slot):
        return pltpu.make_async_copy(
            ybuf.at[slot, pl.ds(pl.multiple_of(r * ROW_CHUNKS, ROW_CHUNKS), ROW_CHUNKS)],
            mo_hbm.at[pl.ds(pl.multiple_of(tok * ROW_CHUNKS, ROW_CHUNKS), ROW_CHUNKS)], sem_out.at[slot])

    def for_rows(fn):
        def body(r, carry):
            fn(r)
            return carry
        lax.fori_loop(0, TM, body, 0, unroll=8)

    def gather_start(tile, slot):
        for_rows(lambda r: gather_copy(src_ref[tile * TM + r], r, slot).start())

    def gather_wait(slot):
        for_rows(lambda r: gather_copy(0, 0, slot).wait())

    def scatter_start(tile, slot):
        for_rows(lambda r: scatter_copy(dst_ref[tile * TM + r], r, slot).start())

    def scatter_wait(slot):
        for_rows(lambda r: scatter_copy(0, 0, slot).wait())

    @pl.when(i == 0)
    def _():
        gather_start(0, 0)

    gather_wait(slot)

    @pl.when(i + 1 < n)
    def _():
        gather_start(i + 1, 1 - slot)

    @pl.when(i >= 2)
    def _():
        scatter_wait(slot)

    @pl.when(valid_ref[i] == 1)
    def _():
        x = _load_token_major(xbuf.at[slot], TM, BUF_PITCH).astype(BF16)

        def ffn(wg_ref, wu_ref, wd_ref):
            g = jnp.dot(x, wg_ref[...], preferred_element_type=F32)
            u = jnp.dot(x, wu_ref[...], preferred_element_type=F32)
            a = (g * jax.nn.sigmoid(g)) * u
            return jnp.dot(a.astype(BF16), wd_ref[...], preferred_element_type=F32)

        gates = g_ref[...]
        y = gates[:, 0:1] * ffn(wga_ref, wua_ref, wda_ref) + gates[:, 1:2] * ffn(wgb_ref, wub_ref, wdb_ref)
        _store_token_major(ybuf.at[slot], y)

    @pl.when(valid_ref[i] == 0)
    def _():
        ybuf[slot] = jnp.zeros(ybuf.shape[1:], F32)

    scatter_start(i, slot)

    @pl.when(i == n - 1)
    def _():
        scatter_wait(1 - slot)
        scatter_wait(slot)


def _moe_sorted(h, src_tok, dst_tok, gates, tile_ea, tile_eb, tile_valid, layer, wg, wu, wd):
    n_rows = src_tok.shape[0]
    d, ff = wg.shape[-2:]
    wa = lambda i, ea, eb, va, src, dst: (layer, ea[i], 0, 0)
    wb = lambda i, ea, eb, va, src, dst: (layer, eb[i], 0, 0)
    return pl.pallas_call(
        _moe_kernel,
        grid_spec=pltpu.PrefetchScalarGridSpec(
            num_scalar_prefetch=5,
            grid=(n_rows // TM,),
            in_specs=[
                pl.BlockSpec(memory_space=pl.ANY),
                pl.BlockSpec((TM, 2), lambda i, ea, eb, va, src, dst: (i, 0)),
                pl.BlockSpec((None, None, d, ff), wa),
                pl.BlockSpec((None, None, d, ff), wa),
                pl.BlockSpec((None, None, ff, d), wa),
                pl.BlockSpec((None, None, d, ff), wb),
                pl.BlockSpec((None, None, d, ff), wb),
                pl.BlockSpec((None, None, ff, d), wb),
            ],
            out_specs=pl.BlockSpec(memory_space=pl.ANY),
            scratch_shapes=[
                pltpu.VMEM((2, TM * BUF_PITCH, LANES), F32),
                pltpu.VMEM((2, TM * ROW_CHUNKS, LANES), F32),
                pltpu.SemaphoreType.DMA((2,)),
                pltpu.SemaphoreType.DMA((2,)),
            ],
        ),
        out_shape=jax.ShapeDtypeStruct((n_rows * ROW_CHUNKS, LANES), F32),
        compiler_params=_params(1),
        name="moe_experts",
    )(tile_ea, tile_eb, tile_valid, src_tok, dst_tok, h, gates, wg, wu, wd, wg, wu, wd)


def _moe_layer(h2p, route, layer, wg, wu, wd):
    t = h2p.shape[0] // ROW_CHUNKS
    n_tiles = t // TM + N_CLASSES
    cls = route[:, 0, :].reshape(t).astype(I32)
    g_lo = route[:, 1, :].reshape(t)
    g_hi = route[:, 2, :].reshape(t)
    onehot = (cls[:, None] == jnp.arange(N_CLASSES, dtype=I32)[None, :]).astype(I32)
    csum = jnp.cumsum(onehot, axis=0)
    counts = csum[-1]
    rank = jnp.take_along_axis(csum, cls[:, None], axis=1)[:, 0] - 1
    tiles_c = (counts + TM - 1) // TM
    tile_end = jnp.cumsum(tiles_c)
    tile_start = tile_end - tiles_c
    pos = tile_start[cls] * TM + rank
    n_rows = n_tiles * TM
    tok_ids = jnp.arange(t, dtype=I32)
    src_tok = (jnp.arange(n_rows, dtype=I32) % t).at[pos].set(tok_ids)
    is_pad = jnp.ones((n_rows,), I32).at[pos].set(0)
    dst_tok = jnp.where(is_pad == 1, t + jnp.cumsum(is_pad) - 1, src_tok)
    tile_ids = jnp.arange(n_tiles, dtype=I32)
    total = tile_end[-1]
    tile_valid = (tile_ids < total).astype(I32)
    tile_cls = jnp.sum((tile_end[None, :] <= jnp.minimum(tile_ids, total - 1)[:, None]).astype(I32), axis=1)
    pair = tile_cls % N_PAIRS
    group = tile_cls // N_PAIRS
    tile_ea = group * EXPERTS_PER_GROUP + jnp.asarray(PAIR_LO, I32)[pair]
    tile_eb = group * EXPERTS_PER_GROUP + jnp.asarray(PAIR_HI, I32)[pair]
    gates = jnp.stack([g_lo[src_tok], g_hi[src_tok]], axis=1)
    return _moe_sorted(h2p, src_tok, dst_tok, gates, tile_ea, tile_eb, tile_valid, layer, wg, wu, wd)


def _swap_rope_halves(w):
    shp = w.shape
    return w.reshape(shp[:-1] + (2, 2, AXIS_DIM // 2))[..., ::-1, :].reshape(shp)


def _rope_tables(length, norm_rope, scale):
    rows = length // GRID_W
    row = jnp.repeat(jnp.arange(rows, dtype=F32), GRID_W)
    col = jnp.tile(jnp.arange(GRID_W, dtype=F32), rows)
    inv = jnp.power(ROPE_THETA, -jnp.arange(0, AXIS_DIM, 2, dtype=F32) / AXIS_DIM)
    ang = jnp.stack([row[:, None] * inv, col[:, None] * inv], axis=1)
    cos, sin = jnp.cos(ang), jnp.sin(ang)
    c_full = jnp.stack([cos, cos], axis=2).reshape(length, ROPE_DIM)
    s_full = jnp.stack([-sin, sin], axis=2).reshape(length, ROPE_DIM)
    lat = jnp.concatenate([norm_rope * c_full, _swap_rope_halves(norm_rope) * s_full], axis=1)
    ctx = jnp.concatenate([norm_rope, jnp.zeros((ROPE_DIM,), F32)])
    ctx = jnp.broadcast_to(ctx[None, :], (TM, 2 * ROPE_DIM))
    return jnp.concatenate([ctx, lat], axis=0) * scale


def kernel(x_prompt, x_sample, cache_ckv, cache_krope, c, c_ctx, ada_w, ada_b, norm_mix_w, norm_ffn_w,
           mla_w_dq, mla_q_lora_norm, mla_w_uq, mla_w_dkv, mla_kv_lora_norm, mla_w_uk, mla_w_uv,
           mla_q_norm, mla_k_norm, mla_w_o, pool_w, pool_scale, router_w, router_bias,
           moe_w_gate, moe_w_up, moe_w_down):
    batch, seq, d = x_prompt.shape
    dec_batch, dec_seq, _ = x_sample.shape
    past = cache_ckv.shape[2]
    depth = ada_w.shape[0]
    assert seq == TM and past == TM and dec_seq % (2 * TM) == 0
    assert dec_batch + 1 <= 8 and depth == 2
    t_ctx = batch * seq
    t_lat = dec_batch * dec_seq
    n_ctx_tiles = t_ctx // TM
    lat_seq_tiles = dec_seq // TM
    ctx_seq_tiles = seq // TM

    def cond_of_tile(i):
        return jnp.where(i < n_ctx_tiles, 0, 1 + (i - n_ctx_tiles) // lat_seq_tiles)

    def table_of_tile(i):
        return jnp.where(i < n_ctx_tiles, 0, 1 + (i - n_ctx_tiles) % lat_seq_tiles)

    x_ctx = x_prompt.reshape(t_ctx, d)
    x_lat = x_sample.reshape(t_lat, d)
    conds = jnp.concatenate([c_ctx[None, :], c, jnp.zeros((7 - dec_batch, d), F32)], axis=0)
    mods = _adaln(conds, ada_w, ada_b).reshape(depth, 8, 6, d)

    rw = jnp.pad(router_w, ((0, 0), (0, LANES - N_EXPERTS)))
    rwh = rw.astype(BF16)
    rwl = (rw - rwh.astype(F32)).astype(BF16)
    rb = jnp.broadcast_to(router_bias.astype(F32)[:, None], (N_EXPERTS, TM))
    wg = moe_w_gate.astype(BF16)
    wu = moe_w_up.astype(BF16)
    wd = moe_w_down.astype(BF16)

    j = 0
    q_norm, k_norm = mla_q_norm[j], mla_k_norm[j]
    w_uq = mla_w_uq[j].reshape(-1, N_HEADS, QK_HEAD)
    w_uq = jnp.concatenate([w_uq, _swap_rope_halves(w_uq[..., QK_NOPE:])], axis=-1)
    w_uqt = w_uq.reshape(-1, N_HEADS * HEAD_PAD).T.astype(BF16)
    w_dkv = mla_w_dkv[j]
    w_dkv = jnp.concatenate([w_dkv, _swap_rope_halves(w_dkv[:, KV_LORA:])], axis=1).astype(BF16)
    q_scale = ATTN_SCALE * math.log2(math.e)
    tq = _rope_tables(dec_seq, q_norm[QK_NOPE:], q_scale).T
    tk = _rope_tables(dec_seq, k_norm[QK_NOPE:], 1.0)
    qnn = jnp.broadcast_to((q_norm[:QK_NOPE] * q_scale)[:, None], (QK_NOPE, TM))
    knn = k_norm[:QK_NOPE][None, :]

    qt, ckv, kr = _mla_in(x_ctx, x_lat, mods[0], cond_of_tile, table_of_tile, norm_mix_w[0][None, :],
                          mla_w_dq[j].astype(BF16), mla_q_lora_norm[j][None, :], w_uqt, w_dkv,
                          mla_kv_lora_norm[j][None, :], qnn, tq)
    state_ckv = ckv[:t_ctx].reshape(batch, 1, seq, KV_LORA)
    state_krope = kr[:t_ctx, :ROPE_DIM].reshape(batch, 1, seq, ROPE_DIM)

    w_uk = mla_w_uk[j].astype(BF16)
    w_uvt = mla_w_uv[j].T.astype(BF16)
    k_ctx, vt_ctx = _kv_expand(ckv[:t_ctx], kr[:t_ctx], tk, lambda i: 0, w_uk, w_uvt, knn)
    kv_len = past + dec_seq
    ckv_lat = jnp.concatenate(
        [cache_ckv[:, j], ckv[t_ctx:].reshape(dec_batch, dec_seq, KV_LORA)], axis=1)
    kr_cache = jnp.pad(cache_krope[:, j], ((0, 0), (0, 0), (0, LANES - ROPE_DIM)))
    kr_lat = jnp.concatenate([kr_cache, kr[t_ctx:].reshape(dec_batch, dec_seq, LANES)], axis=1)
    kv_seq_tiles = kv_len // TM
    k_lat, vt_lat = _kv_expand(ckv_lat.reshape(-1, KV_LORA), kr_lat.reshape(-1, LANES), tk,
                               lambda i: i % kv_seq_tiles, w_uk, w_uvt, knn)

    logit_bound = q_scale * QK_HEAD * jnp.max(jnp.abs(q_norm)) * jnp.max(jnp.abs(k_norm))
    bounded = (logit_bound <= MAX_SAFE_LOGIT).astype(I32).reshape(1)
    attn_ctx = _attention(bounded, qt, k_ctx, vt_ctx, 0, batch, seq, seq, tq=seq, heads=N_HEADS, group=4)
    attn_lat = _attention(bounded, qt, k_lat, vt_lat, t_ctx, dec_batch, dec_seq, kv_len, tq=2 * TM,
                          heads=2, group=4)

    x1, h2p, route = _mla_out(attn_ctx, attn_lat, x_ctx, x_lat, mods[0], cond_of_tile,
                              mla_w_o[j].astype(BF16), norm_ffn_w[0][None, :], rwh, rwl, rb)
    mo = _moe_layer(h2p, route, 0, wg, wu, wd)

    x2, h2p, route = _pool_layer(x1, mo, mods[0], mods[1], cond_of_tile, n_ctx_tiles, ctx_seq_tiles,
                                 lat_seq_tiles, norm_mix_w[1][None, :], pool_w[0].astype(BF16),
                                 pool_scale[0][None, :], norm_ffn_w[1][None, :], rwh, rwl, rb)
    mo = _moe_layer(h2p, route, 1, wg, wu, wd)
    y_ctx = _final(x2, mo, mods[1], cond_of_tile, 0, n_ctx_tiles)
    y_lat = _final(x2, mo, mods[1], cond_of_tile, n_ctx_tiles, t_lat // TM)

    return (y_ctx.reshape(batch, seq, d), y_lat.reshape(dec_batch, dec_seq, d), state_ckv, state_krope)
```

```python
import functools
import math

import jax
import jax.numpy as jnp
from jax import lax
from jax.experimental import pallas as pl
from jax.experimental.pallas import tpu as pltpu

F32 = jnp.float32
BF16 = jnp.bfloat16
I32 = jnp.int32

GRID_W = 64
N_HEADS = 16
QK_NOPE = 128
ROPE_DIM = 64
QK_HEAD = QK_NOPE + ROPE_DIM
V_HEAD = 128
KV_LORA = 256
ROPE_THETA = 10000.0
AXIS_DIM = ROPE_DIM // 2
ATTN_SCALE = QK_HEAD ** -0.5
POOL_WINDOWS = (2, 4, 8, 16)
N_EXPERTS = 16
N_GROUPS = 4
EXPERTS_PER_GROUP = N_EXPERTS // N_GROUPS
EPS = 1e-6

LANES = 128
HEAD_PAD = 2 * LANES
TM = 256
HALO = 16
N_PAIRS = 6
N_CLASSES = N_GROUPS * N_PAIRS
PAIR_LO = (0, 0, 0, 1, 1, 2)
PAIR_HI = (1, 2, 3, 2, 3, 3)
ROW_CHUNKS = 16
BUF_PITCH = 24
VMEM_LIMIT = 52 * 1024 * 1024
MAX_SAFE_LOGIT = 64.0


def _params(n_axes):
    return pltpu.CompilerParams(
        dimension_semantics=("arbitrary",) * n_axes, vmem_limit_bytes=VMEM_LIMIT)


def _rms(x):
    return x * lax.rsqrt(jnp.mean(x * x, axis=-1, keepdims=True) + EPS)


def _store_token_major(ref, x, pitch=ROW_CHUNKS):
    rows = x.shape[0]
    for c in range(ROW_CHUNKS):
        ref[pl.ds(c, rows, stride=pitch), :] = x[:, c * LANES:(c + 1) * LANES]


def _load_token_major(ref, rows, pitch=ROW_CHUNKS):
    return jnp.concatenate(
        [ref[pl.ds(c, rows, stride=pitch), :] for c in range(ROW_CHUNKS)], axis=1)


def _ada_kernel(cond_ref, w_ref, b_ref, o_ref):
    c = cond_ref[...]
    s = c * jax.nn.sigmoid(c)
    o_ref[...] = jnp.dot(s, w_ref[...], preferred_element_type=F32,
                         precision=lax.Precision.HIGHEST) + b_ref[...]


def _adaln(conds, ada_w, ada_b):
    depth, d, n = ada_w.shape
    tn = 1024
    return pl.pallas_call(
        _ada_kernel,
        grid=(depth, n // tn),
        in_specs=[
            pl.BlockSpec((8, d), lambda l, j: (0, 0)),
            pl.BlockSpec((None, d, tn), lambda l, j: (l, 0, j)),
            pl.BlockSpec((None, 1, tn), lambda l, j: (l, 0, j)),
        ],
        out_specs=pl.BlockSpec((None, 8, tn), lambda l, j: (l, 0, j)),
        out_shape=jax.ShapeDtypeStruct((depth, 8, n), F32),
        compiler_params=_params(2),
        name="adaln",
    )(conds, ada_w, ada_b.reshape(depth, 1, n))


def _mla_in_kernel(xc_ref, xl_ref, mod_ref, nw_ref, wdq_ref, qln_ref, wuqt_ref, wdkv_ref, kvn_ref,
                   qnn_ref, tq_ref, qt_ref, ckv_ref, kr_ref, *, n_ctx_tiles):
    x = jnp.where(pl.program_id(0) < n_ctx_tiles, xc_ref[...], xl_ref[...])
    h = _rms(x) * nw_ref[...] * (1.0 + mod_ref[1:2, :]) + mod_ref[0:1, :]
    hb = h.astype(BF16)
    kv = jnp.dot(hb, wdkv_ref[...], preferred_element_type=F32)
    ckv_ref[...] = _rms(kv[:, :KV_LORA]) * kvn_ref[...]
    kr_ref[...] = kv[:, KV_LORA:]
    cq = jnp.dot(hb, wdq_ref[...], preferred_element_type=F32)
    cqn = (_rms(cq) * qln_ref[...]).astype(BF16)
    qt = lax.dot_general(wuqt_ref[...], cqn, (((1,), (1,)), ((), ())),
                         preferred_element_type=F32)
    t_same = tq_ref[:ROPE_DIM, :]
    t_swap = tq_ref[ROPE_DIM:, :]
    qnn = qnn_ref[...]
    half = AXIS_DIM // 2
    for hd in range(N_HEADS):
        a = qt[hd * QK_HEAD: hd * QK_HEAD + QK_NOPE, :]
        b = qt[hd * QK_HEAD + QK_NOPE: (hd + 1) * QK_HEAD, :]
        ss = jnp.sum(a * a, axis=0, keepdims=True) + jnp.sum(b * b, axis=0, keepdims=True)
        r = lax.rsqrt(ss * (1.0 / QK_HEAD) + EPS)
        b_swap = jnp.concatenate([b[half:2 * half], b[:half], b[3 * half:], b[2 * half:3 * half]], axis=0)
        rot = ((b * t_same + b_swap * t_swap) * r).astype(BF16)
        qt_ref[hd * HEAD_PAD: hd * HEAD_PAD + LANES, :] = (a * r * qnn).astype(BF16)
        qt_ref[hd * HEAD_PAD + LANES: hd * HEAD_PAD + LANES + ROPE_DIM, :] = rot
        qt_ref[hd * HEAD_PAD + LANES + ROPE_DIM: (hd + 1) * HEAD_PAD, :] = rot


def _ctx_lat_specs(n_ctx_tiles, width):
    return [pl.BlockSpec((TM, width), lambda i: (jnp.minimum(i, n_ctx_tiles - 1), 0)),
            pl.BlockSpec((TM, width), lambda i: (jnp.maximum(i - n_ctx_tiles, 0), 0))]


def _mla_in(x_ctx, x_lat, mods, cond_of_tile, table_of_tile, nw, wdq, qln, wuqt, wdkv, kvn, qnn, tq):
    d = x_ctx.shape[1]
    n_ctx_tiles = x_ctx.shape[0] // TM
    t = x_ctx.shape[0] + x_lat.shape[0]
    nt = t // TM
    const = lambda i: (0, 0)
    return pl.pallas_call(
        functools.partial(_mla_in_kernel, n_ctx_tiles=n_ctx_tiles),
        grid=(nt,),
        in_specs=_ctx_lat_specs(n_ctx_tiles, d) + [
            pl.BlockSpec((None, 6, d), lambda i: (cond_of_tile(i), 0, 0)),
            pl.BlockSpec((1, d), const),
            pl.BlockSpec(wdq.shape, const),
            pl.BlockSpec((1, wdq.shape[1]), const),
            pl.BlockSpec(wuqt.shape, const),
            pl.BlockSpec(wdkv.shape, const),
            pl.BlockSpec((1, KV_LORA), const),
            pl.BlockSpec((LANES, TM), const),
            pl.BlockSpec((LANES, TM), lambda i: (0, table_of_tile(i))),
        ],
        out_specs=[
            pl.BlockSpec((N_HEADS * HEAD_PAD, TM), lambda i: (0, i)),
            pl.BlockSpec((TM, KV_LORA), lambda i: (i, 0)),
            pl.BlockSpec((TM, LANES), lambda i: (i, 0)),
        ],
        out_shape=[
            jax.ShapeDtypeStruct((N_HEADS * HEAD_PAD, t), BF16),
            jax.ShapeDtypeStruct((t, KV_LORA), F32),
            jax.ShapeDtypeStruct((t, LANES), F32),
        ],
        compiler_params=_params(1),
        name="mla_in",
    )(x_ctx, x_lat, mods, nw, wdq, qln, wuqt, wdkv, kvn, qnn, tq)


def _kv_expand_kernel(ckv_ref, kr_ref, tk_ref, wuk_ref, wuvt_ref, knn_ref, k_ref, vt_ref):
    cb = ckv_ref[...].astype(BF16)
    kn = jnp.dot(cb, wuk_ref[...], preferred_element_type=F32)
    vt_ref[...] = lax.dot_general(wuvt_ref[...], cb, (((1,), (1,)), ((), ())),
                                  preferred_element_type=F32).astype(BF16)
    kr = kr_ref[...]
    rope_lane = lax.broadcasted_iota(I32, (1, LANES), 1) < ROPE_DIM
    ssr = jnp.sum(jnp.where(rope_lane, kr * kr, 0.0), axis=-1, keepdims=True)
    bk = kr * tk_ref[...]
    rot = jnp.where(rope_lane, bk + pltpu.roll(bk, ROPE_DIM, axis=1), 0.0)
    knn = knn_ref[...]
    for hd in range(N_HEADS):
        a = kn[:, hd * LANES: (hd + 1) * LANES]
        r = lax.rsqrt((jnp.sum(a * a, axis=-1, keepdims=True) + ssr) * (1.0 / QK_HEAD) + EPS)
        k_ref[:, hd * HEAD_PAD: hd * HEAD_PAD + LANES] = (a * r * knn).astype(BF16)
        k_ref[:, hd * HEAD_PAD + LANES: (hd + 1) * HEAD_PAD] = (rot * r).astype(BF16)


def _kv_expand(ckv, kr, tk, table_of_tile, wuk, wuvt, knn):
    t = ckv.shape[0]
    const = lambda i: (0, 0)
    return pl.pallas_call(
        _kv_expand_kernel,
        grid=(t // TM,),
        in_specs=[
            pl.BlockSpec((TM, KV_LORA), lambda i: (i, 0)),
            pl.BlockSpec((TM, LANES), lambda i: (i, 0)),
            pl.BlockSpec((TM, LANES), lambda i: (table_of_tile(i), 0)),
            pl.BlockSpec(wuk.shape, const),
            pl.BlockSpec(wuvt.shape, const),
            pl.BlockSpec((1, LANES), const),
        ],
        out_specs=[
            pl.BlockSpec((TM, N_HEADS * HEAD_PAD), lambda i: (i, 0)),
            pl.BlockSpec((None, N_HEADS * V_HEAD, TM), lambda i: (i, 0, 0)),
        ],
        out_shape=[
            jax.ShapeDtypeStruct((t, N_HEADS * HEAD_PAD), BF16),
            jax.ShapeDtypeStruct((t // TM, N_HEADS * V_HEAD, TM), BF16),
        ],
        compiler_params=_params(1),
        name="kv_expand",
    )(ckv, kr, tk, wuk, wuvt, knn)


def _attn_kernel(bounded_ref, qt_ref, k_ref, vt_ref, o_ref, *, heads, nk, group):
    tq = qt_ref.shape[1]
    chains = [(hh, sub) for hh in range(heads) for sub in range(tq // TM)]

    def load_q(part):
        return [qt_ref[hh * HEAD_PAD:(hh + 1) * HEAD_PAD, sub * TM:(sub + 1) * TM] for hh, sub in part]

    def store(part, ls, accs):
        for (hh, sub), l, acc in zip(part, ls, accs):
            o_ref[sub * TM:(sub + 1) * TM, hh * V_HEAD:(hh + 1) * V_HEAD] = (acc / l).T.astype(BF16)

    @pl.when(bounded_ref[0] == 1)
    def _():
        for c0 in range(0, len(chains), group):
            part = chains[c0:c0 + group]
            qts = load_q(part)
            lps = [jnp.zeros((8, TM), F32) for _ in part]
            accs = [None for _ in part]
            p_prev = None
            for j in range(nk + 1):
                p_cur = []
                if j < nk:
                    for n, ((hh, _), qt) in enumerate(zip(part, qts)):
                        ks = k_ref[j * TM:(j + 1) * TM, hh * HEAD_PAD:(hh + 1) * HEAD_PAD]
                        p = jnp.exp2(jnp.dot(ks, qt, preferred_element_type=F32))
                        lps[n] = lps[n] + jnp.sum(p.reshape(TM // 8, 8, TM), axis=0)
                        p_cur.append(p.astype(BF16))
                if j > 0:
                    for n, (hh, _) in enumerate(part):
                        vt = vt_ref[j - 1, hh * V_HEAD:(hh + 1) * V_HEAD, :]
                        pv = jnp.dot(vt, p_prev[n], preferred_element_type=F32)
                        accs[n] = pv if accs[n] is None else accs[n] + pv
                p_prev = p_cur
            store(part, [jnp.sum(lp, axis=0, keepdims=True) for lp in lps], accs)

    @pl.when(bounded_ref[0] == 0)
    def _():
        for chain in chains:
            hh = chain[0]
            qt, = load_q([chain])

            def body(j, carry, hh=hh, qt=qt):
                m, l, acc = carry
                start = pl.multiple_of(j * TM, TM)
                ks = k_ref[pl.ds(start, TM), hh * HEAD_PAD:(hh + 1) * HEAD_PAD]
                s = jnp.dot(ks, qt, preferred_element_type=F32)
                mn = jnp.maximum(m, jnp.max(s, axis=0, keepdims=True))
                alpha = jnp.exp2(m - mn)
                p = jnp.exp2(s - mn)
                l = alpha * l + jnp.sum(p, axis=0, keepdims=True)
                vt = vt_ref[j, hh * V_HEAD:(hh + 1) * V_HEAD, :]
                acc = alpha * acc + jnp.dot(vt, p.astype(BF16), preferred_element_type=F32)
                return mn, l, acc

            init = (jnp.full((1, TM), -jnp.inf, F32), jnp.zeros((1, TM), F32),
                    jnp.zeros((V_HEAD, TM), F32))
            _, l, acc = lax.fori_loop(0, nk, body, init)
            store([chain], [l], [acc])


def _attention(bounded, qt, k, vt, q_col0, n_batch, q_len, kv_len, tq, heads, group):
    nq = q_len // tq
    hb = N_HEADS // heads
    qb0 = q_col0 // tq
    nk = kv_len // TM
    return pl.pallas_call(
        functools.partial(_attn_kernel, heads=heads, nk=nk, group=group),
        grid_spec=pltpu.PrefetchScalarGridSpec(
            num_scalar_prefetch=1,
            grid=(n_batch, hb, nq),
            in_specs=[
                pl.BlockSpec((heads * HEAD_PAD, tq), lambda b, h, i, f: (h, qb0 + b * nq + i)),
                pl.BlockSpec((kv_len, heads * HEAD_PAD), lambda b, h, i, f: (b, h)),
                pl.BlockSpec((nk, heads * V_HEAD, TM), lambda b, h, i, f: (b, h, 0)),
            ],
            out_specs=pl.BlockSpec((tq, heads * V_HEAD), lambda b, h, i, f: (b * nq + i, h)),
        ),
        out_shape=jax.ShapeDtypeStruct((n_batch * q_len, N_HEADS * V_HEAD), BF16),
        compiler_params=_params(3),
        name="attention",
    )(bounded, qt, k, vt)


def _route(h2, rwh_ref, rwl_ref, rb_ref, route_ref):
    hi = h2.astype(BF16)
    lo = (h2 - hi.astype(F32)).astype(BF16)
    rwh = rwh_ref[...]
    logits = (jnp.dot(hi, rwh, preferred_element_type=F32)
              + jnp.dot(lo, rwh, preferred_element_type=F32)
              + jnp.dot(hi, rwl_ref[...], preferred_element_type=F32))
    lt = logits.T[:N_EXPERTS, :]
    scores = jax.nn.sigmoid(lt)
    sel = scores + rb_ref[...]
    srow = [sel[e:e + 1, :] for e in range(N_EXPERTS)]
    prow = [scores[e:e + 1, :] for e in range(N_EXPERTS)]

    def top2_sum(a, b, c, d):
        hab, lab = jnp.maximum(a, b), jnp.minimum(a, b)
        hcd, lcd = jnp.maximum(c, d), jnp.minimum(c, d)
        return jnp.maximum(hab, hcd) + jnp.maximum(jnp.minimum(hab, hcd), jnp.maximum(lab, lcd))

    gs = [top2_sum(*srow[4 * g:4 * g + 4]) for g in range(N_GROUPS)]
    best = jnp.zeros_like(gs[0], dtype=I32)
    bestv = gs[0]
    for g in range(1, N_GROUPS):
        upd = gs[g] > bestv
        best = jnp.where(upd, g, best)
        bestv = jnp.where(upd, gs[g], bestv)

    def pick(rows, j):
        out = rows[j]
        for g in range(1, N_GROUPS):
            out = jnp.where(best == g, rows[4 * g + j], out)
        return out

    sv = [pick(srow, j) for j in range(EXPERTS_PER_GROUP)]
    pv = [pick(prow, j) for j in range(EXPERTS_PER_GROUP)]
    i1 = jnp.zeros_like(best)
    v1 = sv[0]
    for j in range(1, EXPERTS_PER_GROUP):
        upd = sv[j] > v1
        i1 = jnp.where(upd, j, i1)
        v1 = jnp.where(upd, sv[j], v1)
    neg = jnp.float32(-jnp.inf)
    i2 = jnp.where(i1 == 0, 1, 0).astype(I32)
    v2 = jnp.where(i1 == 0, sv[1], sv[0])
    for j in range(1, EXPERTS_PER_GROUP):
        cand = jnp.where(i1 == j, neg, sv[j])
        upd = cand > v2
        i2 = jnp.where(upd, j, i2)
        v2 = jnp.where(upd, cand, v2)
    ilo = jnp.minimum(i1, i2)
    ihi = jnp.maximum(i1, i2)
    pair = jnp.where(ilo == 0, ihi - 1, jnp.where(ilo == 1, ihi + 1, N_PAIRS - 1))
    cls = best * N_PAIRS + pair

    def take(vals, idx):
        out = vals[0]
        for j in range(1, EXPERTS_PER_GROUP):
            out = jnp.where(idx == j, vals[j], out)
        return out

    wlo = take(pv, ilo)
    whi = take(pv, ihi)
    den = wlo + whi
    route_ref[...] = jnp.concatenate(
        [cls.astype(F32), wlo / den, whi / den, jnp.zeros((5, cls.shape[1]), F32)], axis=0)


def _finish_sublayer(x_new, mod_ref, nw2_ref, rwh_ref, rwl_ref, rb_ref, x_out_ref, h2p_ref, route_ref):
    x_out_ref[...] = x_new
    h2 = _rms(x_new) * nw2_ref[...] * (1.0 + mod_ref[4:5, :]) + mod_ref[3:4, :]
    _store_token_major(h2p_ref, h2)
    _route(h2, rwh_ref, rwl_ref, rb_ref, route_ref)


def _mla_out_kernel(attn_c_ref, attn_l_ref, xc_ref, xl_ref, mod_ref, wo_ref, nw2_ref, rwh_ref, rwl_ref,
                    rb_ref, x1_ref, h2p_ref, route_ref, *, n_ctx_tiles):
    is_ctx = pl.program_id(0) < n_ctx_tiles
    attn = jnp.where(is_ctx, attn_c_ref[...], attn_l_ref[...])
    mix = jnp.dot(attn, wo_ref[...], preferred_element_type=F32)
    x1 = jnp.where(is_ctx, xc_ref[...], xl_ref[...]) + mod_ref[2:3, :] * mix
    _finish_sublayer(x1, mod_ref, nw2_ref, rwh_ref, rwl_ref, rb_ref, x1_ref, h2p_ref, route_ref)


def _sublayer_out_specs(t, d):
    nt = t // TM
    specs = [
        pl.BlockSpec((TM, d), lambda i: (i, 0)),
        pl.BlockSpec((TM * ROW_CHUNKS, LANES), lambda i: (i, 0)),
        pl.BlockSpec((None, 8, TM), lambda i: (i, 0, 0)),
    ]
    shapes = [
        jax.ShapeDtypeStruct((t, d), F32),
        jax.ShapeDtypeStruct((t * ROW_CHUNKS, LANES), F32),
        jax.ShapeDtypeStruct((nt, 8, TM), F32),
    ]
    return specs, shapes


def _mla_out(attn_ctx, attn_lat, x_ctx, x_lat, mods, cond_of_tile, wo, nw2, rwh, rwl, rb):
    d = x_ctx.shape[1]
    t = x_ctx.shape[0] + x_lat.shape[0]
    assert d == ROW_CHUNKS * LANES
    const = lambda i: (0, 0)
    n_ctx_tiles = attn_ctx.shape[0] // TM
    out_specs, out_shape = _sublayer_out_specs(t, d)
    return pl.pallas_call(
        functools.partial(_mla_out_kernel, n_ctx_tiles=n_ctx_tiles),
        grid=(t // TM,),
        in_specs=_ctx_lat_specs(n_ctx_tiles, attn_ctx.shape[1]) + _ctx_lat_specs(n_ctx_tiles, d) + [
            pl.BlockSpec((None, 6, d), lambda i: (cond_of_tile(i), 0, 0)),
            pl.BlockSpec(wo.shape, const),
            pl.BlockSpec((1, d), const),
            pl.BlockSpec(rwh.shape, const),
            pl.BlockSpec(rwl.shape, const),
            pl.BlockSpec(rb.shape, const),
        ],
        out_specs=out_specs,
        out_shape=out_shape,
        compiler_params=_params(1),
        name="mla_out",
    )(attn_ctx, attn_lat, x_ctx, x_lat, mods, wo, nw2, rwh, rwl, rb)


def _pool_kernel(x_ref, mo_ref, xp_ref, mop_ref, xn_ref, mon_ref, modp_ref, mod_ref, nw_ref, pw_ref,
                 ps_ref, nw2_ref, rwh_ref, rwl_ref, rb_ref, x2_ref, h2p_ref, route_ref,
                 *, n_ctx_tiles, ctx_seq_tiles, lat_seq_tiles):
    i = pl.program_id(0)
    is_lat = i >= n_ctx_tiles
    seq_tiles = jnp.where(is_lat, lat_seq_tiles, ctx_seq_tiles)
    in_seq = jnp.where(is_lat, i - n_ctx_tiles, i) % seq_tiles
    has_prev = in_seq > 0
    has_next = in_seq < seq_tiles - 1
    g2p = modp_ref[5:6, :]
    nw = nw_ref[...]
    sc1 = 1.0 + mod_ref[1:2, :]
    sh1 = mod_ref[0:1, :]

    def pre(xv, mo):
        xx = xv + g2p * mo
        return xx, _rms(xx) * nw * sc1 + sh1

    xcur, h = pre(x_ref[...], _load_token_major(mo_ref, TM))
    _, hprev = pre(xp_ref[...], _load_token_major(mop_ref, HALO))
    _, hnext = pre(xn_ref[...], _load_token_major(mon_ref, HALO))
    d = h.shape[1]
    gw = d // len(POOL_WINDOWS)
    hb = h.astype(BF16)
    halo = jnp.concatenate(
        [hprev.astype(BF16), hnext.astype(BF16), jnp.zeros((LANES - 2 * HALO, d), BF16)], axis=0)

    t_mid = lax.broadcasted_iota(I32, (TM, TM), 0)
    e_mid = lax.broadcasted_iota(I32, (TM, TM), 1)
    t_hal = lax.broadcasted_iota(I32, (TM, LANES), 0)
    c_hal = lax.broadcasted_iota(I32, (TM, LANES), 1)
    far = jnp.int32(4 * TM)
    pos_prev = jnp.where(has_prev, c_hal - HALO, -far)
    pos_next = jnp.where(has_next, TM + c_hal - HALO, far)
    pos_hal = jnp.where(c_hal < HALO, pos_prev, jnp.where(c_hal < 2 * HALO, pos_next, far))
    t_col = lax.broadcasted_iota(I32, (TM, 1), 0)
    lo_bound = jnp.where(has_prev, -HALO, 0)
    hi_bound = jnp.where(has_next, TM - 1 + HALO, TM - 1)

    ys = []
    for g, w in enumerate(POOL_WINDOWS):
        half = w // 2
        dm = e_mid - t_mid
        band_mid = jnp.where(dm >= -half, jnp.where(dm <= half - 1, 1.0, 0.0), 0.0).astype(BF16)
        dh = pos_hal - t_hal
        band_hal = jnp.where(dh >= -half, jnp.where(dh <= half - 1, 1.0, 0.0), 0.0).astype(BF16)
        sl = slice(g * gw, (g + 1) * gw)
        wsum = (jnp.dot(band_mid, hb[:, sl], preferred_element_type=F32)
                + jnp.dot(band_hal, halo[:, sl], preferred_element_type=F32))
        cnt = (jnp.minimum(t_col + (half - 1), hi_bound)
               - jnp.maximum(t_col - half, lo_bound) + 1).astype(F32)
        pooled = wsum / cnt - h[:, sl]
        ys.append(jnp.dot(pooled.astype(BF16), pw_ref[g], preferred_element_type=F32))
    y = jnp.concatenate(ys, axis=1) * ps_ref[...]
    x2 = xcur + mod_ref[2:3, :] * y
    _finish_sublayer(x2, mod_ref, nw2_ref, rwh_ref, rwl_ref, rb_ref, x2_ref, h2p_ref, route_ref)


def _pool_layer(x1, mo, mods_prev, mods, cond_of_tile, n_ctx_tiles, ctx_seq_tiles, lat_seq_tiles,
                nw, pw, ps, nw2, rwh, rwl, rb):
    t, d = x1.shape
    const = lambda i: (0, 0)
    hb = TM // HALO
    last = t // HALO - 1
    prev_map = lambda i: (jnp.maximum(i * hb - 1, 0), 0)
    next_map = lambda i: (jnp.minimum((i + 1) * hb, last), 0)
    out_specs, out_shape = _sublayer_out_specs(t, d)
    return pl.pallas_call(
        functools.partial(_pool_kernel, n_ctx_tiles=n_ctx_tiles, ctx_seq_tiles=ctx_seq_tiles,
                          lat_seq_tiles=lat_seq_tiles),
        grid=(t // TM,),
        in_specs=[
            pl.BlockSpec((TM, d), lambda i: (i, 0)),
            pl.BlockSpec((TM * ROW_CHUNKS, LANES), lambda i: (i, 0)),
            pl.BlockSpec((HALO, d), prev_map),
            pl.BlockSpec((HALO * ROW_CHUNKS, LANES), prev_map),
            pl.BlockSpec((HALO, d), next_map),
            pl.BlockSpec((HALO * ROW_CHUNKS, LANES), next_map),
            pl.BlockSpec((None, 6, d), lambda i: (cond_of_tile(i), 0, 0)),
            pl.BlockSpec((None, 6, d), lambda i: (cond_of_tile(i), 0, 0)),
            pl.BlockSpec((1, d), const),
            pl.BlockSpec(pw.shape, lambda i: (0, 0, 0)),
            pl.BlockSpec((1, d), const),
            pl.BlockSpec((1, d), const),
            pl.BlockSpec(rwh.shape, const),
            pl.BlockSpec(rwl.shape, const),
            pl.BlockSpec(rb.shape, const),
        ],
        out_specs=out_specs,
        out_shape=out_shape,
        compiler_params=_params(1),
        name="pool_layer",
    )(x1, mo, x1, mo, x1, mo, mods_prev, mods, nw, pw, ps, nw2, rwh, rwl, rb)


def _final_kernel(x_ref, mo_ref, mod_ref, o_ref):
    o_ref[...] = x_ref[...] + mod_ref[5:6, :] * _load_token_major(mo_ref, TM)


def _final(x, mo, mods, cond_of_tile, tile0, n_tiles):
    d = x.shape[1]
    return pl.pallas_call(
        _final_kernel,
        grid=(n_tiles,),
        in_specs=[
            pl.BlockSpec((TM, d), lambda i: (tile0 + i, 0)),
            pl.BlockSpec((TM * ROW_CHUNKS, LANES), lambda i: (tile0 + i, 0)),
            pl.BlockSpec((None, 6, d), lambda i: (cond_of_tile(tile0 + i), 0, 0)),
        ],
        out_specs=pl.BlockSpec((TM, d), lambda i: (i, 0)),
        out_shape=jax.ShapeDtypeStruct((n_tiles * TM, d), F32),
        compiler_params=_params(1),
        name="final_residual",
    )(x, mo, mods)


def _moe_kernel(ea_ref, eb_ref, valid_ref, src_ref, dst_ref, h_hbm, g_ref, wga_ref, wua_ref, wda_ref,
                wgb_ref, wub_ref, wdb_ref, mo_hbm, xbuf, ybuf, sem_in, sem_out):
    i = pl.program_id(0)
    slot = i % 2
    nslot = 1 - slot
    is_valid = valid_ref[i] == 1
    prev_valid = jnp.logical_and(i >= 1, valid_ref[jnp.maximum(i - 1, 0)] == 1)
    prev2_valid = jnp.logical_and(i >= 2, valid_ref[jnp.maximum(i - 2, 0)] == 1)

    def gather_copy(tok, r, slot):
        return pltpu.make_async_copy(
            h_hbm.at[pl.ds(pl.multiple_of(tok * ROW_CHUNKS, ROW_CHUNKS), ROW_CHUNKS)],
            xbuf.at[slot, pl.ds(pl.multiple_of(r * BUF_PITCH, 8), ROW_CHUNKS)], sem_in.at[slot])

    def scatter_copy(tok, r, slot):
        return pltpu.make_async_copy(
            ybuf.at[slot, pl.ds(pl.multiple_of(r * ROW_CHUNKS, ROW_CHUNKS), ROW_CHUNKS)],
            mo_hbm.at[pl.ds(pl.multiple_of(tok * ROW_CHUNKS, ROW_CHUNKS), ROW_CHUNKS)], sem_out.at[slot])

    def for_rows(fn):
        def body(r, carry):
            fn(r)
            return carry
        lax.fori_loop(0, TM, body, 0, unroll=8)

    def gather_start(tile, slot):
        for_rows(lambda r: gather_copy(src_ref[tile * TM + r], r, slot).start())

    def gather_wait(slot):
        for_rows(lambda r: gather_copy(0, 0, slot).wait())

    def scatter_start(tile, slot):
        for_rows(lambda r: scatter_copy(dst_ref[tile * TM + r], r, slot).start())

    def scatter_wait(slot):
        for_rows(lambda r: scatter_copy(0, 0, slot).wait())

    @pl.when(i == 0)
    def _():
        gather_start(0, 0)

    @pl.when(prev2_valid)
    def _():
        scatter_wait(slot)

    @pl.when(is_valid)
    def _():
        gather_wait(slot)
        for r in range(TM):
            gather_copy(src_ref[(i + 1) * TM + r], r, nslot).start()
        x = _load_token_major(xbuf.at[slot], TM, BUF_PITCH).astype(BF16)

        def ffn(wg_ref, wu_ref, wd_ref):
            g = jnp.dot(x, wg_ref[...], preferred_element_type=F32)
            u = jnp.dot(x, wu_ref[...], preferred_element_type=F32)
            a = (g * jax.nn.sigmoid(g)) * u
            return jnp.dot(a.astype(BF16), wd_ref[...], preferred_element_type=F32)

        gates = g_ref[...]
        y = gates[:, 0:1] * ffn(wga_ref, wua_ref, wda_ref) + gates[:, 1:2] * ffn(wgb_ref, wub_ref, wdb_ref)
        _store_token_major(ybuf.at[slot], y)
        scatter_start(i, slot)

    @pl.when(jnp.logical_not(is_valid))
    def _():
        @pl.when(prev_valid)
        def _():
            gather_wait(slot)

        ybuf[slot] = jnp.zeros(ybuf.shape[1:], F32)
        row0 = pl.multiple_of(dst_ref[i * TM] * ROW_CHUNKS, ROW_CHUNKS)
        fill = pltpu.make_async_copy(ybuf.at[slot], mo_hbm.at[pl.ds(row0, TM * ROW_CHUNKS)],
                                     sem_out.at[slot])
        fill.start()
        fill.wait()


def _moe_sorted(h, src_tok, dst_tok, gates, tile_ea, tile_eb, tile_valid, layer, wg, wu, wd):
    n_rows = src_tok.shape[0]
    d, ff = wg.shape[-2:]
    wa = lambda i, ea, eb, va, src, dst: (layer, ea[i], 0, 0)
    wb = lambda i, ea, eb, va, src, dst: (layer, eb[i], 0, 0)
    return pl.pallas_call(
        _moe_kernel,
        grid_spec=pltpu.PrefetchScalarGridSpec(
            num_scalar_prefetch=5,
            grid=(n_rows // TM,),
            in_specs=[
                pl.BlockSpec(memory_space=pl.ANY),
                pl.BlockSpec((TM, 2), lambda i, ea, eb, va, src, dst: (i, 0)),
                pl.BlockSpec((None, None, d, ff), wa),
                pl.BlockSpec((None, None, d, ff), wa),
                pl.BlockSpec((None, None, ff, d), wa),
                pl.BlockSpec((None, None, d, ff), wb),
                pl.BlockSpec((None, None, d, ff), wb),
                pl.BlockSpec((None, None, ff, d), wb),
            ],
            out_specs=pl.BlockSpec(memory_space=pl.ANY),
            scratch_shapes=[
                pltpu.VMEM((2, TM * BUF_PITCH, LANES), F32),
                pltpu.VMEM((2, TM * ROW_CHUNKS, LANES), F32),
                pltpu.SemaphoreType.DMA((2,)),
                pltpu.SemaphoreType.DMA((2,)),
            ],
        ),
        out_shape=jax.ShapeDtypeStruct((n_rows * ROW_CHUNKS, LANES), F32),
        compiler_params=_params(1),
        name="moe_experts",
    )(tile_ea, tile_eb, tile_valid, src_tok, dst_tok, h, gates, wg, wu, wd, wg, wu, wd)


def _moe_layer(h2p, route, layer, wg, wu, wd):
    t = h2p.shape[0] // ROW_CHUNKS
    n_tiles = t // TM + N_CLASSES + 2
    cls = route[:, 0, :].reshape(t).astype(I32)
    g_lo = route[:, 1, :].reshape(t)
    g_hi = route[:, 2, :].reshape(t)
    onehot = (cls[:, None] == jnp.arange(N_CLASSES, dtype=I32)[None, :]).astype(I32)
    csum = jnp.cumsum(onehot, axis=0)
    counts = csum[-1]
    rank = jnp.take_along_axis(csum, cls[:, None], axis=1)[:, 0] - 1
    tiles_c = (counts + TM - 1) // TM
    tile_end = jnp.cumsum(tiles_c)
    tile_start = tile_end - tiles_c
    pos = tile_start[cls] * TM + rank
    n_rows = n_tiles * TM
    row_tok = jnp.full((n_rows,), -1, I32).at[pos].set(jnp.arange(t, dtype=I32))
    is_pad = row_tok < 0
    src_tok = jnp.where(is_pad, jnp.arange(n_rows, dtype=I32) % t, row_tok)
    dst_tok = jnp.where(is_pad, t + jnp.cumsum(is_pad.astype(I32)) - 1, row_tok)
    tile_ids = jnp.arange(n_tiles, dtype=I32)
    total = tile_end[-1]
    tile_valid = (tile_ids < total).astype(I32)
    tile_cls = jnp.sum((tile_end[None, :] <= jnp.minimum(tile_ids, total - 1)[:, None]).astype(I32), axis=1)
    pair = tile_cls % N_PAIRS
    group = tile_cls // N_PAIRS
    tile_ea = group * EXPERTS_PER_GROUP + jnp.asarray(PAIR_LO, I32)[pair]
    tile_eb = group * EXPERTS_PER_GROUP + jnp.asarray(PAIR_HI, I32)[pair]
    gates = jnp.stack([g_lo[src_tok], g_hi[src_tok]], axis=1)
    return _moe_sorted(h2p, src_tok, dst_tok, gates, tile_ea, tile_eb, tile_valid, layer, wg, wu, wd)


def _swap_rope_halves(w):
    shp = w.shape
    return w.reshape(shp[:-1] + (2, 2, AXIS_DIM // 2))[..., ::-1, :].reshape(shp)


def _rope_tables(length, norm_rope, scale):
    rows = length // GRID_W
    row = jnp.repeat(jnp.arange(rows, dtype=F32), GRID_W)
    col = jnp.tile(jnp.arange(GRID_W, dtype=F32), rows)
    inv = jnp.power(ROPE_THETA, -jnp.arange(0, AXIS_DIM, 2, dtype=F32) / AXIS_DIM)
    ang = jnp.stack([row[:, None] * inv, col[:, None] * inv], axis=1)
    cos, sin = jnp.cos(ang), jnp.sin(ang)
    c_full = jnp.stack([cos, cos], axis=2).reshape(length, ROPE_DIM)
    s_full = jnp.stack([-sin, sin], axis=2).reshape(length, ROPE_DIM)
    lat = jnp.concatenate([norm_rope * c_full, _swap_rope_halves(norm_rope) * s_full], axis=1)
    ctx = jnp.concatenate([norm_rope, jnp.zeros((ROPE_DIM,), F32)])
    ctx = jnp.broadcast_to(ctx[None, :], (TM, 2 * ROPE_DIM))
    return jnp.concatenate([ctx, lat], axis=0) * scale


def kernel(x_prompt, x_sample, cache_ckv, cache_krope, c, c_ctx, ada_w, ada_b, norm_mix_w, norm_ffn_w,
           mla_w_dq, mla_q_lora_norm, mla_w_uq, mla_w_dkv, mla_kv_lora_norm, mla_w_uk, mla_w_uv,
           mla_q_norm, mla_k_norm, mla_w_o, pool_w, pool_scale, router_w, router_bias,
           moe_w_gate, moe_w_up, moe_w_down):
    batch, seq, d = x_prompt.shape
    dec_batch, dec_seq, _ = x_sample.shape
    past = cache_ckv.shape[2]
    depth = ada_w.shape[0]
    assert seq == TM and past == TM and dec_seq % (2 * TM) == 0
    assert dec_batch + 1 <= 8 and depth == 2
    t_ctx = batch * seq
    t_lat = dec_batch * dec_seq
    n_ctx_tiles = t_ctx // TM
    lat_seq_tiles = dec_seq // TM
    ctx_seq_tiles = seq // TM

    def cond_of_tile(i):
        return jnp.where(i < n_ctx_tiles, 0, 1 + (i - n_ctx_tiles) // lat_seq_tiles)

    def table_of_tile(i):
        return jnp.where(i < n_ctx_tiles, 0, 1 + (i - n_ctx_tiles) % lat_seq_tiles)

    x_ctx = x_prompt.reshape(t_ctx, d)
    x_lat = x_sample.reshape(t_lat, d)
    conds = jnp.concatenate([c_ctx[None, :], c, jnp.zeros((7 - dec_batch, d), F32)], axis=0)
    mods = _adaln(conds, ada_w, ada_b).reshape(depth, 8, 6, d)

    rw = jnp.pad(router_w, ((0, 0), (0, LANES - N_EXPERTS)))
    rwh = rw.astype(BF16)
    rwl = (rw - rwh.astype(F32)).astype(BF16)
    rb = jnp.broadcast_to(router_bias.astype(F32)[:, None], (N_EXPERTS, TM))
    wg = moe_w_gate.astype(BF16)
    wu = moe_w_up.astype(BF16)
    wd = moe_w_down.astype(BF16)

    j = 0
    q_norm, k_norm = mla_q_norm[j], mla_k_norm[j]
    w_uqt = mla_w_uq[j].T.astype(BF16)
    w_dkv = mla_w_dkv[j]
    w_dkv = jnp.concatenate([w_dkv, _swap_rope_halves(w_dkv[:, KV_LORA:])], axis=1).astype(BF16)
    q_scale = ATTN_SCALE * math.log2(math.e)
    tq = _rope_tables(dec_seq, q_norm[QK_NOPE:], q_scale).T
    tk = _rope_tables(dec_seq, k_norm[QK_NOPE:], 1.0)
    qnn = jnp.broadcast_to((q_norm[:QK_NOPE] * q_scale)[:, None], (QK_NOPE, TM))
    knn = k_norm[:QK_NOPE][None, :]

    qt, ckv, kr = _mla_in(x_ctx, x_lat, mods[0], cond_of_tile, table_of_tile, norm_mix_w[0][None, :],
                          mla_w_dq[j].astype(BF16), mla_q_lora_norm[j][None, :], w_uqt, w_dkv,
                          mla_kv_lora_norm[j][None, :], qnn, tq)
    state_ckv = ckv[:t_ctx].reshape(batch, 1, seq, KV_LORA)
    state_krope = kr[:t_ctx, :ROPE_DIM].reshape(batch, 1, seq, ROPE_DIM)

    w_uk = mla_w_uk[j].astype(BF16)
    w_uvt = mla_w_uv[j].T.astype(BF16)
    k_ctx, vt_ctx = _kv_expand(ckv[:t_ctx], kr[:t_ctx], tk, lambda i: 0, w_uk, w_uvt, knn)
    kv_len = past + dec_seq
    ckv_lat = jnp.concatenate(
        [cache_ckv[:, j], ckv[t_ctx:].reshape(dec_batch, dec_seq, KV_LORA)], axis=1)
    kr_cache = jnp.pad(cache_krope[:, j], ((0, 0), (0, 0), (0, LANES - ROPE_DIM)))
    kr_lat = jnp.concatenate([kr_cache, kr[t_ctx:].reshape(dec_batch, dec_seq, LANES)], axis=1)
    kv_seq_tiles = kv_len // TM
    k_lat, vt_lat = _kv_expand(ckv_lat.reshape(-1, KV_LORA), kr_lat.reshape(-1, LANES), tk,
                               lambda i: i % kv_seq_tiles, w_uk, w_uvt, knn)

    logit_bound = q_scale * QK_HEAD * jnp.max(jnp.abs(q_norm)) * jnp.max(jnp.abs(k_norm))
    bounded = (logit_bound <= MAX_SAFE_LOGIT).astype(I32).reshape(1)
    attn_ctx = _attention(bounded, qt, k_ctx, vt_ctx, 0, batch, seq, seq, tq=seq, heads=N_HEADS, group=4)
    attn_lat = _attention(bounded, qt, k_lat, vt_lat, t_ctx, dec_batch, dec_seq, kv_len, tq=2 * TM,
                          heads=2, group=4)

    x1, h2p, route = _mla_out(attn_ctx, attn_lat, x_ctx, x_lat, mods[0], cond_of_tile,
                              mla_w_o[j].astype(BF16), norm_ffn_w[0][None, :], rwh, rwl, rb)
    mo = _moe_layer(h2p, route, 0, wg, wu, wd)

    x2, h2p, route = _pool_layer(x1, mo, mods[0], mods[1], cond_of_tile, n_ctx_tiles, ctx_seq_tiles,
                                 lat_seq_tiles, norm_mix_w[1][None, :], pool_w[0].astype(BF16),
                                 pool_scale[0][None, :], norm_ffn_w[1][None, :], rwh, rwl, rb)
    mo = _moe_layer(h2p, route, 1, wg, wu, wd)
    y_ctx = _final(x2, mo, mods[1], cond_of_tile, 0, n_ctx_tiles)
    y_lat = _final(x2, mo, mods[1], cond_of_tile, n_ctx_tiles, t_lat // TM)

    return (y_ctx.reshape(batch, seq, d), y_lat.reshape(dec_batch, dec_seq, d), state_ckv, state_krope)
```

```python
import functools
import math

import jax
import jax.numpy as jnp
from jax import lax
from jax.experimental import pallas as pl
from jax.experimental.pallas import tpu as pltpu

F32 = jnp.float32
BF16 = jnp.bfloat16
I32 = jnp.int32

GRID_W = 64
N_HEADS = 16
QK_NOPE = 128
ROPE_DIM = 64
QK_HEAD = QK_NOPE + ROPE_DIM
V_HEAD = 128
KV_LORA = 256
ROPE_THETA = 10000.0
AXIS_DIM = ROPE_DIM // 2
ATTN_SCALE = QK_HEAD ** -0.5
POOL_WINDOWS = (2, 4, 8, 16)
N_EXPERTS = 16
N_GROUPS = 4
EXPERTS_PER_GROUP = N_EXPERTS // N_GROUPS
EPS = 1e-6

LANES = 128
MXU_N = 256
HEAD_PAD = 2 * LANES
TM = 256
HALO = 16
N_PAIRS = 6
N_CLASSES = N_GROUPS * N_PAIRS
PAIR_A = (0, 0, 0, 1, 1, 3)
PAIR_B = (1, 2, 3, 3, 2, 2)
ROW_CHUNKS = 16
BUF_PITCH = 24
VMEM_LIMIT = 52 * 1024 * 1024
MAX_SAFE_LOGIT = 64.0


def _params(n_axes):
    return pltpu.CompilerParams(
        dimension_semantics=("arbitrary",) * n_axes, vmem_limit_bytes=VMEM_LIMIT)


def _rms(x):
    return x * lax.rsqrt(jnp.mean(x * x, axis=-1, keepdims=True) + EPS)


def _store_token_major(ref, x, pitch=ROW_CHUNKS):
    rows = x.shape[0]
    for c in range(ROW_CHUNKS):
        ref[pl.ds(c, rows, stride=pitch), :] = x[:, c * LANES:(c + 1) * LANES]


def _load_token_major(ref, rows, pitch=ROW_CHUNKS):
    return jnp.concatenate(
        [ref[pl.ds(c, rows, stride=pitch), :] for c in range(ROW_CHUNKS)], axis=1)


def _ada_kernel(cond_ref, w_ref, b_ref, o_ref):
    c = cond_ref[...]
    s = c * jax.nn.sigmoid(c)
    o_ref[...] = jnp.dot(s, w_ref[...], preferred_element_type=F32,
                         precision=lax.Precision.HIGHEST) + b_ref[...]


def _adaln(conds, ada_w, ada_b):
    depth, d, n = ada_w.shape
    tn = 1024
    return pl.pallas_call(
        _ada_kernel,
        grid=(depth, n // tn),
        in_specs=[
            pl.BlockSpec((8, d), lambda l, j: (0, 0)),
            pl.BlockSpec((None, d, tn), lambda l, j: (l, 0, j)),
            pl.BlockSpec((None, 1, tn), lambda l, j: (l, 0, j)),
        ],
        out_specs=pl.BlockSpec((None, 8, tn), lambda l, j: (l, 0, j)),
        out_shape=jax.ShapeDtypeStruct((depth, 8, n), F32),
        compiler_params=_params(2),
        name="adaln",
    )(conds, ada_w, ada_b.reshape(depth, 1, n))


def _mla_in_kernel(xc_ref, xl_ref, mod_ref, nw_ref, wdq_ref, qln_ref, wuqt_ref, wdkv_ref, kvn_ref,
                   qnn_ref, tq_ref, qt_ref, ckv_ref, kr_ref, *, n_ctx_tiles):
    x = jnp.where(pl.program_id(0) < n_ctx_tiles, xc_ref[...], xl_ref[...])
    h = _rms(x) * nw_ref[...] * (1.0 + mod_ref[1:2, :]) + mod_ref[0:1, :]
    hb = h.astype(BF16)
    kv = jnp.dot(hb, wdkv_ref[...], preferred_element_type=F32)
    ckv_ref[...] = _rms(kv[:, :KV_LORA]) * kvn_ref[...]
    kr_ref[...] = kv[:, KV_LORA:]
    cq = jnp.dot(hb, wdq_ref[...], preferred_element_type=F32)
    cqn = (_rms(cq) * qln_ref[...]).astype(BF16)
    qt = lax.dot_general(wuqt_ref[...], cqn, (((1,), (1,)), ((), ())),
                         preferred_element_type=F32)
    t_same = tq_ref[:ROPE_DIM, :]
    t_swap = tq_ref[ROPE_DIM:, :]
    qnn = qnn_ref[...]
    half = AXIS_DIM // 2
    for hd in range(N_HEADS):
        a = qt[hd * QK_HEAD: hd * QK_HEAD + QK_NOPE, :]
        b = qt[hd * QK_HEAD + QK_NOPE: (hd + 1) * QK_HEAD, :]
        ss = jnp.sum(a * a, axis=0, keepdims=True) + jnp.sum(b * b, axis=0, keepdims=True)
        r = lax.rsqrt(ss * (1.0 / QK_HEAD) + EPS)
        b_swap = jnp.concatenate([b[half:2 * half], b[:half], b[3 * half:], b[2 * half:3 * half]], axis=0)
        rot = ((b * t_same + b_swap * t_swap) * r).astype(BF16)
        qt_ref[hd * HEAD_PAD: hd * HEAD_PAD + LANES, :] = (a * r * qnn).astype(BF16)
        qt_ref[hd * HEAD_PAD + LANES: hd * HEAD_PAD + LANES + ROPE_DIM, :] = rot
        qt_ref[hd * HEAD_PAD + LANES + ROPE_DIM: (hd + 1) * HEAD_PAD, :] = rot


def _ctx_lat_specs(n_ctx_tiles, width):
    return [pl.BlockSpec((TM, width), lambda i: (jnp.minimum(i, n_ctx_tiles - 1), 0)),
            pl.BlockSpec((TM, width), lambda i: (jnp.maximum(i - n_ctx_tiles, 0), 0))]


def _mla_in(x_ctx, x_lat, mods, cond_of_tile, table_of_tile, nw, wdq, qln, wuqt, wdkv, kvn, qnn, tq):
    d = x_ctx.shape[1]
    n_ctx_tiles = x_ctx.shape[0] // TM
    t = x_ctx.shape[0] + x_lat.shape[0]
    nt = t // TM
    const = lambda i: (0, 0)
    return pl.pallas_call(
        functools.partial(_mla_in_kernel, n_ctx_tiles=n_ctx_tiles),
        grid=(nt,),
        in_specs=_ctx_lat_specs(n_ctx_tiles, d) + [
            pl.BlockSpec((None, 6, d), lambda i: (cond_of_tile(i), 0, 0)),
            pl.BlockSpec((1, d), const),
            pl.BlockSpec(wdq.shape, const),
            pl.BlockSpec((1, wdq.shape[1]), const),
            pl.BlockSpec(wuqt.shape, const),
            pl.BlockSpec(wdkv.shape, const),
            pl.BlockSpec((1, KV_LORA), const),
            pl.BlockSpec((LANES, TM), const),
            pl.BlockSpec((LANES, TM), lambda i: (0, table_of_tile(i))),
        ],
        out_specs=[
            pl.BlockSpec((N_HEADS * HEAD_PAD, TM), lambda i: (0, i)),
            pl.BlockSpec((TM, KV_LORA), lambda i: (i, 0)),
            pl.BlockSpec((TM, LANES), lambda i: (i, 0)),
        ],
        out_shape=[
            jax.ShapeDtypeStruct((N_HEADS * HEAD_PAD, t), BF16),
            jax.ShapeDtypeStruct((t, KV_LORA), F32),
            jax.ShapeDtypeStruct((t, LANES), F32),
        ],
        compiler_params=_params(1),
        name="mla_in",
    )(x_ctx, x_lat, mods, nw, wdq, qln, wuqt, wdkv, kvn, qnn, tq)


def _kv_expand_kernel(ckv_ref, kr_ref, tk_ref, wuk_ref, wuvt_ref, knn_ref, k_ref, vt_ref):
    cb = ckv_ref[...].astype(BF16)
    kn = jnp.dot(cb, wuk_ref[...], preferred_element_type=F32)
    vt_ref[...] = lax.dot_general(wuvt_ref[...], cb, (((1,), (1,)), ((), ())),
                                  preferred_element_type=F32).astype(BF16)
    kr = kr_ref[...]
    rope_lane = lax.broadcasted_iota(I32, (1, LANES), 1) < ROPE_DIM
    ssr = jnp.sum(jnp.where(rope_lane, kr * kr, 0.0), axis=-1, keepdims=True)
    bk = kr * tk_ref[...]
    rot = jnp.where(rope_lane, bk + pltpu.roll(bk, ROPE_DIM, axis=1), 0.0)
    knn = knn_ref[...]
    for hd in range(N_HEADS):
        a = kn[:, hd * LANES: (hd + 1) * LANES]
        r = lax.rsqrt((jnp.sum(a * a, axis=-1, keepdims=True) + ssr) * (1.0 / QK_HEAD) + EPS)
        k_ref[:, hd * HEAD_PAD: hd * HEAD_PAD + LANES] = (a * r * knn).astype(BF16)
        k_ref[:, hd * HEAD_PAD + LANES: (hd + 1) * HEAD_PAD] = (rot * r).astype(BF16)


def _kv_expand(ckv, kr, tk, table_of_tile, wuk, wuvt, knn):
    t = ckv.shape[0]
    const = lambda i: (0, 0)
    return pl.pallas_call(
        _kv_expand_kernel,
        grid=(t // TM,),
        in_specs=[
            pl.BlockSpec((TM, KV_LORA), lambda i: (i, 0)),
            pl.BlockSpec((TM, LANES), lambda i: (i, 0)),
            pl.BlockSpec((TM, LANES), lambda i: (table_of_tile(i), 0)),
            pl.BlockSpec(wuk.shape, const),
            pl.BlockSpec(wuvt.shape, const),
            pl.BlockSpec((1, LANES), const),
        ],
        out_specs=[
            pl.BlockSpec((TM, N_HEADS * HEAD_PAD), lambda i: (i, 0)),
            pl.BlockSpec((None, N_HEADS * V_HEAD, TM), lambda i: (i, 0, 0)),
        ],
        out_shape=[
            jax.ShapeDtypeStruct((t, N_HEADS * HEAD_PAD), BF16),
            jax.ShapeDtypeStruct((t // TM, N_HEADS * V_HEAD, TM), BF16),
        ],
        compiler_params=_params(1),
        name="kv_expand",
    )(ckv, kr, tk, wuk, wuvt, knn)


def _attn_kernel(bounded_ref, qt_ref, k_ref, vt_ref, o_ref, *, heads, nk, group):
    tq = qt_ref.shape[1]
    chains = [(hh, sub) for hh in range(heads) for sub in range(tq // TM)]

    def load_q(part):
        return [qt_ref[hh * HEAD_PAD:(hh + 1) * HEAD_PAD, sub * TM:(sub + 1) * TM] for hh, sub in part]

    def store(part, ls, accs):
        for (hh, sub), l, acc in zip(part, ls, accs):
            o_ref[sub * TM:(sub + 1) * TM, hh * V_HEAD:(hh + 1) * V_HEAD] = (acc / l).T.astype(BF16)

    @pl.when(bounded_ref[0] == 1)
    def _():
        for c0 in range(0, len(chains), group):
            part = chains[c0:c0 + group]
            qts = load_q(part)
            lps = [jnp.zeros((8, TM), F32) for _ in part]
            accs = [None for _ in part]
            p_prev = None
            for j in range(nk + 1):
                p_cur = []
                if j < nk:
                    for n, ((hh, _), qt) in enumerate(zip(part, qts)):
                        ks = k_ref[j * TM:(j + 1) * TM, hh * HEAD_PAD:(hh + 1) * HEAD_PAD]
                        p = jnp.exp2(jnp.dot(ks, qt, preferred_element_type=F32))
                        lps[n] = lps[n] + jnp.sum(p.reshape(TM // 8, 8, TM), axis=0)
                        p_cur.append(p.astype(BF16))
                if j > 0:
                    for n, (hh, _) in enumerate(part):
                        vt = vt_ref[j - 1, hh * V_HEAD:(hh + 1) * V_HEAD, :]
                        pv = jnp.dot(vt, p_prev[n], preferred_element_type=F32)
                        accs[n] = pv if accs[n] is None else accs[n] + pv
                p_prev = p_cur
            store(part, [jnp.sum(lp, axis=0, keepdims=True) for lp in lps], accs)

    @pl.when(bounded_ref[0] == 0)
    def _():
        for chain in chains:
            hh = chain[0]
            qt, = load_q([chain])

            def body(j, carry, hh=hh, qt=qt):
                m, l, acc = carry
                start = pl.multiple_of(j * TM, TM)
                ks = k_ref[pl.ds(start, TM), hh * HEAD_PAD:(hh + 1) * HEAD_PAD]
                s = jnp.dot(ks, qt, preferred_element_type=F32)
                mn = jnp.maximum(m, jnp.max(s, axis=0, keepdims=True))
                alpha = jnp.exp2(m - mn)
                p = jnp.exp2(s - mn)
                l = alpha * l + jnp.sum(p, axis=0, keepdims=True)
                vt = vt_ref[j, hh * V_HEAD:(hh + 1) * V_HEAD, :]
                acc = alpha * acc + jnp.dot(vt, p.astype(BF16), preferred_element_type=F32)
                return mn, l, acc

            init = (jnp.full((1, TM), -jnp.inf, F32), jnp.zeros((1, TM), F32),
                    jnp.zeros((V_HEAD, TM), F32))
            _, l, acc = lax.fori_loop(0, nk, body, init)
            store([chain], [l], [acc])


def _attention(bounded, qt, k, vt, q_col0, n_batch, q_len, kv_len, tq, heads, group):
    nq = q_len // tq
    hb = N_HEADS // heads
    qb0 = q_col0 // tq
    nk = kv_len // TM
    return pl.pallas_call(
        functools.partial(_attn_kernel, heads=heads, nk=nk, group=group),
        grid_spec=pltpu.PrefetchScalarGridSpec(
            num_scalar_prefetch=1,
            grid=(n_batch, hb, nq),
            in_specs=[
                pl.BlockSpec((heads * HEAD_PAD, tq), lambda b, h, i, f: (h, qb0 + b * nq + i)),
                pl.BlockSpec((kv_len, heads * HEAD_PAD), lambda b, h, i, f: (b, h)),
                pl.BlockSpec((nk, heads * V_HEAD, TM), lambda b, h, i, f: (b, h, 0)),
            ],
            out_specs=pl.BlockSpec((tq, heads * V_HEAD), lambda b, h, i, f: (b * nq + i, h)),
        ),
        out_shape=jax.ShapeDtypeStruct((n_batch * q_len, N_HEADS * V_HEAD), BF16),
        compiler_params=_params(3),
        name="attention",
    )(bounded, qt, k, vt)


def _route(h2, rwh_ref, rwl_ref, rb_ref, route_ref):
    hi = h2.astype(BF16)
    lo = (h2 - hi.astype(F32)).astype(BF16)
    rwh = rwh_ref[...]
    logits = (jnp.dot(hi, rwh, preferred_element_type=F32)
              + jnp.dot(lo, rwh, preferred_element_type=F32)
              + jnp.dot(hi, rwl_ref[...], preferred_element_type=F32))
    lt = logits.T[:N_EXPERTS, :]
    scores = jax.nn.sigmoid(lt)
    sel = scores + rb_ref[...]
    srow = [sel[e:e + 1, :] for e in range(N_EXPERTS)]
    prow = [scores[e:e + 1, :] for e in range(N_EXPERTS)]

    def top2_sum(a, b, c, d):
        hab, lab = jnp.maximum(a, b), jnp.minimum(a, b)
        hcd, lcd = jnp.maximum(c, d), jnp.minimum(c, d)
        return jnp.maximum(hab, hcd) + jnp.maximum(jnp.minimum(hab, hcd), jnp.maximum(lab, lcd))

    gs = [top2_sum(*srow[4 * g:4 * g + 4]) for g in range(N_GROUPS)]
    best = jnp.zeros_like(gs[0], dtype=I32)
    bestv = gs[0]
    for g in range(1, N_GROUPS):
        upd = gs[g] > bestv
        best = jnp.where(upd, g, best)
        bestv = jnp.where(upd, gs[g], bestv)

    def pick(rows, j):
        out = rows[j]
        for g in range(1, N_GROUPS):
            out = jnp.where(best == g, rows[4 * g + j], out)
        return out

    sv = [pick(srow, j) for j in range(EXPERTS_PER_GROUP)]
    pv = [pick(prow, j) for j in range(EXPERTS_PER_GROUP)]
    i1 = jnp.zeros_like(best)
    v1 = sv[0]
    for j in range(1, EXPERTS_PER_GROUP):
        upd = sv[j] > v1
        i1 = jnp.where(upd, j, i1)
        v1 = jnp.where(upd, sv[j], v1)
    neg = jnp.float32(-jnp.inf)
    i2 = jnp.where(i1 == 0, 1, 0).astype(I32)
    v2 = jnp.where(i1 == 0, sv[1], sv[0])
    for j in range(1, EXPERTS_PER_GROUP):
        cand = jnp.where(i1 == j, neg, sv[j])
        upd = cand > v2
        i2 = jnp.where(upd, j, i2)
        v2 = jnp.where(upd, cand, v2)
    ilo = jnp.minimum(i1, i2)
    ihi = jnp.maximum(i1, i2)
    pair = jnp.where(ilo == 0, ihi - 1, jnp.where(ilo == 1, 6 - ihi, N_PAIRS - 1))
    cls = best * N_PAIRS + pair

    def take(vals, idx):
        out = vals[0]
        for j in range(1, EXPERTS_PER_GROUP):
            out = jnp.where(idx == j, vals[j], out)
        return out

    wlo = take(pv, ilo)
    whi = take(pv, ihi)
    den = wlo + whi
    a_is_hi = pair == N_PAIRS - 1
    gate_a = jnp.where(a_is_hi, whi, wlo) / den
    gate_b = jnp.where(a_is_hi, wlo, whi) / den
    route_ref[...] = jnp.concatenate(
        [cls.astype(F32), gate_a, gate_b, jnp.zeros((5, cls.shape[1]), F32)], axis=0)


def _finish_sublayer(x_new, mod_ref, nw2_ref, rwh_ref, rwl_ref, rb_ref, x_out_ref, h2p_ref, route_ref):
    x_out_ref[...] = x_new
    h2 = _rms(x_new) * nw2_ref[...] * (1.0 + mod_ref[4:5, :]) + mod_ref[3:4, :]
    _store_token_major(h2p_ref, h2)
    _route(h2, rwh_ref, rwl_ref, rb_ref, route_ref)


def _mla_out_kernel(attn_c_ref, attn_l_ref, xc_ref, xl_ref, mod_ref, wo_ref, nw2_ref, rwh_ref, rwl_ref,
                    rb_ref, x1_ref, h2p_ref, route_ref, *, n_ctx_tiles):
    is_ctx = pl.program_id(0) < n_ctx_tiles
    attn = jnp.where(is_ctx, attn_c_ref[...], attn_l_ref[...])
    mix = jnp.dot(attn, wo_ref[...], preferred_element_type=F32)
    x1 = jnp.where(is_ctx, xc_ref[...], xl_ref[...]) + mod_ref[2:3, :] * mix
    _finish_sublayer(x1, mod_ref, nw2_ref, rwh_ref, rwl_ref, rb_ref, x1_ref, h2p_ref, route_ref)


def _sublayer_out_specs(t, d):
    nt = t // TM
    specs = [
        pl.BlockSpec((TM, d), lambda i: (i, 0)),
        pl.BlockSpec((TM * ROW_CHUNKS, LANES), lambda i: (i, 0)),
        pl.BlockSpec((None, 8, TM), lambda i: (i, 0, 0)),
    ]
    shapes = [
        jax.ShapeDtypeStruct((t, d), F32),
        jax.ShapeDtypeStruct((t * ROW_CHUNKS, LANES), F32),
        jax.ShapeDtypeStruct((nt, 8, TM), F32),
    ]
    return specs, shapes


def _mla_out(attn_ctx, attn_lat, x_ctx, x_lat, mods, cond_of_tile, wo, nw2, rwh, rwl, rb):
    d = x_ctx.shape[1]
    t = x_ctx.shape[0] + x_lat.shape[0]
    assert d == ROW_CHUNKS * LANES
    const = lambda i: (0, 0)
    n_ctx_tiles = attn_ctx.shape[0] // TM
    out_specs, out_shape = _sublayer_out_specs(t, d)
    return pl.pallas_call(
        functools.partial(_mla_out_kernel, n_ctx_tiles=n_ctx_tiles),
        grid=(t // TM,),
        in_specs=_ctx_lat_specs(n_ctx_tiles, attn_ctx.shape[1]) + _ctx_lat_specs(n_ctx_tiles, d) + [
            pl.BlockSpec((None, 6, d), lambda i: (cond_of_tile(i), 0, 0)),
            pl.BlockSpec(wo.shape, const),
            pl.BlockSpec((1, d), const),
            pl.BlockSpec(rwh.shape, const),
            pl.BlockSpec(rwl.shape, const),
            pl.BlockSpec(rb.shape, const),
        ],
        out_specs=out_specs,
        out_shape=out_shape,
        compiler_params=_params(1),
        name="mla_out",
    )(attn_ctx, attn_lat, x_ctx, x_lat, mods, wo, nw2, rwh, rwl, rb)


def _pool_kernel(x_ref, mo_ref, xp_ref, mop_ref, xn_ref, mon_ref, modp_ref, mod_ref, nw_ref, pw_ref,
                 ps_ref, nw2_ref, rwh_ref, rwl_ref, rb_ref, x2_ref, h2p_ref, route_ref,
                 *, n_ctx_tiles, ctx_seq_tiles, lat_seq_tiles):
    i = pl.program_id(0)
    is_lat = i >= n_ctx_tiles
    seq_tiles = jnp.where(is_lat, lat_seq_tiles, ctx_seq_tiles)
    in_seq = jnp.where(is_lat, i - n_ctx_tiles, i) % seq_tiles
    has_prev = in_seq > 0
    has_next = in_seq < seq_tiles - 1
    g2p = modp_ref[5:6, :]
    nw = nw_ref[...]
    sc1 = 1.0 + mod_ref[1:2, :]
    sh1 = mod_ref[0:1, :]

    def pre(xv, mo):
        xx = xv + g2p * mo
        return xx, _rms(xx) * nw * sc1 + sh1

    xcur, h = pre(x_ref[...], _load_token_major(mo_ref, TM))
    _, hprev = pre(xp_ref[...], _load_token_major(mop_ref, HALO))
    _, hnext = pre(xn_ref[...], _load_token_major(mon_ref, HALO))
    d = h.shape[1]
    gw = d // len(POOL_WINDOWS)
    hb = h.astype(BF16)
    halo = jnp.concatenate(
        [hprev.astype(BF16), hnext.astype(BF16), jnp.zeros((LANES - 2 * HALO, d), BF16)], axis=0)

    t_mid = lax.broadcasted_iota(I32, (TM, TM), 0)
    e_mid = lax.broadcasted_iota(I32, (TM, TM), 1)
    t_hal = lax.broadcasted_iota(I32, (TM, LANES), 0)
    c_hal = lax.broadcasted_iota(I32, (TM, LANES), 1)
    far = jnp.int32(4 * TM)
    pos_prev = jnp.where(has_prev, c_hal - HALO, -far)
    pos_next = jnp.where(has_next, TM + c_hal - HALO, far)
    pos_hal = jnp.where(c_hal < HALO, pos_prev, jnp.where(c_hal < 2 * HALO, pos_next, far))
    t_col = lax.broadcasted_iota(I32, (TM, 1), 0)
    lo_bound = jnp.where(has_prev, -HALO, 0)
    hi_bound = jnp.where(has_next, TM - 1 + HALO, TM - 1)

    ys = []
    for g, w in enumerate(POOL_WINDOWS):
        half = w // 2
        dm = e_mid - t_mid
        band_mid = jnp.where(dm >= -half, jnp.where(dm <= half - 1, 1.0, 0.0), 0.0).astype(BF16)
        dh = pos_hal - t_hal
        band_hal = jnp.where(dh >= -half, jnp.where(dh <= half - 1, 1.0, 0.0), 0.0).astype(BF16)
        sl = slice(g * gw, (g + 1) * gw)
        wsum = (jnp.dot(band_mid, hb[:, sl], preferred_element_type=F32)
                + jnp.dot(band_hal, halo[:, sl], preferred_element_type=F32))
        cnt = (jnp.minimum(t_col + (half - 1), hi_bound)
               - jnp.maximum(t_col - half, lo_bound) + 1).astype(F32)
        pooled = wsum / cnt - h[:, sl]
        ys.append(jnp.dot(pooled.astype(BF16), pw_ref[g], preferred_element_type=F32))
    y = jnp.concatenate(ys, axis=1) * ps_ref[...]
    x2 = xcur + mod_ref[2:3, :] * y
    _finish_sublayer(x2, mod_ref, nw2_ref, rwh_ref, rwl_ref, rb_ref, x2_ref, h2p_ref, route_ref)


def _pool_layer(x1, mo, mods_prev, mods, cond_of_tile, n_ctx_tiles, ctx_seq_tiles, lat_seq_tiles,
                nw, pw, ps, nw2, rwh, rwl, rb):
    t, d = x1.shape
    const = lambda i: (0, 0)
    hb = TM // HALO
    last = t // HALO - 1
    prev_map = lambda i: (jnp.maximum(i * hb - 1, 0), 0)
    next_map = lambda i: (jnp.minimum((i + 1) * hb, last), 0)
    out_specs, out_shape = _sublayer_out_specs(t, d)
    return pl.pallas_call(
        functools.partial(_pool_kernel, n_ctx_tiles=n_ctx_tiles, ctx_seq_tiles=ctx_seq_tiles,
                          lat_seq_tiles=lat_seq_tiles),
        grid=(t // TM,),
        in_specs=[
            pl.BlockSpec((TM, d), lambda i: (i, 0)),
            pl.BlockSpec((TM * ROW_CHUNKS, LANES), lambda i: (i, 0)),
            pl.BlockSpec((HALO, d), prev_map),
            pl.BlockSpec((HALO * ROW_CHUNKS, LANES), prev_map),
            pl.BlockSpec((HALO, d), next_map),
            pl.BlockSpec((HALO * ROW_CHUNKS, LANES), next_map),
            pl.BlockSpec((None, 6, d), lambda i: (cond_of_tile(i), 0, 0)),
            pl.BlockSpec((None, 6, d), lambda i: (cond_of_tile(i), 0, 0)),
            pl.BlockSpec((1, d), const),
            pl.BlockSpec(pw.shape, lambda i: (0, 0, 0)),
            pl.BlockSpec((1, d), const),
            pl.BlockSpec((1, d), const),
            pl.BlockSpec(rwh.shape, const),
            pl.BlockSpec(rwl.shape, const),
            pl.BlockSpec(rb.shape, const),
        ],
        out_specs=out_specs,
        out_shape=out_shape,
        compiler_params=_params(1),
        name="pool_layer",
    )(x1, mo, x1, mo, x1, mo, mods_prev, mods, nw, pw, ps, nw2, rwh, rwl, rb)


def _final_kernel(x_ref, mo_ref, mod_ref, o_ref):
    o_ref[...] = x_ref[...] + mod_ref[5:6, :] * _load_token_major(mo_ref, TM)


def _final(x, mo, mods, cond_of_tile, tile0, n_tiles):
    d = x.shape[1]
    return pl.pallas_call(
        _final_kernel,
        grid=(n_tiles,),
        in_specs=[
            pl.BlockSpec((TM, d), lambda i: (tile0 + i, 0)),
            pl.BlockSpec((TM * ROW_CHUNKS, LANES), lambda i: (tile0 + i, 0)),
            pl.BlockSpec((None, 6, d), lambda i: (cond_of_tile(tile0 + i), 0, 0)),
        ],
        out_specs=pl.BlockSpec((TM, d), lambda i: (i, 0)),
        out_shape=jax.ShapeDtypeStruct((n_tiles * TM, d), F32),
        compiler_params=_params(1),
        name="final_residual",
    )(x, mo, mods)


def _moe_kernel(ea_ref, eb_ref, valid_ref, src_ref, dst_ref, h_hbm, g_ref, wga_ref, wua_ref, wda_ref,
                wgb_ref, wub_ref, wdb_ref, mo_hbm, xbuf, ybuf, sem_in, sem_out):
    i = pl.program_id(0)
    slot = i % 2
    nslot = 1 - slot
    is_valid = valid_ref[i] == 1
    prev_valid = jnp.logical_and(i >= 1, valid_ref[jnp.maximum(i - 1, 0)] == 1)
    prev2_valid = jnp.logical_and(i >= 2, valid_ref[jnp.maximum(i - 2, 0)] == 1)

    def gather_copy(tok, r, slot):
        return pltpu.make_async_copy(
            h_hbm.at[pl.ds(pl.multiple_of(tok * ROW_CHUNKS, ROW_CHUNKS), ROW_CHUNKS)],
            xbuf.at[slot, pl.ds(pl.multiple_of(r * BUF_PITCH, 8), ROW_CHUNKS)], sem_in.at[slot])

    def scatter_copy(tok, r, slot):
        return pltpu.make_async_copy(
            ybuf.at[slot, pl.ds(pl.multiple_of(r * ROW_CHUNKS, ROW_CHUNKS), ROW_CHUNKS)],
            mo_hbm.at[pl.ds(pl.multiple_of(tok * ROW_CHUNKS, ROW_CHUNKS), ROW_CHUNKS)], sem_out.at[slot])

    def for_rows(fn):
        def body(r, carry):
            fn(r)
            return carry
        lax.fori_loop(0, TM, body, 0, unroll=8)

    def gather_start(tile, slot):
        for_rows(lambda r: gather_copy(src_ref[tile * TM + r], r, slot).start())

    def gather_wait(slot):
        for_rows(lambda r: gather_copy(0, 0, slot).wait())

    def scatter_start(tile, slot):
        for_rows(lambda r: scatter_copy(dst_ref[tile * TM + r], r, slot).start())

    def scatter_wait(slot):
        for_rows(lambda r: scatter_copy(0, 0, slot).wait())

    @pl.when(i == 0)
    def _():
        gather_start(0, 0)

    @pl.when(prev2_valid)
    def _():
        scatter_wait(slot)

    @pl.when(is_valid)
    def _():
        gather_wait(slot)
        x = _load_token_major(xbuf.at[slot], TM, BUF_PITCH).astype(BF16)
        ff = wga_ref.shape[1]
        d = wda_ref.shape[1]
        n_dots = 2 * (2 * (ff // MXU_N) + d // MXU_N)
        issued = [0]

        def request_next_rows(k):
            upto = TM if k == n_dots else (TM * k) // n_dots
            for r in range(issued[0], upto):
                gather_copy(src_ref[(i + 1) * TM + r], r, nslot).start()
            issued[0] = upto

        done = [0]

        def dot_piece(lhs, w_ref, n0):
            out = jnp.dot(lhs, w_ref[:, n0:n0 + MXU_N], preferred_element_type=F32)
            done[0] += 1
            request_next_rows(done[0])
            return out

        def ffn(wg_ref, wu_ref, wd_ref):
            g = jnp.concatenate([dot_piece(x, wg_ref, n0) for n0 in range(0, ff, MXU_N)], axis=1)
            u = jnp.concatenate([dot_piece(x, wu_ref, n0) for n0 in range(0, ff, MXU_N)], axis=1)
            a = ((g * jax.nn.sigmoid(g)) * u).astype(BF16)
            return [dot_piece(a, wd_ref, n0) for n0 in range(0, d, MXU_N)]

        gates = g_ref[...]
        ya = ffn(wga_ref, wua_ref, wda_ref)
        yb = ffn(wgb_ref, wub_ref, wdb_ref)
        y = jnp.concatenate([gates[:, 0:1] * pa + gates[:, 1:2] * pb for pa, pb in zip(ya, yb)], axis=1)
        _store_token_major(ybuf.at[slot], y)
        scatter_start(i, slot)

    @pl.when(jnp.logical_not(is_valid))
    def _():
        @pl.when(prev_valid)
        def _():
            gather_wait(slot)

        ybuf[slot] = jnp.zeros(ybuf.shape[1:], F32)
        row0 = pl.multiple_of(dst_ref[i * TM] * ROW_CHUNKS, ROW_CHUNKS)
        fill = pltpu.make_async_copy(ybuf.at[slot], mo_hbm.at[pl.ds(row0, TM * ROW_CHUNKS)],
                                     sem_out.at[slot])
        fill.start()
        fill.wait()


def _moe_sorted(h, src_tok, dst_tok, gates, tile_ea, tile_eb, tile_valid, layer, wg, wu, wd):
    n_rows = src_tok.shape[0]
    d, ff = wg.shape[-2:]
    wa = lambda i, ea, eb, va, src, dst: (layer, ea[i], 0, 0)
    wb = lambda i, ea, eb, va, src, dst: (layer, eb[i], 0, 0)
    return pl.pallas_call(
        _moe_kernel,
        grid_spec=pltpu.PrefetchScalarGridSpec(
            num_scalar_prefetch=5,
            grid=(n_rows // TM,),
            in_specs=[
                pl.BlockSpec(memory_space=pl.ANY),
                pl.BlockSpec((TM, 2), lambda i, ea, eb, va, src, dst: (i, 0)),
                pl.BlockSpec((None, None, d, ff), wa),
                pl.BlockSpec((None, None, d, ff), wa),
                pl.BlockSpec((None, None, ff, d), wa),
                pl.BlockSpec((None, None, d, ff), wb),
                pl.BlockSpec((None, None, d, ff), wb),
                pl.BlockSpec((None, None, ff, d), wb),
            ],
            out_specs=pl.BlockSpec(memory_space=pl.ANY),
            scratch_shapes=[
                pltpu.VMEM((2, TM * BUF_PITCH, LANES), F32),
                pltpu.VMEM((2, TM * ROW_CHUNKS, LANES), F32),
                pltpu.SemaphoreType.DMA((2,)),
                pltpu.SemaphoreType.DMA((2,)),
            ],
        ),
        out_shape=jax.ShapeDtypeStruct((n_rows * ROW_CHUNKS, LANES), F32),
        compiler_params=_params(1),
        name="moe_experts",
    )(tile_ea, tile_eb, tile_valid, src_tok, dst_tok, h, gates, wg, wu, wd, wg, wu, wd)


def _moe_layer(h2p, route, layer, wg, wu, wd):
    t = h2p.shape[0] // ROW_CHUNKS
    n_tiles = t // TM + N_CLASSES + 2
    cls = route[:, 0, :].reshape(t).astype(I32)
    g_a = route[:, 1, :].reshape(t)
    g_b = route[:, 2, :].reshape(t)
    onehot = (cls[:, None] == jnp.arange(N_CLASSES, dtype=I32)[None, :]).astype(I32)
    csum = jnp.cumsum(onehot, axis=0)
    counts = csum[-1]
    rank = jnp.take_along_axis(csum, cls[:, None], axis=1)[:, 0] - 1
    tiles_c = (counts + TM - 1) // TM
    tile_end = jnp.cumsum(tiles_c)
    tile_start = tile_end - tiles_c
    pos = tile_start[cls] * TM + rank
    n_rows = n_tiles * TM
    row_tok = jnp.full((n_rows,), -1, I32).at[pos].set(jnp.arange(t, dtype=I32), unique_indices=True)
    is_pad = row_tok < 0
    src_tok = jnp.where(is_pad, jnp.arange(n_rows, dtype=I32) % t, row_tok)
    dst_tok = jnp.where(is_pad, t + jnp.cumsum(is_pad.astype(I32)) - 1, row_tok)
    tile_ids = jnp.arange(n_tiles, dtype=I32)
    total = tile_end[-1]
    tile_valid = (tile_ids < total).astype(I32)
    tile_cls = jnp.sum((tile_end[None, :] <= jnp.minimum(tile_ids, total - 1)[:, None]).astype(I32), axis=1)
    pair = tile_cls % N_PAIRS
    group = tile_cls // N_PAIRS
    tile_ea = group * EXPERTS_PER_GROUP + jnp.asarray(PAIR_A, I32)[pair]
    tile_eb = group * EXPERTS_PER_GROUP + jnp.asarray(PAIR_B, I32)[pair]
    gates = jnp.stack([g_a[src_tok], g_b[src_tok]], axis=1)
    return _moe_sorted(h2p, src_tok, dst_tok, gates, tile_ea, tile_eb, tile_valid, layer, wg, wu, wd)


def _swap_rope_halves(w):
    shp = w.shape
    return w.reshape(shp[:-1] + (2, 2, AXIS_DIM // 2))[..., ::-1, :].reshape(shp)


def _rope_tables(length, norm_rope, scale):
    rows = length // GRID_W
    row = jnp.repeat(jnp.arange(rows, dtype=F32), GRID_W)
    col = jnp.tile(jnp.arange(GRID_W, dtype=F32), rows)
    inv = jnp.power(ROPE_THETA, -jnp.arange(0, AXIS_DIM, 2, dtype=F32) / AXIS_DIM)
    ang = jnp.stack([row[:, None] * inv, col[:, None] * inv], axis=1)
    cos, sin = jnp.cos(ang), jnp.sin(ang)
    c_full = jnp.stack([cos, cos], axis=2).reshape(length, ROPE_DIM)
    s_full = jnp.stack([-sin, sin], axis=2).reshape(length, ROPE_DIM)
    lat = jnp.concatenate([norm_rope * c_full, _swap_rope_halves(norm_rope) * s_full], axis=1)
    ctx = jnp.concatenate([norm_rope, jnp.zeros((ROPE_DIM,), F32)])
    ctx = jnp.broadcast_to(ctx[None, :], (TM, 2 * ROPE_DIM))
    return jnp.concatenate([ctx, lat], axis=0) * scale


def kernel(x_prompt, x_sample, cache_ckv, cache_krope, c, c_ctx, ada_w, ada_b, norm_mix_w, norm_ffn_w,
           mla_w_dq, mla_q_lora_norm, mla_w_uq, mla_w_dkv, mla_kv_lora_norm, mla_w_uk, mla_w_uv,
           mla_q_norm, mla_k_norm, mla_w_o, pool_w, pool_scale, router_w, router_bias,
           moe_w_gate, moe_w_up, moe_w_down):
    batch, seq, d = x_prompt.shape
    dec_batch, dec_seq, _ = x_sample.shape
    past = cache_ckv.shape[2]
    depth = ada_w.shape[0]
    assert seq == TM and past == TM and dec_seq % (2 * TM) == 0
    assert dec_batch + 1 <= 8 and depth == 2
    t_ctx = batch * seq
    t_lat = dec_batch * dec_seq
    n_ctx_tiles = t_ctx // TM
    lat_seq_tiles = dec_seq // TM
    ctx_seq_tiles = seq // TM

    def cond_of_tile(i):
        return jnp.where(i < n_ctx_tiles, 0, 1 + (i - n_ctx_tiles) // lat_seq_tiles)

    def table_of_tile(i):
        return jnp.where(i < n_ctx_tiles, 0, 1 + (i - n_ctx_tiles) % lat_seq_tiles)

    x_ctx = x_prompt.reshape(t_ctx, d)
    x_lat = x_sample.reshape(t_lat, d)
    conds = jnp.concatenate([c_ctx[None, :], c, jnp.zeros((7 - dec_batch, d), F32)], axis=0)
    mods = _adaln(conds, ada_w, ada_b).reshape(depth, 8, 6, d)

    rw = jnp.pad(router_w, ((0, 0), (0, LANES - N_EXPERTS)))
    rwh = rw.astype(BF16)
    rwl = (rw - rwh.astype(F32)).astype(BF16)
    rb = jnp.broadcast_to(router_bias.astype(F32)[:, None], (N_EXPERTS, TM))
    wg = moe_w_gate.astype(BF16)
    wu = moe_w_up.astype(BF16)
    wd = moe_w_down.astype(BF16)

    j = 0
    q_norm, k_norm = mla_q_norm[j], mla_k_norm[j]
    w_uqt = mla_w_uq[j].T.astype(BF16)
    w_dkv = mla_w_dkv[j]
    w_dkv = jnp.concatenate([w_dkv, _swap_rope_halves(w_dkv[:, KV_LORA:])], axis=1).astype(BF16)
    q_scale = ATTN_SCALE * math.log2(math.e)
    tq = _rope_tables(dec_seq, q_norm[QK_NOPE:], q_scale).T
    tk = _rope_tables(dec_seq, k_norm[QK_NOPE:], 1.0)
    qnn = jnp.broadcast_to((q_norm[:QK_NOPE] * q_scale)[:, None], (QK_NOPE, TM))
    knn = k_norm[:QK_NOPE][None, :]

    qt, ckv, kr = _mla_in(x_ctx, x_lat, mods[0], cond_of_tile, table_of_tile, norm_mix_w[0][None, :],
                          mla_w_dq[j].astype(BF16), mla_q_lora_norm[j][None, :], w_uqt, w_dkv,
                          mla_kv_lora_norm[j][None, :], qnn, tq)
    state_ckv = ckv[:t_ctx].reshape(batch, 1, seq, KV_LORA)
    state_krope = kr[:t_ctx, :ROPE_DIM].reshape(batch, 1, seq, ROPE_DIM)

    w_uk = mla_w_uk[j].astype(BF16)
    w_uvt = mla_w_uv[j].T.astype(BF16)
    k_ctx, vt_ctx = _kv_expand(ckv[:t_ctx], kr[:t_ctx], tk, lambda i: 0, w_uk, w_uvt, knn)
    kv_len = past + dec_seq
    ckv_lat = jnp.concatenate(
        [cache_ckv[:, j], ckv[t_ctx:].reshape(dec_batch, dec_seq, KV_LORA)], axis=1)
    kr_cache = jnp.pad(cache_krope[:, j], ((0, 0), (0, 0), (0, LANES - ROPE_DIM)))
    kr_lat = jnp.concatenate([kr_cache, kr[t_ctx:].reshape(dec_batch, dec_seq, LANES)], axis=1)
    kv_seq_tiles = kv_len // TM
    k_lat, vt_lat = _kv_expand(ckv_lat.reshape(-1, KV_LORA), kr_lat.reshape(-1, LANES), tk,
                               lambda i: i % kv_seq_tiles, w_uk, w_uvt, knn)

    logit_bound = q_scale * QK_HEAD * jnp.max(jnp.abs(q_norm)) * jnp.max(jnp.abs(k_norm))
    bounded = (logit_bound <= MAX_SAFE_LOGIT).astype(I32).reshape(1)
    attn_ctx = _attention(bounded, qt, k_ctx, vt_ctx, 0, batch, seq, seq, tq=seq, heads=N_HEADS, group=4)
    attn_lat = _attention(bounded, qt, k_lat, vt_lat, t_ctx, dec_batch, dec_seq, kv_len, tq=2 * TM,
                          heads=2, group=4)

    x1, h2p, route = _mla_out(attn_ctx, attn_lat, x_ctx, x_lat, mods[0], cond_of_tile,
                              mla_w_o[j].astype(BF16), norm_ffn_w[0][None, :], rwh, rwl, rb)
    mo = _moe_layer(h2p, route, 0, wg, wu, wd)

    x2, h2p, route = _pool_layer(x1, mo, mods[0], mods[1], cond_of_tile, n_ctx_tiles, ctx_seq_tiles,
                                 lat_seq_tiles, norm_mix_w[1][None, :], pool_w[0].astype(BF16),
                                 pool_scale[0][None, :], norm_ffn_w[1][None, :], rwh, rwl, rb)
    mo = _moe_layer(h2p, route, 1, wg, wu, wd)
    y_ctx = _final(x2, mo, mods[1], cond_of_tile, 0, n_ctx_tiles)
    y_lat = _final(x2, mo, mods[1], cond_of_tile, n_ctx_tiles, t_lat // TM)

    return (y_ctx.reshape(batch, seq, d), y_lat.reshape(dec_batch, dec_seq, d), state_ckv, state_krope)
```

```python
import functools
import math

import jax
import jax.numpy as jnp
from jax import lax
from jax.experimental import pallas as pl
from jax.experimental.pallas import tpu as pltpu

F32 = jnp.float32
BF16 = jnp.bfloat16
I32 = jnp.int32

GRID_W = 64
N_HEADS = 16
QK_NOPE = 128
ROPE_DIM = 64
QK_HEAD = QK_NOPE + ROPE_DIM
V_HEAD = 128
KV_LORA = 256
ROPE_THETA = 10000.0
AXIS_DIM = ROPE_DIM // 2
ATTN_SCALE = QK_HEAD ** -0.5
POOL_WINDOWS = (2, 4, 8, 16)
N_EXPERTS = 16
N_GROUPS = 4
EXPERTS_PER_GROUP = N_EXPERTS // N_GROUPS
EPS = 1e-6

LANES = 128
SMEM_WORDS_TILE = 1024
HEAD_PAD = 2 * LANES
TM = 256
HALO = 16
N_PAIRS = 6
N_CLASSES = N_GROUPS * N_PAIRS
PAIR_A = (0, 0, 0, 1, 1, 3)
PAIR_B = (1, 2, 3, 3, 2, 2)
ROW_CHUNKS = 16
BUF_PITCH = 24
VMEM_LIMIT = 52 * 1024 * 1024
MAX_SAFE_LOGIT = 64.0


def _params(n_axes):
    return pltpu.CompilerParams(
        dimension_semantics=("arbitrary",) * n_axes, vmem_limit_bytes=VMEM_LIMIT)


def _rms(x):
    return x * lax.rsqrt(jnp.mean(x * x, axis=-1, keepdims=True) + EPS)


def _store_token_major(ref, x, pitch=ROW_CHUNKS):
    rows = x.shape[0]
    for c in range(ROW_CHUNKS):
        ref[pl.ds(c, rows, stride=pitch), :] = x[:, c * LANES:(c + 1) * LANES]


def _load_token_major(ref, rows, pitch=ROW_CHUNKS):
    return jnp.concatenate(
        [ref[pl.ds(c, rows, stride=pitch), :] for c in range(ROW_CHUNKS)], axis=1)


def _ada_kernel(cond_ref, w_ref, b_ref, o_ref):
    c = cond_ref[...]
    s = c * jax.nn.sigmoid(c)
    o_ref[...] = jnp.dot(s, w_ref[...], preferred_element_type=F32,
                         precision=lax.Precision.HIGHEST) + b_ref[...]


def _adaln(conds, ada_w, ada_b):
    depth, d, n = ada_w.shape
    tn = 1024
    return pl.pallas_call(
        _ada_kernel,
        grid=(depth, n // tn),
        in_specs=[
            pl.BlockSpec((8, d), lambda l, j: (0, 0)),
            pl.BlockSpec((None, d, tn), lambda l, j: (l, 0, j)),
            pl.BlockSpec((None, 1, tn), lambda l, j: (l, 0, j)),
        ],
        out_specs=pl.BlockSpec((None, 8, tn), lambda l, j: (l, 0, j)),
        out_shape=jax.ShapeDtypeStruct((depth, 8, n), F32),
        compiler_params=_params(2),
        name="adaln",
    )(conds, ada_w, ada_b.reshape(depth, 1, n))


def _mla_in_kernel(xc_ref, xl_ref, mod_ref, nw_ref, wdq_ref, qln_ref, wuqt_ref, wdkv_ref, kvn_ref,
                   qnn_ref, tq_ref, qt_ref, ckv_ref, kr_ref, *, n_ctx_tiles):
    x = jnp.where(pl.program_id(0) < n_ctx_tiles, xc_ref[...], xl_ref[...])
    h = _rms(x) * (nw_ref[...] * (1.0 + mod_ref[1:2, :])) + mod_ref[0:1, :]
    hb = h.astype(BF16)
    kv = jnp.dot(hb, wdkv_ref[...], preferred_element_type=F32)
    ckv_ref[...] = _rms(kv[:, :KV_LORA]) * kvn_ref[...]
    kr_ref[...] = kv[:, KV_LORA:]
    cq = jnp.dot(hb, wdq_ref[...], preferred_element_type=F32)
    cqn = (_rms(cq) * qln_ref[...]).astype(BF16)
    qt = lax.dot_general(wuqt_ref[...], cqn, (((1,), (1,)), ((), ())),
                         preferred_element_type=F32)
    t_same = tq_ref[:ROPE_DIM, :]
    t_swap = tq_ref[ROPE_DIM:, :]
    qnn = qnn_ref[...]
    half = AXIS_DIM // 2
    for hd in range(N_HEADS):
        a = qt[hd * QK_HEAD: hd * QK_HEAD + QK_NOPE, :]
        b = qt[hd * QK_HEAD + QK_NOPE: (hd + 1) * QK_HEAD, :]
        ss = jnp.sum(a * a, axis=0, keepdims=True) + jnp.sum(b * b, axis=0, keepdims=True)
        r = lax.rsqrt(ss * (1.0 / QK_HEAD) + EPS)
        b_swap = jnp.concatenate([b[half:2 * half], b[:half], b[3 * half:], b[2 * half:3 * half]], axis=0)
        rot = ((b * t_same + b_swap * t_swap) * r).astype(BF16)
        qt_ref[hd * HEAD_PAD: hd * HEAD_PAD + LANES, :] = (a * r * qnn).astype(BF16)
        qt_ref[hd * HEAD_PAD + LANES: hd * HEAD_PAD + LANES + ROPE_DIM, :] = rot
        qt_ref[hd * HEAD_PAD + LANES + ROPE_DIM: (hd + 1) * HEAD_PAD, :] = rot


def _ctx_lat_specs(n_ctx_tiles, width):
    return [pl.BlockSpec((TM, width), lambda i: (jnp.minimum(i, n_ctx_tiles - 1), 0)),
            pl.BlockSpec((TM, width), lambda i: (jnp.maximum(i - n_ctx_tiles, 0), 0))]


def _mla_in(x_ctx, x_lat, mods, cond_of_tile, table_of_tile, nw, wdq, qln, wuqt, wdkv, kvn, qnn, tq):
    d = x_ctx.shape[1]
    n_ctx_tiles = x_ctx.shape[0] // TM
    t = x_ctx.shape[0] + x_lat.shape[0]
    nt = t // TM
    const = lambda i: (0, 0)
    return pl.pallas_call(
        functools.partial(_mla_in_kernel, n_ctx_tiles=n_ctx_tiles),
        grid=(nt,),
        in_specs=_ctx_lat_specs(n_ctx_tiles, d) + [
            pl.BlockSpec((None, 6, d), lambda i: (cond_of_tile(i), 0, 0)),
            pl.BlockSpec((1, d), const),
            pl.BlockSpec(wdq.shape, const),
            pl.BlockSpec((1, wdq.shape[1]), const),
            pl.BlockSpec(wuqt.shape, const),
            pl.BlockSpec(wdkv.shape, const),
            pl.BlockSpec((1, KV_LORA), const),
            pl.BlockSpec((LANES, TM), const),
            pl.BlockSpec((LANES, TM), lambda i: (0, table_of_tile(i))),
        ],
        out_specs=[
            pl.BlockSpec((N_HEADS * HEAD_PAD, TM), lambda i: (0, i)),
            pl.BlockSpec((TM, KV_LORA), lambda i: (i, 0)),
            pl.BlockSpec((TM, LANES), lambda i: (i, 0)),
        ],
        out_shape=[
            jax.ShapeDtypeStruct((N_HEADS * HEAD_PAD, t), BF16),
            jax.ShapeDtypeStruct((t, KV_LORA), F32),
            jax.ShapeDtypeStruct((t, LANES), F32),
        ],
        compiler_params=_params(1),
        name="mla_in",
    )(x_ctx, x_lat, mods, nw, wdq, qln, wuqt, wdkv, kvn, qnn, tq)


def _kv_expand_kernel(ckv_ref, kr_ref, tk_ref, wuk_ref, wuvt_ref, knn_ref, k_ref, vt_ref):
    cb = ckv_ref[...].astype(BF16)
    kn = jnp.dot(cb, wuk_ref[...], preferred_element_type=F32)
    vt_ref[...] = lax.dot_general(wuvt_ref[...], cb, (((1,), (1,)), ((), ())),
                                  preferred_element_type=F32).astype(BF16)
    kr = kr_ref[...]
    rope_lane = lax.broadcasted_iota(I32, (1, LANES), 1) < ROPE_DIM
    ssr = jnp.sum(jnp.where(rope_lane, kr * kr, 0.0), axis=-1, keepdims=True)
    bk = kr * tk_ref[...]
    rot = jnp.where(rope_lane, bk + pltpu.roll(bk, ROPE_DIM, axis=1), 0.0)
    knn = knn_ref[...]
    for hd in range(N_HEADS):
        a = kn[:, hd * LANES: (hd + 1) * LANES]
        r = lax.rsqrt((jnp.sum(a * a, axis=-1, keepdims=True) + ssr) * (1.0 / QK_HEAD) + EPS)
        k_ref[:, hd * HEAD_PAD: hd * HEAD_PAD + LANES] = (a * r * knn).astype(BF16)
        k_ref[:, hd * HEAD_PAD + LANES: (hd + 1) * HEAD_PAD] = (rot * r).astype(BF16)


def _kv_expand(ckv, kr, tk, table_of_tile, wuk, wuvt, knn):
    t = ckv.shape[0]
    const = lambda i: (0, 0)
    return pl.pallas_call(
        _kv_expand_kernel,
        grid=(t // TM,),
        in_specs=[
            pl.BlockSpec((TM, KV_LORA), lambda i: (i, 0)),
            pl.BlockSpec((TM, LANES), lambda i: (i, 0)),
            pl.BlockSpec((TM, LANES), lambda i: (table_of_tile(i), 0)),
            pl.BlockSpec(wuk.shape, const),
            pl.BlockSpec(wuvt.shape, const),
            pl.BlockSpec((1, LANES), const),
        ],
        out_specs=[
            pl.BlockSpec((TM, N_HEADS * HEAD_PAD), lambda i: (i, 0)),
            pl.BlockSpec((None, N_HEADS * V_HEAD, TM), lambda i: (i, 0, 0)),
        ],
        out_shape=[
            jax.ShapeDtypeStruct((t, N_HEADS * HEAD_PAD), BF16),
            jax.ShapeDtypeStruct((t // TM, N_HEADS * V_HEAD, TM), BF16),
        ],
        compiler_params=_params(1),
        name="kv_expand",
    )(ckv, kr, tk, wuk, wuvt, knn)


def _attn_kernel(bounded_ref, qt_ref, k_ref, vt_ref, o_ref, *, heads, nk, group):
    tq = qt_ref.shape[1]
    chains = [(hh, sub) for hh in range(heads) for sub in range(tq // TM)]

    def load_q(part):
        return [qt_ref[hh * HEAD_PAD:(hh + 1) * HEAD_PAD, sub * TM:(sub + 1) * TM] for hh, sub in part]

    def store(part, ls, accs):
        for (hh, sub), l, acc in zip(part, ls, accs):
            o_ref[sub * TM:(sub + 1) * TM, hh * V_HEAD:(hh + 1) * V_HEAD] = (acc / l).T.astype(BF16)

    @pl.when(bounded_ref[0] == 1)
    def _():
        for c0 in range(0, len(chains), group):
            part = chains[c0:c0 + group]
            qts = load_q(part)
            lps = [jnp.zeros((8, TM), F32) for _ in part]
            accs = [None for _ in part]
            p_prev = None
            for j in range(nk + 1):
                p_cur = []
                if j < nk:
                    for n, ((hh, _), qt) in enumerate(zip(part, qts)):
                        ks = k_ref[j * TM:(j + 1) * TM, hh * HEAD_PAD:(hh + 1) * HEAD_PAD]
                        p = jnp.exp2(jnp.dot(ks, qt, preferred_element_type=F32))
                        lps[n] = lps[n] + jnp.sum(p.reshape(TM // 8, 8, TM), axis=0)
                        p_cur.append(p.astype(BF16))
                if j > 0:
                    for n, (hh, _) in enumerate(part):
                        vt = vt_ref[j - 1, hh * V_HEAD:(hh + 1) * V_HEAD, :]
                        pv = jnp.dot(vt, p_prev[n], preferred_element_type=F32)
                        accs[n] = pv if accs[n] is None else accs[n] + pv
                p_prev = p_cur
            store(part, [jnp.sum(lp, axis=0, keepdims=True) for lp in lps], accs)

    @pl.when(bounded_ref[0] == 0)
    def _():
        for chain in chains:
            hh = chain[0]
            qt, = load_q([chain])

            def body(j, carry, hh=hh, qt=qt):
                m, l, acc = carry
                start = pl.multiple_of(j * TM, TM)
                ks = k_ref[pl.ds(start, TM), hh * HEAD_PAD:(hh + 1) * HEAD_PAD]
                s = jnp.dot(ks, qt, preferred_element_type=F32)
                mn = jnp.maximum(m, jnp.max(s, axis=0, keepdims=True))
                alpha = jnp.exp2(m - mn)
                p = jnp.exp2(s - mn)
                l = alpha * l + jnp.sum(p, axis=0, keepdims=True)
                vt = vt_ref[j, hh * V_HEAD:(hh + 1) * V_HEAD, :]
                acc = alpha * acc + jnp.dot(vt, p.astype(BF16), preferred_element_type=F32)
                return mn, l, acc

            init = (jnp.full((1, TM), -jnp.inf, F32), jnp.zeros((1, TM), F32),
                    jnp.zeros((V_HEAD, TM), F32))
            _, l, acc = lax.fori_loop(0, nk, body, init)
            store([chain], [l], [acc])


def _attention(bounded, qt, k, vt, q_col0, n_batch, q_len, kv_len, tq, heads, group):
    nq = q_len // tq
    hb = N_HEADS // heads
    qb0 = q_col0 // tq
    nk = kv_len // TM
    return pl.pallas_call(
        functools.partial(_attn_kernel, heads=heads, nk=nk, group=group),
        grid_spec=pltpu.PrefetchScalarGridSpec(
            num_scalar_prefetch=1,
            grid=(n_batch, hb, nq),
            in_specs=[
                pl.BlockSpec((heads * HEAD_PAD, tq), lambda b, h, i, f: (h, qb0 + b * nq + i)),
                pl.BlockSpec((kv_len, heads * HEAD_PAD), lambda b, h, i, f: (b, h)),
                pl.BlockSpec((nk, heads * V_HEAD, TM), lambda b, h, i, f: (b, h, 0)),
            ],
            out_specs=pl.BlockSpec((tq, heads * V_HEAD), lambda b, h, i, f: (b * nq + i, h)),
        ),
        out_shape=jax.ShapeDtypeStruct((n_batch * q_len, N_HEADS * V_HEAD), BF16),
        compiler_params=_params(3),
        name="attention",
    )(bounded, qt, k, vt)


def _route(h2, rwh_ref, rwl_ref, rb_ref, route_ref):
    hi = h2.astype(BF16)
    lo = (h2 - hi.astype(F32)).astype(BF16)
    rwh = rwh_ref[...]
    logits = (jnp.dot(hi, rwh, preferred_element_type=F32)
              + jnp.dot(lo, rwh, preferred_element_type=F32)
              + jnp.dot(hi, rwl_ref[...], preferred_element_type=F32))
    lt = logits.T[:N_EXPERTS, :]
    scores = jax.nn.sigmoid(lt)
    sel = scores + rb_ref[...]
    srow = [sel[e:e + 1, :] for e in range(N_EXPERTS)]
    prow = [scores[e:e + 1, :] for e in range(N_EXPERTS)]

    def top2_sum(a, b, c, d):
        hab, lab = jnp.maximum(a, b), jnp.minimum(a, b)
        hcd, lcd = jnp.maximum(c, d), jnp.minimum(c, d)
        return jnp.maximum(hab, hcd) + jnp.maximum(jnp.minimum(hab, hcd), jnp.maximum(lab, lcd))

    gs = [top2_sum(*srow[4 * g:4 * g + 4]) for g in range(N_GROUPS)]
    best = jnp.zeros_like(gs[0], dtype=I32)
    bestv = gs[0]
    for g in range(1, N_GROUPS):
        upd = gs[g] > bestv
        best = jnp.where(upd, g, best)
        bestv = jnp.where(upd, gs[g], bestv)

    def pick(rows, j):
        out = rows[j]
        for g in range(1, N_GROUPS):
            out = jnp.where(best == g, rows[4 * g + j], out)
        return out

    sv = [pick(srow, j) for j in range(EXPERTS_PER_GROUP)]
    pv = [pick(prow, j) for j in range(EXPERTS_PER_GROUP)]
    i1 = jnp.zeros_like(best)
    v1 = sv[0]
    for j in range(1, EXPERTS_PER_GROUP):
        upd = sv[j] > v1
        i1 = jnp.where(upd, j, i1)
        v1 = jnp.where(upd, sv[j], v1)
    neg = jnp.float32(-jnp.inf)
    i2 = jnp.where(i1 == 0, 1, 0).astype(I32)
    v2 = jnp.where(i1 == 0, sv[1], sv[0])
    for j in range(1, EXPERTS_PER_GROUP):
        cand = jnp.where(i1 == j, neg, sv[j])
        upd = cand > v2
        i2 = jnp.where(upd, j, i2)
        v2 = jnp.where(upd, cand, v2)
    ilo = jnp.minimum(i1, i2)
    ihi = jnp.maximum(i1, i2)
    pair = jnp.where(ilo == 0, ihi - 1, jnp.where(ilo == 1, 6 - ihi, N_PAIRS - 1))
    cls = best * N_PAIRS + pair

    def take(vals, idx):
        out = vals[0]
        for j in range(1, EXPERTS_PER_GROUP):
            out = jnp.where(idx == j, vals[j], out)
        return out

    wlo = take(pv, ilo)
    whi = take(pv, ihi)
    den = wlo + whi
    a_is_hi = pair == N_PAIRS - 1
    gate_a = jnp.where(a_is_hi, whi, wlo) / den
    gate_b = jnp.where(a_is_hi, wlo, whi) / den
    route_ref[...] = jnp.concatenate(
        [cls.astype(F32), gate_a, gate_b, jnp.zeros((5, cls.shape[1]), F32)], axis=0)


def _finish_sublayer(x_new, mod_ref, nw2_ref, rwh_ref, rwl_ref, rb_ref, x_out_ref, h2p_ref, route_ref):
    x_out_ref[...] = x_new
    h2 = _rms(x_new) * (nw2_ref[...] * (1.0 + mod_ref[4:5, :])) + mod_ref[3:4, :]
    _store_token_major(h2p_ref, h2)
    _route(h2, rwh_ref, rwl_ref, rb_ref, route_ref)


def _mla_out_kernel(attn_c_ref, attn_l_ref, xc_ref, xl_ref, mod_ref, wo_ref, nw2_ref, rwh_ref, rwl_ref,
                    rb_ref, x1_ref, h2p_ref, route_ref, *, n_ctx_tiles):
    is_ctx = pl.program_id(0) < n_ctx_tiles
    attn = jnp.where(is_ctx, attn_c_ref[...], attn_l_ref[...])
    mix = jnp.dot(attn, wo_ref[...], preferred_element_type=F32)
    x1 = jnp.where(is_ctx, xc_ref[...], xl_ref[...]) + mod_ref[2:3, :] * mix
    _finish_sublayer(x1, mod_ref, nw2_ref, rwh_ref, rwl_ref, rb_ref, x1_ref, h2p_ref, route_ref)


def _sublayer_out_specs(t, d):
    nt = t // TM
    specs = [
        pl.BlockSpec((TM, d), lambda i: (i, 0)),
        pl.BlockSpec((TM * ROW_CHUNKS, LANES), lambda i: (i, 0)),
        pl.BlockSpec((None, 8, TM), lambda i: (i, 0, 0)),
    ]
    shapes = [
        jax.ShapeDtypeStruct((t, d), F32),
        jax.ShapeDtypeStruct((t * ROW_CHUNKS, LANES), F32),
        jax.ShapeDtypeStruct((nt, 8, TM), F32),
    ]
    return specs, shapes


def _mla_out(attn_ctx, attn_lat, x_ctx, x_lat, mods, cond_of_tile, wo, nw2, rwh, rwl, rb):
    d = x_ctx.shape[1]
    t = x_ctx.shape[0] + x_lat.shape[0]
    assert d == ROW_CHUNKS * LANES
    const = lambda i: (0, 0)
    n_ctx_tiles = attn_ctx.shape[0] // TM
    out_specs, out_shape = _sublayer_out_specs(t, d)
    return pl.pallas_call(
        functools.partial(_mla_out_kernel, n_ctx_tiles=n_ctx_tiles),
        grid=(t // TM,),
        in_specs=_ctx_lat_specs(n_ctx_tiles, attn_ctx.shape[1]) + _ctx_lat_specs(n_ctx_tiles, d) + [
            pl.BlockSpec((None, 6, d), lambda i: (cond_of_tile(i), 0, 0)),
            pl.BlockSpec(wo.shape, const),
            pl.BlockSpec((1, d), const),
            pl.BlockSpec(rwh.shape, const),
            pl.BlockSpec(rwl.shape, const),
            pl.BlockSpec(rb.shape, const),
        ],
        out_specs=out_specs,
        out_shape=out_shape,
        compiler_params=_params(1),
        name="mla_out",
    )(attn_ctx, attn_lat, x_ctx, x_lat, mods, wo, nw2, rwh, rwl, rb)


def _pool_kernel(x_ref, mo_ref, xp_ref, mop_ref, xn_ref, mon_ref, modp_ref, mod_ref, nw_ref, pw_ref,
                 ps_ref, nw2_ref, rwh_ref, rwl_ref, rb_ref, x2_ref, h2p_ref, route_ref,
                 *, n_ctx_tiles, ctx_seq_tiles, lat_seq_tiles):
    i = pl.program_id(0)
    is_lat = i >= n_ctx_tiles
    seq_tiles = jnp.where(is_lat, lat_seq_tiles, ctx_seq_tiles)
    in_seq = jnp.where(is_lat, i - n_ctx_tiles, i) % seq_tiles
    has_prev = in_seq > 0
    has_next = in_seq < seq_tiles - 1
    g2p = modp_ref[5:6, :]
    gain1 = nw_ref[...] * (1.0 + mod_ref[1:2, :])
    sh1 = mod_ref[0:1, :]

    def pre(xv, mo):
        xx = xv + g2p * mo
        return xx, _rms(xx) * gain1 + sh1

    xcur, h = pre(x_ref[...], _load_token_major(mo_ref, TM))
    _, hprev = pre(xp_ref[...], _load_token_major(mop_ref, HALO))
    _, hnext = pre(xn_ref[...], _load_token_major(mon_ref, HALO))
    d = h.shape[1]
    gw = d // len(POOL_WINDOWS)
    hb = h.astype(BF16)
    halo = jnp.concatenate(
        [hprev.astype(BF16), hnext.astype(BF16), jnp.zeros((LANES - 2 * HALO, d), BF16)], axis=0)

    t_mid = lax.broadcasted_iota(I32, (TM, TM), 0)
    e_mid = lax.broadcasted_iota(I32, (TM, TM), 1)
    t_hal = lax.broadcasted_iota(I32, (TM, LANES), 0)
    c_hal = lax.broadcasted_iota(I32, (TM, LANES), 1)
    far = jnp.int32(4 * TM)
    pos_prev = jnp.where(has_prev, c_hal - HALO, -far)
    pos_next = jnp.where(has_next, TM + c_hal - HALO, far)
    pos_hal = jnp.where(c_hal < HALO, pos_prev, jnp.where(c_hal < 2 * HALO, pos_next, far))
    t_col = lax.broadcasted_iota(I32, (TM, 1), 0)
    lo_bound = jnp.where(has_prev, -HALO, 0)
    hi_bound = jnp.where(has_next, TM - 1 + HALO, TM - 1)

    ys = []
    for g, w in enumerate(POOL_WINDOWS):
        half = w // 2
        dm = e_mid - t_mid
        band_mid = jnp.where(dm >= -half, jnp.where(dm <= half - 1, 1.0, 0.0), 0.0).astype(BF16)
        dh = pos_hal - t_hal
        band_hal = jnp.where(dh >= -half, jnp.where(dh <= half - 1, 1.0, 0.0), 0.0).astype(BF16)
        sl = slice(g * gw, (g + 1) * gw)
        wsum = (jnp.dot(band_mid, hb[:, sl], preferred_element_type=F32)
                + jnp.dot(band_hal, halo[:, sl], preferred_element_type=F32))
        cnt = (jnp.minimum(t_col + (half - 1), hi_bound)
               - jnp.maximum(t_col - half, lo_bound) + 1).astype(F32)
        pooled = wsum / cnt - h[:, sl]
        ys.append(jnp.dot(pooled.astype(BF16), pw_ref[g], preferred_element_type=F32))
    y = jnp.concatenate(ys, axis=1) * ps_ref[...]
    x2 = xcur + mod_ref[2:3, :] * y
    _finish_sublayer(x2, mod_ref, nw2_ref, rwh_ref, rwl_ref, rb_ref, x2_ref, h2p_ref, route_ref)


def _pool_layer(x1, mo, mods_prev, mods, cond_of_tile, n_ctx_tiles, ctx_seq_tiles, lat_seq_tiles,
                nw, pw, ps, nw2, rwh, rwl, rb):
    t, d = x1.shape
    const = lambda i: (0, 0)
    hb = TM // HALO
    last = t // HALO - 1
    prev_map = lambda i: (jnp.maximum(i * hb - 1, 0), 0)
    next_map = lambda i: (jnp.minimum((i + 1) * hb, last), 0)
    out_specs, out_shape = _sublayer_out_specs(t, d)
    return pl.pallas_call(
        functools.partial(_pool_kernel, n_ctx_tiles=n_ctx_tiles, ctx_seq_tiles=ctx_seq_tiles,
                          lat_seq_tiles=lat_seq_tiles),
        grid=(t // TM,),
        in_specs=[
            pl.BlockSpec((TM, d), lambda i: (i, 0)),
            pl.BlockSpec((TM * ROW_CHUNKS, LANES), lambda i: (i, 0)),
            pl.BlockSpec((HALO, d), prev_map),
            pl.BlockSpec((HALO * ROW_CHUNKS, LANES), prev_map),
            pl.BlockSpec((HALO, d), next_map),
            pl.BlockSpec((HALO * ROW_CHUNKS, LANES), next_map),
            pl.BlockSpec((None, 6, d), lambda i: (cond_of_tile(i), 0, 0)),
            pl.BlockSpec((None, 6, d), lambda i: (cond_of_tile(i), 0, 0)),
            pl.BlockSpec((1, d), const),
            pl.BlockSpec(pw.shape, lambda i: (0, 0, 0)),
            pl.BlockSpec((1, d), const),
            pl.BlockSpec((1, d), const),
            pl.BlockSpec(rwh.shape, const),
            pl.BlockSpec(rwl.shape, const),
            pl.BlockSpec(rb.shape, const),
        ],
        out_specs=out_specs,
        out_shape=out_shape,
        compiler_params=_params(1),
        name="pool_layer",
    )(x1, mo, x1, mo, x1, mo, mods_prev, mods, nw, pw, ps, nw2, rwh, rwl, rb)


def _final_kernel(x_ref, mo_ref, mod_ref, o_ref):
    o_ref[...] = x_ref[...] + mod_ref[5:6, :] * _load_token_major(mo_ref, TM)


def _final(x, mo, mods, cond_of_tile, tile0, n_tiles):
    d = x.shape[1]
    return pl.pallas_call(
        _final_kernel,
        grid=(n_tiles,),
        in_specs=[
            pl.BlockSpec((TM, d), lambda i: (tile0 + i, 0)),
            pl.BlockSpec((TM * ROW_CHUNKS, LANES), lambda i: (tile0 + i, 0)),
            pl.BlockSpec((None, 6, d), lambda i: (cond_of_tile(tile0 + i), 0, 0)),
        ],
        out_specs=pl.BlockSpec((TM, d), lambda i: (i, 0)),
        out_shape=jax.ShapeDtypeStruct((n_tiles * TM, d), F32),
        compiler_params=_params(1),
        name="final_residual",
    )(x, mo, mods)


def _moe_kernel(ea_ref, eb_ref, valid_ref, src_ref, dst_ref, h_hbm, g_ref, wga_ref, wua_ref, wda_ref,
                wgb_ref, wub_ref, wdb_ref, mo_hbm, xbuf, ybuf, sem_in, sem_out):
    i = pl.program_id(0)
    slot = i % 2
    nslot = 1 - slot
    is_valid = valid_ref[i] == 1
    prev_valid = jnp.logical_and(i >= 1, valid_ref[jnp.maximum(i - 1, 0)] == 1)
    prev2_valid = jnp.logical_and(i >= 2, valid_ref[jnp.maximum(i - 2, 0)] == 1)

    def gather_copy(tok, r, slot):
        return pltpu.make_async_copy(
            h_hbm.at[pl.ds(pl.multiple_of(tok * ROW_CHUNKS, ROW_CHUNKS), ROW_CHUNKS)],
            xbuf.at[slot, pl.ds(pl.multiple_of(r * BUF_PITCH, 8), ROW_CHUNKS)], sem_in.at[slot])

    def scatter_copy(tok, r, slot):
        return pltpu.make_async_copy(
            ybuf.at[slot, pl.ds(pl.multiple_of(r * ROW_CHUNKS, ROW_CHUNKS), ROW_CHUNKS)],
            mo_hbm.at[pl.ds(pl.multiple_of(tok * ROW_CHUNKS, ROW_CHUNKS), ROW_CHUNKS)], sem_out.at[slot])

    def for_rows(fn):
        def body(r, carry):
            fn(r)
            return carry
        lax.fori_loop(0, TM, body, 0, unroll=8)

    def gather_start(tile, slot):
        for_rows(lambda r: gather_copy(src_ref[tile * TM + r], r, slot).start())

    def gather_wait(slot):
        for_rows(lambda r: gather_copy(0, 0, slot).wait())

    def scatter_start(tile, slot):
        for_rows(lambda r: scatter_copy(dst_ref[tile * TM + r], r, slot).start())

    def scatter_wait(slot):
        for_rows(lambda r: scatter_copy(0, 0, slot).wait())

    @pl.when(i == 0)
    def _():
        gather_start(0, 0)

    @pl.when(prev2_valid)
    def _():
        scatter_wait(slot)

    @pl.when(is_valid)
    def _():
        gather_wait(slot)
        for r in range(TM):
            gather_copy(src_ref[(i + 1) * TM + r], r, nslot).start()
        x = _load_token_major(xbuf.at[slot], TM, BUF_PITCH).astype(BF16)

        def ffn(wg_ref, wu_ref, wd_ref):
            g = jnp.dot(x, wg_ref[...], preferred_element_type=F32)
            u = jnp.dot(x, wu_ref[...], preferred_element_type=F32)
            a = (g * jax.nn.sigmoid(g)) * u
            return jnp.dot(a.astype(BF16), wd_ref[...], preferred_element_type=F32)

        gates = g_ref[...]
        y = gates[:, 0:1] * ffn(wga_ref, wua_ref, wda_ref) + gates[:, 1:2] * ffn(wgb_ref, wub_ref, wdb_ref)
        _store_token_major(ybuf.at[slot], y)
        scatter_start(i, slot)

    @pl.when(jnp.logical_not(is_valid))
    def _():
        @pl.when(prev_valid)
        def _():
            gather_wait(slot)

        ybuf[slot] = jnp.zeros(ybuf.shape[1:], F32)
        row0 = pl.multiple_of(dst_ref[i * TM] * ROW_CHUNKS, ROW_CHUNKS)
        fill = pltpu.make_async_copy(ybuf.at[slot], mo_hbm.at[pl.ds(row0, TM * ROW_CHUNKS)],
                                     sem_out.at[slot])
        fill.start()
        fill.wait()


def _moe_sorted(h, src_tok, dst_tok, gates, tile_ea, tile_eb, tile_valid, layer, wg, wu, wd):
    n_rows = src_tok.shape[0]
    d, ff = wg.shape[-2:]
    wa = lambda i, ea, eb, va, src, dst: (layer, ea[i], 0, 0)
    wb = lambda i, ea, eb, va, src, dst: (layer, eb[i], 0, 0)
    return pl.pallas_call(
        _moe_kernel,
        grid_spec=pltpu.PrefetchScalarGridSpec(
            num_scalar_prefetch=5,
            grid=(n_rows // TM,),
            in_specs=[
                pl.BlockSpec(memory_space=pl.ANY),
                pl.BlockSpec((TM, 2), lambda i, ea, eb, va, src, dst: (i, 0)),
                pl.BlockSpec((None, None, d, ff), wa),
                pl.BlockSpec((None, None, d, ff), wa),
                pl.BlockSpec((None, None, ff, d), wa),
                pl.BlockSpec((None, None, d, ff), wb),
                pl.BlockSpec((None, None, d, ff), wb),
                pl.BlockSpec((None, None, ff, d), wb),
            ],
            out_specs=pl.BlockSpec(memory_space=pl.ANY),
            scratch_shapes=[
                pltpu.VMEM((2, TM * BUF_PITCH, LANES), F32),
                pltpu.VMEM((2, TM * ROW_CHUNKS, LANES), F32),
                pltpu.SemaphoreType.DMA((2,)),
                pltpu.SemaphoreType.DMA((2,)),
            ],
        ),
        out_shape=jax.ShapeDtypeStruct((n_rows * ROW_CHUNKS, LANES), F32),
        compiler_params=_params(1),
        name="moe_experts",
    )(tile_ea, tile_eb, tile_valid, src_tok, dst_tok, h, gates, wg, wu, wd, wg, wu, wd)


def _invert_kernel(pos_ref, out_ref, fill_ref, sem):
    fill_ref[...] = jnp.full(fill_ref.shape, -1, I32)
    clear = pltpu.make_async_copy(fill_ref, out_ref, sem)
    clear.start()
    clear.wait()

    def place(k, carry):
        out_ref[pos_ref[k]] = k
        return carry

    lax.fori_loop(0, pos_ref.shape[0], place, 0, unroll=8)


def _invert_positions(pos, n_rows):
    padded = n_rows + (-n_rows) % SMEM_WORDS_TILE
    out = pl.pallas_call(
        _invert_kernel,
        grid_spec=pltpu.PrefetchScalarGridSpec(
            num_scalar_prefetch=1,
            grid=(1,),
            in_specs=[],
            out_specs=pl.BlockSpec(memory_space=pltpu.SMEM),
            scratch_shapes=[pltpu.VMEM((padded,), I32), pltpu.SemaphoreType.DMA(())],
        ),
        out_shape=jax.ShapeDtypeStruct((padded,), I32),
        compiler_params=_params(1),
        name="invert_positions",
    )(pos)
    return out[:n_rows]


def _moe_layer(h2p, route, layer, wg, wu, wd):
    t = h2p.shape[0] // ROW_CHUNKS
    n_tiles = t // TM + N_CLASSES + 2
    cls = route[:, 0, :].reshape(t).astype(I32)
    g_a = route[:, 1, :].reshape(t)
    g_b = route[:, 2, :].reshape(t)
    onehot = (cls[:, None] == jnp.arange(N_CLASSES, dtype=I32)[None, :]).astype(I32)
    csum = jnp.cumsum(onehot, axis=0)
    counts = csum[-1]
    rank = jnp.take_along_axis(csum, cls[:, None], axis=1)[:, 0] - 1
    tiles_c = (counts + TM - 1) // TM
    tile_end = jnp.cumsum(tiles_c)
    tile_start = tile_end - tiles_c
    pos = tile_start[cls] * TM + rank
    n_rows = n_tiles * TM
    row_tok = _invert_positions(pos, n_rows)
    is_pad = row_tok < 0
    src_tok = jnp.where(is_pad, jnp.arange(n_rows, dtype=I32) % t, row_tok)
    dst_tok = jnp.where(is_pad, t + jnp.cumsum(is_pad.astype(I32)) - 1, row_tok)
    tile_ids = jnp.arange(n_tiles, dtype=I32)
    total = tile_end[-1]
    tile_valid = (tile_ids < total).astype(I32)
    tile_cls = jnp.sum((tile_end[None, :] <= jnp.minimum(tile_ids, total - 1)[:, None]).astype(I32), axis=1)
    pair = tile_cls % N_PAIRS
    group = tile_cls // N_PAIRS
    tile_ea = group * EXPERTS_PER_GROUP + jnp.asarray(PAIR_A, I32)[pair]
    tile_eb = group * EXPERTS_PER_GROUP + jnp.asarray(PAIR_B, I32)[pair]
    gates = jnp.stack([g_a[src_tok], g_b[src_tok]], axis=1)
    return _moe_sorted(h2p, src_tok, dst_tok, gates, tile_ea, tile_eb, tile_valid, layer, wg, wu, wd)


def _swap_rope_halves(w):
    shp = w.shape
    return w.reshape(shp[:-1] + (2, 2, AXIS_DIM // 2))[..., ::-1, :].reshape(shp)


def _rope_tables(length, norm_rope, scale):
    rows = length // GRID_W
    row = jnp.repeat(jnp.arange(rows, dtype=F32), GRID_W)
    col = jnp.tile(jnp.arange(GRID_W, dtype=F32), rows)
    inv = jnp.power(ROPE_THETA, -jnp.arange(0, AXIS_DIM, 2, dtype=F32) / AXIS_DIM)
    ang = jnp.stack([row[:, None] * inv, col[:, None] * inv], axis=1)
    cos, sin = jnp.cos(ang), jnp.sin(ang)
    c_full = jnp.stack([cos, cos], axis=2).reshape(length, ROPE_DIM)
    s_full = jnp.stack([-sin, sin], axis=2).reshape(length, ROPE_DIM)
    lat = jnp.concatenate([norm_rope * c_full, _swap_rope_halves(norm_rope) * s_full], axis=1)
    ctx = jnp.concatenate([norm_rope, jnp.zeros((ROPE_DIM,), F32)])
    ctx = jnp.broadcast_to(ctx[None, :], (TM, 2 * ROPE_DIM))
    return jnp.concatenate([ctx, lat], axis=0) * scale


def kernel(x_prompt, x_sample, cache_ckv, cache_krope, c, c_ctx, ada_w, ada_b, norm_mix_w, norm_ffn_w,
           mla_w_dq, mla_q_lora_norm, mla_w_uq, mla_w_dkv, mla_kv_lora_norm, mla_w_uk, mla_w_uv,
           mla_q_norm, mla_k_norm, mla_w_o, pool_w, pool_scale, router_w, router_bias,
           moe_w_gate, moe_w_up, moe_w_down):
    batch, seq, d = x_prompt.shape
    dec_batch, dec_seq, _ = x_sample.shape
    past = cache_ckv.shape[2]
    depth = ada_w.shape[0]
    assert seq == TM and past == TM and dec_seq % (2 * TM) == 0
    assert dec_batch + 1 <= 8 and depth == 2
    t_ctx = batch * seq
    t_lat = dec_batch * dec_seq
    n_ctx_tiles = t_ctx // TM
    lat_seq_tiles = dec_seq // TM
    ctx_seq_tiles = seq // TM

    def cond_of_tile(i):
        return jnp.where(i < n_ctx_tiles, 0, 1 + (i - n_ctx_tiles) // lat_seq_tiles)

    def table_of_tile(i):
        return jnp.where(i < n_ctx_tiles, 0, 1 + (i - n_ctx_tiles) % lat_seq_tiles)

    x_ctx = x_prompt.reshape(t_ctx, d)
    x_lat = x_sample.reshape(t_lat, d)
    conds = jnp.concatenate([c_ctx[None, :], c, jnp.zeros((7 - dec_batch, d), F32)], axis=0)
    mods = _adaln(conds, ada_w, ada_b).reshape(depth, 8, 6, d)

    rw = jnp.pad(router_w, ((0, 0), (0, LANES - N_EXPERTS)))
    rwh = rw.astype(BF16)
    rwl = (rw - rwh.astype(F32)).astype(BF16)
    rb = jnp.broadcast_to(router_bias.astype(F32)[:, None], (N_EXPERTS, TM))
    wg = moe_w_gate.astype(BF16)
    wu = moe_w_up.astype(BF16)
    wd = moe_w_down.astype(BF16)

    j = 0
    q_norm, k_norm = mla_q_norm[j], mla_k_norm[j]
    w_uqt = mla_w_uq[j].T.astype(BF16)
    w_dkv = mla_w_dkv[j]
    w_dkv = jnp.concatenate([w_dkv, _swap_rope_halves(w_dkv[:, KV_LORA:])], axis=1).astype(BF16)
    q_scale = ATTN_SCALE * math.log2(math.e)
    tq = _rope_tables(dec_seq, q_norm[QK_NOPE:], q_scale).T
    tk = _rope_tables(dec_seq, k_norm[QK_NOPE:], 1.0)
    qnn = jnp.broadcast_to((q_norm[:QK_NOPE] * q_scale)[:, None], (QK_NOPE, TM))
    knn = k_norm[:QK_NOPE][None, :]

    qt, ckv, kr = _mla_in(x_ctx, x_lat, mods[0], cond_of_tile, table_of_tile, norm_mix_w[0][None, :],
                          mla_w_dq[j].astype(BF16), mla_q_lora_norm[j][None, :], w_uqt, w_dkv,
                          mla_kv_lora_norm[j][None, :], qnn, tq)
    state_ckv = ckv[:t_ctx].reshape(batch, 1, seq, KV_LORA)
    state_krope = kr[:t_ctx, :ROPE_DIM].reshape(batch, 1, seq, ROPE_DIM)

    w_uk = mla_w_uk[j].astype(BF16)
    w_uvt = mla_w_uv[j].T.astype(BF16)
    k_ctx, vt_ctx = _kv_expand(ckv[:t_ctx], kr[:t_ctx], tk, lambda i: 0, w_uk, w_uvt, knn)
    kv_len = past + dec_seq
    ckv_lat = jnp.concatenate(
        [cache_ckv[:, j], ckv[t_ctx:].reshape(dec_batch, dec_seq, KV_LORA)], axis=1)
    kr_cache = jnp.pad(cache_krope[:, j], ((0, 0), (0, 0), (0, LANES - ROPE_DIM)))
    kr_lat = jnp.concatenate([kr_cache, kr[t_ctx:].reshape(dec_batch, dec_seq, LANES)], axis=1)
    kv_seq_tiles = kv_len // TM
    k_lat, vt_lat = _kv_expand(ckv_lat.reshape(-1, KV_LORA), kr_lat.reshape(-1, LANES), tk,
                               lambda i: i % kv_seq_tiles, w_uk, w_uvt, knn)

    logit_bound = q_scale * QK_HEAD * jnp.max(jnp.abs(q_norm)) * jnp.max(jnp.abs(k_norm))
    bounded = (logit_bound <= MAX_SAFE_LOGIT).astype(I32).reshape(1)
    attn_ctx = _attention(bounded, qt, k_ctx, vt_ctx, 0, batch, seq, seq, tq=seq, heads=N_HEADS, group=4)
    attn_lat = _attention(bounded, qt, k_lat, vt_lat, t_ctx, dec_batch, dec_seq, kv_len, tq=2 * TM,
                          heads=2, group=4)

    x1, h2p, route = _mla_out(attn_ctx, attn_lat, x_ctx, x_lat, mods[0], cond_of_tile,
                              mla_w_o[j].astype(BF16), norm_ffn_w[0][None, :], rwh, rwl, rb)
    mo = _moe_layer(h2p, route, 0, wg, wu, wd)

    x2, h2p, route = _pool_layer(x1, mo, mods[0], mods[1], cond_of_tile, n_ctx_tiles, ctx_seq_tiles,
                                 lat_seq_tiles, norm_mix_w[1][None, :], pool_w[0].astype(BF16),
                                 pool_scale[0][None, :], norm_ffn_w[1][None, :], rwh, rwl, rb)
    mo = _moe_layer(h2p, route, 1, wg, wu, wd)
    y_ctx = _final(x2, mo, mods[1], cond_of_tile, 0, n_ctx_tiles)
    y_lat = _final(x2, mo, mods[1], cond_of_tile, n_ctx_tiles, t_lat // TM)

    return (y_ctx.reshape(batch, seq, d), y_lat.reshape(dec_batch, dec_seq, d), state_ckv, state_krope)
```

```python
import functools
import math

import jax
import jax.numpy as jnp
from jax import lax
from jax.experimental import pallas as pl
from jax.experimental.pallas import tpu as pltpu

F32 = jnp.float32
BF16 = jnp.bfloat16
I32 = jnp.int32

GRID_W = 64
N_HEADS = 16
QK_NOPE = 128
ROPE_DIM = 64
QK_HEAD = QK_NOPE + ROPE_DIM
V_HEAD = 128
KV_LORA = 256
ROPE_THETA = 10000.0
AXIS_DIM = ROPE_DIM // 2
ATTN_SCALE = QK_HEAD ** -0.5
POOL_WINDOWS = (2, 4, 8, 16)
N_EXPERTS = 16
N_GROUPS = 4
EXPERTS_PER_GROUP = N_EXPERTS // N_GROUPS
EPS = 1e-6

LANES = 128
SMEM_WORDS_TILE = 1024
HEAD_PAD = 2 * LANES
TM = 256
HALO = 16
N_PAIRS = 6
N_CLASSES = N_GROUPS * N_PAIRS
PAIR_A = (0, 0, 0, 1, 1, 3)
PAIR_B = (1, 2, 3, 3, 2, 2)
ROW_CHUNKS = 16
BUF_PITCH = 24
VMEM_LIMIT = 52 * 1024 * 1024
MAX_SAFE_LOGIT = 64.0


def _params(n_axes):
    return pltpu.CompilerParams(
        dimension_semantics=("arbitrary",) * n_axes, vmem_limit_bytes=VMEM_LIMIT)


def _rms(x):
    return x * lax.rsqrt(jnp.mean(x * x, axis=-1, keepdims=True) + EPS)


def _store_token_major(ref, x, pitch=ROW_CHUNKS):
    rows = x.shape[0]
    for c in range(ROW_CHUNKS):
        ref[pl.ds(c, rows, stride=pitch), :] = x[:, c * LANES:(c + 1) * LANES]


def _load_token_major(ref, rows, pitch=ROW_CHUNKS):
    return jnp.concatenate(
        [ref[pl.ds(c, rows, stride=pitch), :] for c in range(ROW_CHUNKS)], axis=1)


def _ada_kernel(cond_ref, w_ref, b_ref, o_ref):
    c = cond_ref[...]
    s = c * jax.nn.sigmoid(c)
    o_ref[...] = jnp.dot(s, w_ref[...], preferred_element_type=F32,
                         precision=lax.Precision.HIGHEST) + b_ref[...]


def _adaln(conds, ada_w, ada_b):
    depth, d, n = ada_w.shape
    tn = 1024
    return pl.pallas_call(
        _ada_kernel,
        grid=(depth, n // tn),
        in_specs=[
            pl.BlockSpec((8, d), lambda l, j: (0, 0)),
            pl.BlockSpec((None, d, tn), lambda l, j: (l, 0, j)),
            pl.BlockSpec((None, 1, tn), lambda l, j: (l, 0, j)),
        ],
        out_specs=pl.BlockSpec((None, 8, tn), lambda l, j: (l, 0, j)),
        out_shape=jax.ShapeDtypeStruct((depth, 8, n), F32),
        compiler_params=_params(2),
        name="adaln",
    )(conds, ada_w, ada_b.reshape(depth, 1, n))


def _mla_in_kernel(xc_ref, xl_ref, mod_ref, nw_ref, wdq_ref, qln_ref, wuqt_ref, wdkv_ref, kvn_ref,
                   qnn_ref, tq_ref, qt_ref, ckv_ref, kr_ref, *, n_ctx_tiles):
    x = jnp.where(pl.program_id(0) < n_ctx_tiles, xc_ref[...], xl_ref[...])
    h = _rms(x) * (nw_ref[...] * (1.0 + mod_ref[1:2, :])) + mod_ref[0:1, :]
    hb = h.astype(BF16)
    kv = jnp.dot(hb, wdkv_ref[...], preferred_element_type=F32)
    ckv_ref[...] = _rms(kv[:, :KV_LORA]) * kvn_ref[...]
    kr_ref[...] = kv[:, KV_LORA:]
    cq = jnp.dot(hb, wdq_ref[...], preferred_element_type=F32)
    cqn = (_rms(cq) * qln_ref[...]).astype(BF16)
    qt = lax.dot_general(wuqt_ref[...], cqn, (((1,), (1,)), ((), ())),
                         preferred_element_type=F32)
    t_same = tq_ref[:ROPE_DIM, :]
    t_swap = tq_ref[ROPE_DIM:, :]
    qnn = qnn_ref[...]
    half = AXIS_DIM // 2
    for hd in range(N_HEADS):
        a = qt[hd * QK_HEAD: hd * QK_HEAD + QK_NOPE, :]
        b = qt[hd * QK_HEAD + QK_NOPE: (hd + 1) * QK_HEAD, :]
        ss = jnp.sum(a * a, axis=0, keepdims=True) + jnp.sum(b * b, axis=0, keepdims=True)
        r = lax.rsqrt(ss * (1.0 / QK_HEAD) + EPS)
        b_swap = jnp.concatenate([b[half:2 * half], b[:half], b[3 * half:], b[2 * half:3 * half]], axis=0)
        rot = ((b * t_same + b_swap * t_swap) * r).astype(BF16)
        qt_ref[hd * HEAD_PAD: hd * HEAD_PAD + LANES, :] = (a * r * qnn).astype(BF16)
        qt_ref[hd * HEAD_PAD + LANES: hd * HEAD_PAD + LANES + ROPE_DIM, :] = rot
        qt_ref[hd * HEAD_PAD + LANES + ROPE_DIM: (hd + 1) * HEAD_PAD, :] = rot


def _ctx_lat_specs(n_ctx_tiles, width):
    return [pl.BlockSpec((TM, width), lambda i: (jnp.minimum(i, n_ctx_tiles - 1), 0)),
            pl.BlockSpec((TM, width), lambda i: (jnp.maximum(i - n_ctx_tiles, 0), 0))]


def _mla_in(x_ctx, x_lat, mods, cond_of_tile, table_of_tile, nw, wdq, qln, wuqt, wdkv, kvn, qnn, tq):
    d = x_ctx.shape[1]
    n_ctx_tiles = x_ctx.shape[0] // TM
    t = x_ctx.shape[0] + x_lat.shape[0]
    nt = t // TM
    const = lambda i: (0, 0)
    return pl.pallas_call(
        functools.partial(_mla_in_kernel, n_ctx_tiles=n_ctx_tiles),
        grid=(nt,),
        in_specs=_ctx_lat_specs(n_ctx_tiles, d) + [
            pl.BlockSpec((None, 6, d), lambda i: (cond_of_tile(i), 0, 0)),
            pl.BlockSpec((1, d), const),
            pl.BlockSpec(wdq.shape, const),
            pl.BlockSpec((1, wdq.shape[1]), const),
            pl.BlockSpec(wuqt.shape, const),
            pl.BlockSpec(wdkv.shape, const),
            pl.BlockSpec((1, KV_LORA), const),
            pl.BlockSpec((LANES, TM), const),
            pl.BlockSpec((LANES, TM), lambda i: (0, table_of_tile(i))),
        ],
        out_specs=[
            pl.BlockSpec((N_HEADS * HEAD_PAD, TM), lambda i: (0, i)),
            pl.BlockSpec((TM, KV_LORA), lambda i: (i, 0)),
            pl.BlockSpec((TM, LANES), lambda i: (i, 0)),
        ],
        out_shape=[
            jax.ShapeDtypeStruct((N_HEADS * HEAD_PAD, t), BF16),
            jax.ShapeDtypeStruct((t, KV_LORA), F32),
            jax.ShapeDtypeStruct((t, LANES), F32),
        ],
        compiler_params=_params(1),
        name="mla_in",
    )(x_ctx, x_lat, mods, nw, wdq, qln, wuqt, wdkv, kvn, qnn, tq)


def _kv_expand_kernel(ckv_ref, kr_ref, tk_ref, wuk_ref, wuvt_ref, knn_ref, k_ref, vt_ref):
    cb = ckv_ref[...].astype(BF16)
    kn = jnp.dot(cb, wuk_ref[...], preferred_element_type=F32)
    vt_ref[...] = lax.dot_general(wuvt_ref[...], cb, (((1,), (1,)), ((), ())),
                                  preferred_element_type=F32).astype(BF16)
    kr = kr_ref[...]
    rope_lane = lax.broadcasted_iota(I32, (1, LANES), 1) < ROPE_DIM
    ssr = jnp.sum(jnp.where(rope_lane, kr * kr, 0.0), axis=-1, keepdims=True)
    bk = kr * tk_ref[...]
    rot = jnp.where(rope_lane, bk + pltpu.roll(bk, ROPE_DIM, axis=1), 0.0)
    knn = knn_ref[...]
    for hd in range(N_HEADS):
        a = kn[:, hd * LANES: (hd + 1) * LANES]
        r = lax.rsqrt((jnp.sum(a * a, axis=-1, keepdims=True) + ssr) * (1.0 / QK_HEAD) + EPS)
        k_ref[:, hd * HEAD_PAD: hd * HEAD_PAD + LANES] = (a * r * knn).astype(BF16)
        k_ref[:, hd * HEAD_PAD + LANES: (hd + 1) * HEAD_PAD] = (rot * r).astype(BF16)


def _kv_expand(ckv, kr, tk, table_of_tile, wuk, wuvt, knn):
    t = ckv.shape[0]
    const = lambda i: (0, 0)
    return pl.pallas_call(
        _kv_expand_kernel,
        grid=(t // TM,),
        in_specs=[
            pl.BlockSpec((TM, KV_LORA), lambda i: (i, 0)),
            pl.BlockSpec((TM, LANES), lambda i: (i, 0)),
            pl.BlockSpec((TM, LANES), lambda i: (table_of_tile(i), 0)),
            pl.BlockSpec(wuk.shape, const),
            pl.BlockSpec(wuvt.shape, const),
            pl.BlockSpec((1, LANES), const),
        ],
        out_specs=[
            pl.BlockSpec((TM, N_HEADS * HEAD_PAD), lambda i: (i, 0)),
            pl.BlockSpec((None, N_HEADS * V_HEAD, TM), lambda i: (i, 0, 0)),
        ],
        out_shape=[
            jax.ShapeDtypeStruct((t, N_HEADS * HEAD_PAD), BF16),
            jax.ShapeDtypeStruct((t // TM, N_HEADS * V_HEAD, TM), BF16),
        ],
        compiler_params=_params(1),
        name="kv_expand",
    )(ckv, kr, tk, wuk, wuvt, knn)


def _attn_kernel(bounded_ref, qt_ref, k_ref, vt_ref, o_ref, *, heads, nk, group):
    tq = qt_ref.shape[1]
    chains = [(hh, sub) for hh in range(heads) for sub in range(tq // TM)]

    def load_q(part):
        return [qt_ref[hh * HEAD_PAD:(hh + 1) * HEAD_PAD, sub * TM:(sub + 1) * TM] for hh, sub in part]

    def store(part, ls, accs):
        for (hh, sub), l, acc in zip(part, ls, accs):
            o_ref[sub * TM:(sub + 1) * TM, hh * V_HEAD:(hh + 1) * V_HEAD] = (acc / l).T.astype(BF16)

    @pl.when(bounded_ref[0] == 1)
    def _():
        for c0 in range(0, len(chains), group):
            part = chains[c0:c0 + group]
            qts = load_q(part)
            lps = [jnp.zeros((8, TM), F32) for _ in part]
            accs = [None for _ in part]
            p_prev = None
            for j in range(nk + 1):
                p_cur = []
                if j < nk:
                    for n, ((hh, _), qt) in enumerate(zip(part, qts)):
                        ks = k_ref[j * TM:(j + 1) * TM, hh * HEAD_PAD:(hh + 1) * HEAD_PAD]
                        p = jnp.exp2(jnp.dot(ks, qt, preferred_element_type=F32))
                        lps[n] = lps[n] + jnp.sum(p.reshape(TM // 8, 8, TM), axis=0)
                        p_cur.append(p.astype(BF16))
                if j > 0:
                    for n, (hh, _) in enumerate(part):
                        vt = vt_ref[j - 1, hh * V_HEAD:(hh + 1) * V_HEAD, :]
                        pv = jnp.dot(vt, p_prev[n], preferred_element_type=F32)
                        accs[n] = pv if accs[n] is None else accs[n] + pv
                p_prev = p_cur
            store(part, [jnp.sum(lp, axis=0, keepdims=True) for lp in lps], accs)

    @pl.when(bounded_ref[0] == 0)
    def _():
        for chain in chains:
            hh = chain[0]
            qt, = load_q([chain])

            def body(j, carry, hh=hh, qt=qt):
                m, l, acc = carry
                start = pl.multiple_of(j * TM, TM)
                ks = k_ref[pl.ds(start, TM), hh * HEAD_PAD:(hh + 1) * HEAD_PAD]
                s = jnp.dot(ks, qt, preferred_element_type=F32)
                mn = jnp.maximum(m, jnp.max(s, axis=0, keepdims=True))
                alpha = jnp.exp2(m - mn)
                p = jnp.exp2(s - mn)
                l = alpha * l + jnp.sum(p, axis=0, keepdims=True)
                vt = vt_ref[j, hh * V_HEAD:(hh + 1) * V_HEAD, :]
                acc = alpha * acc + jnp.dot(vt, p.astype(BF16), preferred_element_type=F32)
                return mn, l, acc

            init = (jnp.full((1, TM), -jnp.inf, F32), jnp.zeros((1, TM), F32),
                    jnp.zeros((V_HEAD, TM), F32))
            _, l, acc = lax.fori_loop(0, nk, body, init)
            store([chain], [l], [acc])


def _attention(bounded, qt, k, vt, q_col0, n_batch, q_len, kv_len, tq, heads, group):
    nq = q_len // tq
    hb = N_HEADS // heads
    qb0 = q_col0 // tq
    nk = kv_len // TM
    return pl.pallas_call(
        functools.partial(_attn_kernel, heads=heads, nk=nk, group=group),
        grid_spec=pltpu.PrefetchScalarGridSpec(
            num_scalar_prefetch=1,
            grid=(n_batch, hb, nq),
            in_specs=[
                pl.BlockSpec((heads * HEAD_PAD, tq), lambda b, h, i, f: (h, qb0 + b * nq + i)),
                pl.BlockSpec((kv_len, heads * HEAD_PAD), lambda b, h, i, f: (b, h)),
                pl.BlockSpec((nk, heads * V_HEAD, TM), lambda b, h, i, f: (b, h, 0)),
            ],
            out_specs=pl.BlockSpec((tq, heads * V_HEAD), lambda b, h, i, f: (b * nq + i, h)),
        ),
        out_shape=jax.ShapeDtypeStruct((n_batch * q_len, N_HEADS * V_HEAD), BF16),
        compiler_params=_params(3),
        name="attention",
    )(bounded, qt, k, vt)


def _route(h2, rwh_ref, rwl_ref, rb_ref, route_ref):
    hi = h2.astype(BF16)
    lo = (h2 - hi.astype(F32)).astype(BF16)
    rwh = rwh_ref[...]
    logits = (jnp.dot(hi, rwh, preferred_element_type=F32)
              + jnp.dot(lo, rwh, preferred_element_type=F32)
              + jnp.dot(hi, rwl_ref[...], preferred_element_type=F32))
    lt = logits.T[:N_EXPERTS, :]
    scores = jax.nn.sigmoid(lt)
    sel = scores + rb_ref[...]
    srow = [sel[e:e + 1, :] for e in range(N_EXPERTS)]
    prow = [scores[e:e + 1, :] for e in range(N_EXPERTS)]

    def top2_sum(a, b, c, d):
        hab, lab = jnp.maximum(a, b), jnp.minimum(a, b)
        hcd, lcd = jnp.maximum(c, d), jnp.minimum(c, d)
        return jnp.maximum(hab, hcd) + jnp.maximum(jnp.minimum(hab, hcd), jnp.maximum(lab, lcd))

    gs = [top2_sum(*srow[4 * g:4 * g + 4]) for g in range(N_GROUPS)]
    best = jnp.zeros_like(gs[0], dtype=I32)
    bestv = gs[0]
    for g in range(1, N_GROUPS):
        upd = gs[g] > bestv
        best = jnp.where(upd, g, best)
        bestv = jnp.where(upd, gs[g], bestv)

    def pick(rows, j):
        out = rows[j]
        for g in range(1, N_GROUPS):
            out = jnp.where(best == g, rows[4 * g + j], out)
        return out

    sv = [pick(srow, j) for j in range(EXPERTS_PER_GROUP)]
    pv = [pick(prow, j) for j in range(EXPERTS_PER_GROUP)]
    i1 = jnp.zeros_like(best)
    v1 = sv[0]
    for j in range(1, EXPERTS_PER_GROUP):
        upd = sv[j] > v1
        i1 = jnp.where(upd, j, i1)
        v1 = jnp.where(upd, sv[j], v1)
    neg = jnp.float32(-jnp.inf)
    i2 = jnp.where(i1 == 0, 1, 0).astype(I32)
    v2 = jnp.where(i1 == 0, sv[1], sv[0])
    for j in range(1, EXPERTS_PER_GROUP):
        cand = jnp.where(i1 == j, neg, sv[j])
        upd = cand > v2
        i2 = jnp.where(upd, j, i2)
        v2 = jnp.where(upd, cand, v2)
    ilo = jnp.minimum(i1, i2)
    ihi = jnp.maximum(i1, i2)
    pair = jnp.where(ilo == 0, ihi - 1, jnp.where(ilo == 1, 6 - ihi, N_PAIRS - 1))
    cls = best * N_PAIRS + pair

    def take(vals, idx):
        out = vals[0]
        for j in range(1, EXPERTS_PER_GROUP):
            out = jnp.where(idx == j, vals[j], out)
        return out

    wlo = take(pv, ilo)
    whi = take(pv, ihi)
    den = wlo + whi
    a_is_hi = pair == N_PAIRS - 1
    gate_a = jnp.where(a_is_hi, whi, wlo) / den
    gate_b = jnp.where(a_is_hi, wlo, whi) / den
    route_ref[...] = jnp.concatenate(
        [cls.astype(F32), gate_a, gate_b, jnp.zeros((5, cls.shape[1]), F32)], axis=0)


def _finish_sublayer(x_new, mod_ref, nw2_ref, rwh_ref, rwl_ref, rb_ref, x_out_ref, h2p_ref, route_ref):
    x_out_ref[...] = x_new
    h2 = _rms(x_new) * (nw2_ref[...] * (1.0 + mod_ref[4:5, :])) + mod_ref[3:4, :]
    _store_token_major(h2p_ref, h2)
    _route(h2, rwh_ref, rwl_ref, rb_ref, route_ref)


def _mla_out_kernel(attn_c_ref, attn_l_ref, xc_ref, xl_ref, mod_ref, wo_ref, nw2_ref, rwh_ref, rwl_ref,
                    rb_ref, x1_ref, h2p_ref, route_ref, *, n_ctx_tiles):
    is_ctx = pl.program_id(0) < n_ctx_tiles
    attn = jnp.where(is_ctx, attn_c_ref[...], attn_l_ref[...])
    mix = jnp.dot(attn, wo_ref[...], preferred_element_type=F32)
    x1 = jnp.where(is_ctx, xc_ref[...], xl_ref[...]) + mod_ref[2:3, :] * mix
    _finish_sublayer(x1, mod_ref, nw2_ref, rwh_ref, rwl_ref, rb_ref, x1_ref, h2p_ref, route_ref)


def _sublayer_out_specs(t, d):
    nt = t // TM
    specs = [
        pl.BlockSpec((TM, d), lambda i: (i, 0)),
        pl.BlockSpec((TM * ROW_CHUNKS, LANES), lambda i: (i, 0)),
        pl.BlockSpec((None, 8, TM), lambda i: (i, 0, 0)),
    ]
    shapes = [
        jax.ShapeDtypeStruct((t, d), F32),
        jax.ShapeDtypeStruct((t * ROW_CHUNKS, LANES), F32),
        jax.ShapeDtypeStruct((nt, 8, TM), F32),
    ]
    return specs, shapes


def _mla_out(attn_ctx, attn_lat, x_ctx, x_lat, mods, cond_of_tile, wo, nw2, rwh, rwl, rb):
    d = x_ctx.shape[1]
    t = x_ctx.shape[0] + x_lat.shape[0]
    assert d == ROW_CHUNKS * LANES
    const = lambda i: (0, 0)
    n_ctx_tiles = attn_ctx.shape[0] // TM
    out_specs, out_shape = _sublayer_out_specs(t, d)
    return pl.pallas_call(
        functools.partial(_mla_out_kernel, n_ctx_tiles=n_ctx_tiles),
        grid=(t // TM,),
        in_specs=_ctx_lat_specs(n_ctx_tiles, attn_ctx.shape[1]) + _ctx_lat_specs(n_ctx_tiles, d) + [
            pl.BlockSpec((None, 6, d), lambda i: (cond_of_tile(i), 0, 0)),
            pl.BlockSpec(wo.shape, const),
            pl.BlockSpec((1, d), const),
            pl.BlockSpec(rwh.shape, const),
            pl.BlockSpec(rwl.shape, const),
            pl.BlockSpec(rb.shape, const),
        ],
        out_specs=out_specs,
        out_shape=out_shape,
        compiler_params=_params(1),
        name="mla_out",
    )(attn_ctx, attn_lat, x_ctx, x_lat, mods, wo, nw2, rwh, rwl, rb)


def _pool_bands():
    t = jnp.arange(TM, dtype=I32)[:, None]
    e = jnp.arange(TM, dtype=I32)[None, :]
    c = jnp.arange(LANES, dtype=I32)[None, :]
    pos_hal = jnp.where(c < HALO, c - HALO, jnp.where(c < 2 * HALO, TM + c - HALO, 4 * TM))
    mids, hals = [], []
    for w in POOL_WINDOWS:
        half = w // 2
        mids.append(((e - t >= -half) & (e - t <= half - 1)).astype(BF16))
        hals.append(((pos_hal - t >= -half) & (pos_hal - t <= half - 1)).astype(BF16))
    return jnp.stack(mids), jnp.stack(hals)


def _pool_kernel(x_ref, mo_ref, xp_ref, mop_ref, xn_ref, mon_ref, modp_ref, mod_ref, nw_ref, pw_ref,
                 ps_ref, nw2_ref, rwh_ref, rwl_ref, rb_ref, bmid_ref, bhal_ref, x2_ref, h2p_ref, route_ref,
                 *, n_ctx_tiles, ctx_seq_tiles, lat_seq_tiles):
    i = pl.program_id(0)
    is_lat = i >= n_ctx_tiles
    seq_tiles = jnp.where(is_lat, lat_seq_tiles, ctx_seq_tiles)
    in_seq = jnp.where(is_lat, i - n_ctx_tiles, i) % seq_tiles
    has_prev = in_seq > 0
    has_next = in_seq < seq_tiles - 1
    g2p = modp_ref[5:6, :]
    gain1 = nw_ref[...] * (1.0 + mod_ref[1:2, :])
    sh1 = mod_ref[0:1, :]

    def pre(xv, mo):
        xx = xv + g2p * mo
        return xx, _rms(xx) * gain1 + sh1

    xcur, h = pre(x_ref[...], _load_token_major(mo_ref, TM))
    _, hprev = pre(xp_ref[...], _load_token_major(mop_ref, HALO))
    _, hnext = pre(xn_ref[...], _load_token_major(mon_ref, HALO))
    d = h.shape[1]
    gw = d // len(POOL_WINDOWS)
    hb = h.astype(BF16)
    hprev = jnp.where(has_prev, hprev, 0.0)
    hnext = jnp.where(has_next, hnext, 0.0)
    halo = jnp.concatenate(
        [hprev.astype(BF16), hnext.astype(BF16), jnp.zeros((LANES - 2 * HALO, d), BF16)], axis=0)
    t_col = lax.broadcasted_iota(I32, (TM, 1), 0)
    lo_bound = jnp.where(has_prev, -HALO, 0)
    hi_bound = jnp.where(has_next, TM - 1 + HALO, TM - 1)

    ys = []
    for g, w in enumerate(POOL_WINDOWS):
        half = w // 2
        sl = slice(g * gw, (g + 1) * gw)
        wsum = (jnp.dot(bmid_ref[g], hb[:, sl], preferred_element_type=F32)
                + jnp.dot(bhal_ref[g], halo[:, sl], preferred_element_type=F32))
        cnt = (jnp.minimum(t_col + (half - 1), hi_bound)
               - jnp.maximum(t_col - half, lo_bound) + 1).astype(F32)
        pooled = wsum / cnt - h[:, sl]
        ys.append(jnp.dot(pooled.astype(BF16), pw_ref[g], preferred_element_type=F32))
    y = jnp.concatenate(ys, axis=1) * ps_ref[...]
    x2 = xcur + mod_ref[2:3, :] * y
    _finish_sublayer(x2, mod_ref, nw2_ref, rwh_ref, rwl_ref, rb_ref, x2_ref, h2p_ref, route_ref)


def _pool_layer(x1, mo, mods_prev, mods, cond_of_tile, n_ctx_tiles, ctx_seq_tiles, lat_seq_tiles,
                nw, pw, ps, nw2, rwh, rwl, rb):
    t, d = x1.shape
    const = lambda i: (0, 0)
    hb = TM // HALO
    last = t // HALO - 1
    prev_map = lambda i: (jnp.maximum(i * hb - 1, 0), 0)
    next_map = lambda i: (jnp.minimum((i + 1) * hb, last), 0)
    band_mid, band_hal = _pool_bands()
    out_specs, out_shape = _sublayer_out_specs(t, d)
    return pl.pallas_call(
        functools.partial(_pool_kernel, n_ctx_tiles=n_ctx_tiles, ctx_seq_tiles=ctx_seq_tiles,
                          lat_seq_tiles=lat_seq_tiles),
        grid=(t // TM,),
        in_specs=[
            pl.BlockSpec((TM, d), lambda i: (i, 0)),
            pl.BlockSpec((TM * ROW_CHUNKS, LANES), lambda i: (i, 0)),
            pl.BlockSpec((HALO, d), prev_map),
            pl.BlockSpec((HALO * ROW_CHUNKS, LANES), prev_map),
            pl.BlockSpec((HALO, d), next_map),
            pl.BlockSpec((HALO * ROW_CHUNKS, LANES), next_map),
            pl.BlockSpec((None, 6, d), lambda i: (cond_of_tile(i), 0, 0)),
            pl.BlockSpec((None, 6, d), lambda i: (cond_of_tile(i), 0, 0)),
            pl.BlockSpec((1, d), const),
            pl.BlockSpec(pw.shape, lambda i: (0, 0, 0)),
            pl.BlockSpec((1, d), const),
            pl.BlockSpec((1, d), const),
            pl.BlockSpec(rwh.shape, const),
            pl.BlockSpec(rwl.shape, const),
            pl.BlockSpec(rb.shape, const),
            pl.BlockSpec(band_mid.shape, lambda i: (0, 0, 0)),
            pl.BlockSpec(band_hal.shape, lambda i: (0, 0, 0)),
        ],
        out_specs=out_specs,
        out_shape=out_shape,
        compiler_params=_params(1),
        name="pool_layer",
    )(x1, mo, x1, mo, x1, mo, mods_prev, mods, nw, pw, ps, nw2, rwh, rwl, rb, band_mid, band_hal)


def _final_kernel(x_ref, mo_ref, mod_ref, o_ref):
    o_ref[...] = x_ref[...] + mod_ref[5:6, :] * _load_token_major(mo_ref, TM)


def _final(x, mo, mods, cond_of_tile, tile0, n_tiles):
    d = x.shape[1]
    return pl.pallas_call(
        _final_kernel,
        grid=(n_tiles,),
        in_specs=[
            pl.BlockSpec((TM, d), lambda i: (tile0 + i, 0)),
            pl.BlockSpec((TM * ROW_CHUNKS, LANES), lambda i: (tile0 + i, 0)),
            pl.BlockSpec((None, 6, d), lambda i: (cond_of_tile(tile0 + i), 0, 0)),
        ],
        out_specs=pl.BlockSpec((TM, d), lambda i: (i, 0)),
        out_shape=jax.ShapeDtypeStruct((n_tiles * TM, d), F32),
        compiler_params=_params(1),
        name="final_residual",
    )(x, mo, mods)


def _moe_kernel(ea_ref, eb_ref, valid_ref, src_ref, dst_ref, h_hbm, g_ref, wga_ref, wua_ref, wda_ref,
                wgb_ref, wub_ref, wdb_ref, mo_hbm, xbuf, ybuf, sem_in, sem_out):
    i = pl.program_id(0)
    slot = i % 2
    nslot = 1 - slot
    is_valid = valid_ref[i] == 1
    prev_valid = jnp.logical_and(i >= 1, valid_ref[jnp.maximum(i - 1, 0)] == 1)
    prev2_valid = jnp.logical_and(i >= 2, valid_ref[jnp.maximum(i - 2, 0)] == 1)

    def gather_copy(tok, r, slot):
        return pltpu.make_async_copy(
            h_hbm.at[pl.ds(pl.multiple_of(tok * ROW_CHUNKS, ROW_CHUNKS), ROW_CHUNKS)],
            xbuf.at[slot, pl.ds(pl.multiple_of(r * BUF_PITCH, 8), ROW_CHUNKS)], sem_in.at[slot])

    def scatter_copy(tok, r, slot):
        return pltpu.make_async_copy(
            ybuf.at[slot, pl.ds(pl.multiple_of(r * ROW_CHUNKS, ROW_CHUNKS), ROW_CHUNKS)],
            mo_hbm.at[pl.ds(pl.multiple_of(tok * ROW_CHUNKS, ROW_CHUNKS), ROW_CHUNKS)], sem_out.at[slot])

    def for_rows(fn):
        def body(r, carry):
            fn(r)
            return carry
        lax.fori_loop(0, TM, body, 0, unroll=8)

    def gather_start(tile, slot):
        for_rows(lambda r: gather_copy(src_ref[tile * TM + r], r, slot).start())

    def gather_wait(slot):
        for_rows(lambda r: gather_copy(0, 0, slot).wait())

    def scatter_start(tile, slot):
        for_rows(lambda r: scatter_copy(dst_ref[tile * TM + r], r, slot).start())

    def scatter_wait(slot):
        for_rows(lambda r: scatter_copy(0, 0, slot).wait())

    @pl.when(i == 0)
    def _():
        gather_start(0, 0)

    @pl.when(prev2_valid)
    def _():
        scatter_wait(slot)

    @pl.when(is_valid)
    def _():
        gather_wait(slot)
        for r in range(TM):
            gather_copy(src_ref[(i + 1) * TM + r], r, nslot).start()
        x = _load_token_major(xbuf.at[slot], TM, BUF_PITCH).astype(BF16)

        def ffn(wg_ref, wu_ref, wd_ref):
            g = jnp.dot(x, wg_ref[...], preferred_element_type=F32)
            u = jnp.dot(x, wu_ref[...], preferred_element_type=F32)
            a = (g * jax.nn.sigmoid(g)) * u
            return jnp.dot(a.astype(BF16), wd_ref[...], preferred_element_type=F32)

        gates = g_ref[...]
        y = gates[:, 0:1] * ffn(wga_ref, wua_ref, wda_ref) + gates[:, 1:2] * ffn(wgb_ref, wub_ref, wdb_ref)
        _store_token_major(ybuf.at[slot], y)
        scatter_start(i, slot)

    @pl.when(jnp.logical_not(is_valid))
    def _():
        @pl.when(prev_valid)
        def _():
            gather_wait(slot)

        ybuf[slot] = jnp.zeros(ybuf.shape[1:], F32)
        row0 = pl.multiple_of(dst_ref[i * TM] * ROW_CHUNKS, ROW_CHUNKS)
        fill = pltpu.make_async_copy(ybuf.at[slot], mo_hbm.at[pl.ds(row0, TM * ROW_CHUNKS)],
                                     sem_out.at[slot])
        fill.start()
        fill.wait()


def _moe_sorted(h, src_tok, dst_tok, gates, tile_ea, tile_eb, tile_valid, layer, wg, wu, wd):
    n_rows = src_tok.shape[0]
    d, ff = wg.shape[-2:]
    wa = lambda i, ea, eb, va, src, dst: (layer, ea[i], 0, 0)
    wb = lambda i, ea, eb, va, src, dst: (layer, eb[i], 0, 0)
    return pl.pallas_call(
        _moe_kernel,
        grid_spec=pltpu.PrefetchScalarGridSpec(
            num_scalar_prefetch=5,
            grid=(n_rows // TM,),
            in_specs=[
                pl.BlockSpec(memory_space=pl.ANY),
                pl.BlockSpec((TM, 2), lambda i, ea, eb, va, src, dst: (i, 0)),
                pl.BlockSpec((None, None, d, ff), wa),
                pl.BlockSpec((None, None, d, ff), wa),
                pl.BlockSpec((None, None, ff, d), wa),
                pl.BlockSpec((None, None, d, ff), wb),
                pl.BlockSpec((None, None, d, ff), wb),
                pl.BlockSpec((None, None, ff, d), wb),
            ],
            out_specs=pl.BlockSpec(memory_space=pl.ANY),
            scratch_shapes=[
                pltpu.VMEM((2, TM * BUF_PITCH, LANES), F32),
                pltpu.VMEM((2, TM * ROW_CHUNKS, LANES), F32),
                pltpu.SemaphoreType.DMA((2,)),
                pltpu.SemaphoreType.DMA((2,)),
            ],
        ),
        out_shape=jax.ShapeDtypeStruct((n_rows * ROW_CHUNKS, LANES), F32),
        compiler_params=_params(1),
        name="moe_experts",
    )(tile_ea, tile_eb, tile_valid, src_tok, dst_tok, h, gates, wg, wu, wd, wg, wu, wd)


def _invert_kernel(pos_ref, out_ref, fill_ref, sem):
    fill_ref[...] = jnp.full(fill_ref.shape, -1, I32)
    clear = pltpu.make_async_copy(fill_ref, out_ref, sem)
    clear.start()
    clear.wait()

    def place(k, carry):
        out_ref[pos_ref[k]] = k
        return carry

    lax.fori_loop(0, pos_ref.shape[0], place, 0, unroll=8)


def _invert_positions(pos, n_rows):
    padded = n_rows + (-n_rows) % SMEM_WORDS_TILE
    out = pl.pallas_call(
        _invert_kernel,
        grid_spec=pltpu.PrefetchScalarGridSpec(
            num_scalar_prefetch=1,
            grid=(1,),
            in_specs=[],
            out_specs=pl.BlockSpec(memory_space=pltpu.SMEM),
            scratch_shapes=[pltpu.VMEM((padded,), I32), pltpu.SemaphoreType.DMA(())],
        ),
        out_shape=jax.ShapeDtypeStruct((padded,), I32),
        compiler_params=_params(1),
        name="invert_positions",
    )(pos)
    return out[:n_rows]


def _moe_layer(h2p, route, layer, wg, wu, wd):
    t = h2p.shape[0] // ROW_CHUNKS
    n_tiles = t // TM + N_CLASSES + 2
    cls = route[:, 0, :].reshape(t).astype(I32)
    gate_rows = jnp.swapaxes(route[:, 1:3, :], 1, 2).reshape(t, 2)
    onehot = (cls[:, None] == jnp.arange(N_CLASSES, dtype=I32)[None, :]).astype(I32)
    csum = jnp.cumsum(onehot, axis=0)
    counts = csum[-1]
    tiles_c = (counts + TM - 1) // TM
    tile_end = jnp.cumsum(tiles_c)
    tile_start = tile_end - tiles_c
    pos = jnp.sum(onehot * (csum - 1 + (tile_start * TM)[None, :]), axis=1)
    n_rows = n_tiles * TM
    row_tok = _invert_positions(pos, n_rows)
    is_pad = row_tok < 0
    src_tok = jnp.where(is_pad, jnp.arange(n_rows, dtype=I32) % t, row_tok)
    dst_tok = jnp.where(is_pad, t + jnp.cumsum(is_pad.astype(I32)) - 1, row_tok)
    tile_ids = jnp.arange(n_tiles, dtype=I32)
    total = tile_end[-1]
    tile_valid = (tile_ids < total).astype(I32)
    tile_cls = jnp.sum((tile_end[None, :] <= jnp.minimum(tile_ids, total - 1)[:, None]).astype(I32), axis=1)
    pair = tile_cls % N_PAIRS
    group = tile_cls // N_PAIRS
    tile_ea = group * EXPERTS_PER_GROUP + jnp.asarray(PAIR_A, I32)[pair]
    tile_eb = group * EXPERTS_PER_GROUP + jnp.asarray(PAIR_B, I32)[pair]
    gates = gate_rows[src_tok]
    return _moe_sorted(h2p, src_tok, dst_tok, gates, tile_ea, tile_eb, tile_valid, layer, wg, wu, wd)


def _swap_rope_halves(w):
    shp = w.shape
    return w.reshape(shp[:-1] + (2, 2, AXIS_DIM // 2))[..., ::-1, :].reshape(shp)


def _rope_tables(length, norm_rope, scale):
    rows = length // GRID_W
    row = jnp.repeat(jnp.arange(rows, dtype=F32), GRID_W)
    col = jnp.tile(jnp.arange(GRID_W, dtype=F32), rows)
    inv = jnp.power(ROPE_THETA, -jnp.arange(0, AXIS_DIM, 2, dtype=F32) / AXIS_DIM)
    ang = jnp.stack([row[:, None] * inv, col[:, None] * inv], axis=1)
    cos, sin = jnp.cos(ang), jnp.sin(ang)
    c_full = jnp.stack([cos, cos], axis=2).reshape(length, ROPE_DIM)
    s_full = jnp.stack([-sin, sin], axis=2).reshape(length, ROPE_DIM)
    lat = jnp.concatenate([norm_rope * c_full, _swap_rope_halves(norm_rope) * s_full], axis=1)
    ctx = jnp.concatenate([norm_rope, jnp.zeros((ROPE_DIM,), F32)])
    ctx = jnp.broadcast_to(ctx[None, :], (TM, 2 * ROPE_DIM))
    return jnp.concatenate([ctx, lat], axis=0) * scale


def kernel(x_prompt, x_sample, cache_ckv, cache_krope, c, c_ctx, ada_w, ada_b, norm_mix_w, norm_ffn_w,
           mla_w_dq, mla_q_lora_norm, mla_w_uq, mla_w_dkv, mla_kv_lora_norm, mla_w_uk, mla_w_uv,
           mla_q_norm, mla_k_norm, mla_w_o, pool_w, pool_scale, router_w, router_bias,
           moe_w_gate, moe_w_up, moe_w_down):
    batch, seq, d = x_prompt.shape
    dec_batch, dec_seq, _ = x_sample.shape
    past = cache_ckv.shape[2]
    depth = ada_w.shape[0]
    assert seq == TM and past == TM and dec_seq % (2 * TM) == 0
    assert dec_batch + 1 <= 8 and depth == 2
    t_ctx = batch * seq
    t_lat = dec_batch * dec_seq
    n_ctx_tiles = t_ctx // TM
    lat_seq_tiles = dec_seq // TM
    ctx_seq_tiles = seq // TM

    def cond_of_tile(i):
        return jnp.where(i < n_ctx_tiles, 0, 1 + (i - n_ctx_tiles) // lat_seq_tiles)

    def table_of_tile(i):
        return jnp.where(i < n_ctx_tiles, 0, 1 + (i - n_ctx_tiles) % lat_seq_tiles)

    x_ctx = x_prompt.reshape(t_ctx, d)
    x_lat = x_sample.reshape(t_lat, d)
    conds = jnp.concatenate([c_ctx[None, :], c, jnp.zeros((7 - dec_batch, d), F32)], axis=0)
    mods = _adaln(conds, ada_w, ada_b).reshape(depth, 8, 6, d)

    rw = jnp.pad(router_w, ((0, 0), (0, LANES - N_EXPERTS)))
    rwh = rw.astype(BF16)
    rwl = (rw - rwh.astype(F32)).astype(BF16)
    rb = jnp.broadcast_to(router_bias.astype(F32)[:, None], (N_EXPERTS, TM))
    wg = moe_w_gate.astype(BF16)
    wu = moe_w_up.astype(BF16)
    wd = moe_w_down.astype(BF16)

    j = 0
    q_norm, k_norm = mla_q_norm[j], mla_k_norm[j]
    w_uqt = mla_w_uq[j].T.astype(BF16)
    w_dkv = mla_w_dkv[j]
    w_dkv = jnp.concatenate([w_dkv, _swap_rope_halves(w_dkv[:, KV_LORA:])], axis=1).astype(BF16)
    q_scale = ATTN_SCALE * math.log2(math.e)
    tq = _rope_tables(dec_seq, q_norm[QK_NOPE:], q_scale).T
    tk = _rope_tables(dec_seq, k_norm[QK_NOPE:], 1.0)
    qnn = jnp.broadcast_to((q_norm[:QK_NOPE] * q_scale)[:, None], (QK_NOPE, TM))
    knn = k_norm[:QK_NOPE][None, :]

    qt, ckv, kr = _mla_in(x_ctx, x_lat, mods[0], cond_of_tile, table_of_tile, norm_mix_w[0][None, :],
                          mla_w_dq[j].astype(BF16), mla_q_lora_norm[j][None, :], w_uqt, w_dkv,
                          mla_kv_lora_norm[j][None, :], qnn, tq)
    state_ckv = ckv[:t_ctx].reshape(batch, 1, seq, KV_LORA)
    state_krope = kr[:t_ctx, :ROPE_DIM].reshape(batch, 1, seq, ROPE_DIM)

    w_uk = mla_w_uk[j].astype(BF16)
    w_uvt = mla_w_uv[j].T.astype(BF16)
    k_ctx, vt_ctx = _kv_expand(ckv[:t_ctx], kr[:t_ctx], tk, lambda i: 0, w_uk, w_uvt, knn)
    kv_len = past + dec_seq
    ckv_lat = jnp.concatenate(
        [cache_ckv[:, j], ckv[t_ctx:].reshape(dec_batch, dec_seq, KV_LORA)], axis=1)
    kr_cache = jnp.pad(cache_krope[:, j], ((0, 0), (0, 0), (0, LANES - ROPE_DIM)))
    kr_lat = jnp.concatenate([kr_cache, kr[t_ctx:].reshape(dec_batch, dec_seq, LANES)], axis=1)
    kv_seq_tiles = kv_len // TM
    k_lat, vt_lat = _kv_expand(ckv_lat.reshape(-1, KV_LORA), kr_lat.reshape(-1, LANES), tk,
                               lambda i: i % kv_seq_tiles, w_uk, w_uvt, knn)

    logit_bound = q_scale * QK_HEAD * jnp.max(jnp.abs(q_norm)) * jnp.max(jnp.abs(k_norm))
    bounded = (logit_bound <= MAX_SAFE_LOGIT).astype(I32).reshape(1)
    attn_ctx = _attention(bounded, qt, k_ctx, vt_ctx, 0, batch, seq, seq, tq=seq, heads=N_HEADS, group=4)
    attn_lat = _attention(bounded, qt, k_lat, vt_lat, t_ctx, dec_batch, dec_seq, kv_len, tq=2 * TM,
                          heads=4, group=8)

    x1, h2p, route = _mla_out(attn_ctx, attn_lat, x_ctx, x_lat, mods[0], cond_of_tile,
                              mla_w_o[j].astype(BF16), norm_ffn_w[0][None, :], rwh, rwl, rb)
    mo = _moe_layer(h2p, route, 0, wg, wu, wd)

    x2, h2p, route = _pool_layer(x1, mo, mods[0], mods[1], cond_of_tile, n_ctx_tiles, ctx_seq_tiles,
                                 lat_seq_tiles, norm_mix_w[1][None, :], pool_w[0].astype(BF16),
                                 pool_scale[0][None, :], norm_ffn_w[1][None, :], rwh, rwl, rb)
    mo = _moe_layer(h2p, route, 1, wg, wu, wd)
    y_ctx = _final(x2, mo, mods[1], cond_of_tile, 0, n_ctx_tiles)
    y_lat = _final(x2, mo, mods[1], cond_of_tile, n_ctx_tiles, t_lat // TM)

    return (y_ctx.reshape(batch, seq, d), y_lat.reshape(dec_batch, dec_seq, d), state_ckv, state_krope)
```

```python
import functools
import math

import jax
import jax.numpy as jnp
import numpy as np
from jax import lax
from jax.experimental import pallas as pl
from jax.experimental.pallas import tpu as pltpu

F32 = jnp.float32
BF16 = jnp.bfloat16
I32 = jnp.int32

GRID_W = 64
N_HEADS = 16
QK_NOPE = 128
ROPE_DIM = 64
QK_HEAD = QK_NOPE + ROPE_DIM
V_HEAD = 128
KV_LORA = 256
ROPE_THETA = 10000.0
AXIS_DIM = ROPE_DIM // 2
ATTN_SCALE = QK_HEAD ** -0.5
POOL_WINDOWS = (2, 4, 8, 16)
N_EXPERTS = 16
N_GROUPS = 4
EXPERTS_PER_GROUP = N_EXPERTS // N_GROUPS
EPS = 1e-6

LANES = 128
SMEM_WORDS_TILE = 1024
HEAD_PAD = 2 * LANES
TM = 256
HALO = 16
N_PAIRS = 6
N_CLASSES = N_GROUPS * N_PAIRS
PAIR_A = (0, 0, 0, 1, 1, 3)
PAIR_B = (1, 2, 3, 3, 2, 2)
ROW_CHUNKS = 16
BUF_PITCH = 24
VMEM_LIMIT = 52 * 1024 * 1024
MAX_SAFE_LOGIT = 64.0


def _params(n_axes):
    return pltpu.CompilerParams(
        dimension_semantics=("arbitrary",) * n_axes, vmem_limit_bytes=VMEM_LIMIT)


def _rms(x):
    return x * lax.rsqrt(jnp.mean(x * x, axis=-1, keepdims=True) + EPS)


def _store_token_major(ref, x, pitch=ROW_CHUNKS):
    rows = x.shape[0]
    for c in range(ROW_CHUNKS):
        ref[pl.ds(c, rows, stride=pitch), :] = x[:, c * LANES:(c + 1) * LANES]


def _load_token_major(ref, rows, pitch=ROW_CHUNKS):
    return jnp.concatenate(
        [ref[pl.ds(c, rows, stride=pitch), :] for c in range(ROW_CHUNKS)], axis=1)


def _ada_kernel(cond_ref, w_ref, b_ref, o_ref):
    c = cond_ref[...]
    s = c * jax.nn.sigmoid(c)
    o_ref[...] = jnp.dot(s, w_ref[...], preferred_element_type=F32,
                         precision=lax.Precision.HIGHEST) + b_ref[...]


def _adaln(conds, ada_w, ada_b):
    depth, d, n = ada_w.shape
    tn = 1024
    return pl.pallas_call(
        _ada_kernel,
        grid=(depth, n // tn),
        in_specs=[
            pl.BlockSpec((8, d), lambda l, j: (0, 0)),
            pl.BlockSpec((None, d, tn), lambda l, j: (l, 0, j)),
            pl.BlockSpec((None, 1, tn), lambda l, j: (l, 0, j)),
        ],
        out_specs=pl.BlockSpec((None, 8, tn), lambda l, j: (l, 0, j)),
        out_shape=jax.ShapeDtypeStruct((depth, 8, n), F32),
        compiler_params=_params(2),
        name="adaln",
    )(conds, ada_w, ada_b.reshape(depth, 1, n))


def _mla_in_kernel(xc_ref, xl_ref, mod_ref, nw_ref, wdq_ref, qln_ref, wuqt_ref, wdkv_ref, kvn_ref,
                   qnn_ref, tq_ref, qt_ref, ckv_ref, kr_ref, *, n_ctx_tiles):
    x = jnp.where(pl.program_id(0) < n_ctx_tiles, xc_ref[...], xl_ref[...])
    h = _rms(x) * (nw_ref[...] * (1.0 + mod_ref[1:2, :])) + mod_ref[0:1, :]
    hb = h.astype(BF16)
    kv = jnp.dot(hb, wdkv_ref[...], preferred_element_type=F32)
    ckv_ref[...] = _rms(kv[:, :KV_LORA]) * kvn_ref[...]
    kr_ref[...] = kv[:, KV_LORA:]
    cq = jnp.dot(hb, wdq_ref[...], preferred_element_type=F32)
    cqn = (_rms(cq) * qln_ref[...]).astype(BF16)
    qt = lax.dot_general(wuqt_ref[...], cqn, (((1,), (1,)), ((), ())),
                         preferred_element_type=F32)
    t_same = tq_ref[:ROPE_DIM, :]
    t_swap = tq_ref[ROPE_DIM:, :]
    qnn = qnn_ref[...]
    half = AXIS_DIM // 2
    for hd in range(N_HEADS):
        a = qt[hd * QK_HEAD: hd * QK_HEAD + QK_NOPE, :]
        b = qt[hd * QK_HEAD + QK_NOPE: (hd + 1) * QK_HEAD, :]
        ss = jnp.sum(a * a, axis=0, keepdims=True) + jnp.sum(b * b, axis=0, keepdims=True)
        r = lax.rsqrt(ss * (1.0 / QK_HEAD) + EPS)
        b_swap = jnp.concatenate([b[half:2 * half], b[:half], b[3 * half:], b[2 * half:3 * half]], axis=0)
        rot = ((b * t_same + b_swap * t_swap) * r).astype(BF16)
        qt_ref[hd * HEAD_PAD: hd * HEAD_PAD + LANES, :] = (a * r * qnn).astype(BF16)
        qt_ref[hd * HEAD_PAD + LANES: hd * HEAD_PAD + LANES + ROPE_DIM, :] = rot
        qt_ref[hd * HEAD_PAD + LANES + ROPE_DIM: (hd + 1) * HEAD_PAD, :] = rot


def _ctx_lat_specs(n_ctx_tiles, width):
    return [pl.BlockSpec((TM, width), lambda i: (jnp.minimum(i, n_ctx_tiles - 1), 0)),
            pl.BlockSpec((TM, width), lambda i: (jnp.maximum(i - n_ctx_tiles, 0), 0))]


def _mla_in(x_ctx, x_lat, mods, cond_of_tile, table_of_tile, nw, wdq, qln, wuqt, wdkv, kvn, qnn, tq):
    d = x_ctx.shape[1]
    n_ctx_tiles = x_ctx.shape[0] // TM
    t = x_ctx.shape[0] + x_lat.shape[0]
    nt = t // TM
    const = lambda i: (0, 0)
    return pl.pallas_call(
        functools.partial(_mla_in_kernel, n_ctx_tiles=n_ctx_tiles),
        grid=(nt,),
        in_specs=_ctx_lat_specs(n_ctx_tiles, d) + [
            pl.BlockSpec((None, 6, d), lambda i: (cond_of_tile(i), 0, 0)),
            pl.BlockSpec((1, d), const),
            pl.BlockSpec(wdq.shape, const),
            pl.BlockSpec((1, wdq.shape[1]), const),
            pl.BlockSpec(wuqt.shape, const),
            pl.BlockSpec(wdkv.shape, const),
            pl.BlockSpec((1, KV_LORA), const),
            pl.BlockSpec((LANES, TM), const),
            pl.BlockSpec((LANES, TM), lambda i: (0, table_of_tile(i))),
        ],
        out_specs=[
            pl.BlockSpec((N_HEADS * HEAD_PAD, TM), lambda i: (0, i)),
            pl.BlockSpec((TM, KV_LORA), lambda i: (i, 0)),
            pl.BlockSpec((TM, LANES), lambda i: (i, 0)),
        ],
        out_shape=[
            jax.ShapeDtypeStruct((N_HEADS * HEAD_PAD, t), BF16),
            jax.ShapeDtypeStruct((t, KV_LORA), F32),
            jax.ShapeDtypeStruct((t, LANES), F32),
        ],
        compiler_params=_params(1),
        name="mla_in",
    )(x_ctx, x_lat, mods, nw, wdq, qln, wuqt, wdkv, kvn, qnn, tq)


def _kv_expand_kernel(ckv_ref, kr_ref, tk_ref, wuk_ref, wuvt_ref, knn_ref, k_ref, vt_ref):
    cb = ckv_ref[...].astype(BF16)
    kn = jnp.dot(cb, wuk_ref[...], preferred_element_type=F32)
    vt_ref[...] = lax.dot_general(wuvt_ref[...], cb, (((1,), (1,)), ((), ())),
                                  preferred_element_type=F32).astype(BF16)
    kr = kr_ref[...]
    rope_lane = lax.broadcasted_iota(I32, (1, LANES), 1) < ROPE_DIM
    ssr = jnp.sum(jnp.where(rope_lane, kr * kr, 0.0), axis=-1, keepdims=True)
    bk = kr * tk_ref[...]
    rot = jnp.where(rope_lane, bk + pltpu.roll(bk, ROPE_DIM, axis=1), 0.0)
    knn = knn_ref[...]
    for hd in range(N_HEADS):
        a = kn[:, hd * LANES: (hd + 1) * LANES]
        r = lax.rsqrt((jnp.sum(a * a, axis=-1, keepdims=True) + ssr) * (1.0 / QK_HEAD) + EPS)
        k_ref[:, hd * HEAD_PAD: hd * HEAD_PAD + LANES] = (a * r * knn).astype(BF16)
        k_ref[:, hd * HEAD_PAD + LANES: (hd + 1) * HEAD_PAD] = (rot * r).astype(BF16)


def _kv_expand(ckv, kr, tk, table_of_tile, wuk, wuvt, knn):
    t = ckv.shape[0]
    const = lambda i: (0, 0)
    return pl.pallas_call(
        _kv_expand_kernel,
        grid=(t // TM,),
        in_specs=[
            pl.BlockSpec((TM, KV_LORA), lambda i: (i, 0)),
            pl.BlockSpec((TM, LANES), lambda i: (i, 0)),
            pl.BlockSpec((TM, LANES), lambda i: (table_of_tile(i), 0)),
            pl.BlockSpec(wuk.shape, const),
            pl.BlockSpec(wuvt.shape, const),
            pl.BlockSpec((1, LANES), const),
        ],
        out_specs=[
            pl.BlockSpec((TM, N_HEADS * HEAD_PAD), lambda i: (i, 0)),
            pl.BlockSpec((None, N_HEADS * V_HEAD, TM), lambda i: (i, 0, 0)),
        ],
        out_shape=[
            jax.ShapeDtypeStruct((t, N_HEADS * HEAD_PAD), BF16),
            jax.ShapeDtypeStruct((t // TM, N_HEADS * V_HEAD, TM), BF16),
        ],
        compiler_params=_params(1),
        name="kv_expand",
    )(ckv, kr, tk, wuk, wuvt, knn)


def _attn_kernel(bounded_ref, qt_ref, k_ref, vt_ref, o_ref, *, heads, nk, group):
    tq = qt_ref.shape[1]
    chains = [(hh, sub) for hh in range(heads) for sub in range(tq // TM)]

    def load_q(part):
        return [qt_ref[hh * HEAD_PAD:(hh + 1) * HEAD_PAD, sub * TM:(sub + 1) * TM] for hh, sub in part]

    def store(part, ls, accs):
        for (hh, sub), l, acc in zip(part, ls, accs):
            o_ref[sub * TM:(sub + 1) * TM, hh * V_HEAD:(hh + 1) * V_HEAD] = (acc / l).T.astype(BF16)

    @pl.when(bounded_ref[0] == 1)
    def _():
        for c0 in range(0, len(chains), group):
            part = chains[c0:c0 + group]
            qts = load_q(part)
            lps = [jnp.zeros((8, TM), F32) for _ in part]
            accs = [None for _ in part]
            p_prev = None
            for j in range(nk + 1):
                p_cur = []
                if j < nk:
                    for n, ((hh, _), qt) in enumerate(zip(part, qts)):
                        ks = k_ref[j * TM:(j + 1) * TM, hh * HEAD_PAD:(hh + 1) * HEAD_PAD]
                        p = jnp.exp2(jnp.dot(ks, qt, preferred_element_type=F32))
                        lps[n] = lps[n] + jnp.sum(p.reshape(TM // 8, 8, TM), axis=0)
                        p_cur.append(p.astype(BF16))
                if j > 0:
                    for n, (hh, _) in enumerate(part):
                        vt = vt_ref[j - 1, hh * V_HEAD:(hh + 1) * V_HEAD, :]
                        pv = jnp.dot(vt, p_prev[n], preferred_element_type=F32)
                        accs[n] = pv if accs[n] is None else accs[n] + pv
                p_prev = p_cur
            store(part, [jnp.sum(lp, axis=0, keepdims=True) for lp in lps], accs)

    @pl.when(bounded_ref[0] == 0)
    def _():
        for chain in chains:
            hh = chain[0]
            qt, = load_q([chain])

            def body(j, carry, hh=hh, qt=qt):
                m, l, acc = carry
                start = pl.multiple_of(j * TM, TM)
                ks = k_ref[pl.ds(start, TM), hh * HEAD_PAD:(hh + 1) * HEAD_PAD]
                s = jnp.dot(ks, qt, preferred_element_type=F32)
                mn = jnp.maximum(m, jnp.max(s, axis=0, keepdims=True))
                alpha = jnp.exp2(m - mn)
                p = jnp.exp2(s - mn)
                l = alpha * l + jnp.sum(p, axis=0, keepdims=True)
                vt = vt_ref[j, hh * V_HEAD:(hh + 1) * V_HEAD, :]
                acc = alpha * acc + jnp.dot(vt, p.astype(BF16), preferred_element_type=F32)
                return mn, l, acc

            init = (jnp.full((1, TM), -jnp.inf, F32), jnp.zeros((1, TM), F32),
                    jnp.zeros((V_HEAD, TM), F32))
            _, l, acc = lax.fori_loop(0, nk, body, init)
            store([chain], [l], [acc])


def _attention(bounded, qt, k, vt, q_col0, n_batch, q_len, kv_len, tq, heads, group):
    nq = q_len // tq
    hb = N_HEADS // heads
    qb0 = q_col0 // tq
    nk = kv_len // TM
    return pl.pallas_call(
        functools.partial(_attn_kernel, heads=heads, nk=nk, group=group),
        grid_spec=pltpu.PrefetchScalarGridSpec(
            num_scalar_prefetch=1,
            grid=(n_batch, hb, nq),
            in_specs=[
                pl.BlockSpec((heads * HEAD_PAD, tq), lambda b, h, i, f: (h, qb0 + b * nq + i)),
                pl.BlockSpec((kv_len, heads * HEAD_PAD), lambda b, h, i, f: (b, h)),
                pl.BlockSpec((nk, heads * V_HEAD, TM), lambda b, h, i, f: (b, h, 0)),
            ],
            out_specs=pl.BlockSpec((tq, heads * V_HEAD), lambda b, h, i, f: (b * nq + i, h)),
        ),
        out_shape=jax.ShapeDtypeStruct((n_batch * q_len, N_HEADS * V_HEAD), BF16),
        compiler_params=_params(3),
        name="attention",
    )(bounded, qt, k, vt)


def _route(h2, rwh_ref, rwl_ref, rb_ref, route_ref):
    hi = h2.astype(BF16)
    lo = (h2 - hi.astype(F32)).astype(BF16)
    rwh = rwh_ref[...]
    logits = (jnp.dot(hi, rwh, preferred_element_type=F32)
              + jnp.dot(lo, rwh, preferred_element_type=F32)
              + jnp.dot(hi, rwl_ref[...], preferred_element_type=F32))
    lt = logits.T[:N_EXPERTS, :]
    scores = jax.nn.sigmoid(lt)
    sel = scores + rb_ref[...]
    srow = [sel[e:e + 1, :] for e in range(N_EXPERTS)]
    prow = [scores[e:e + 1, :] for e in range(N_EXPERTS)]

    def top2_sum(a, b, c, d):
        hab, lab = jnp.maximum(a, b), jnp.minimum(a, b)
        hcd, lcd = jnp.maximum(c, d), jnp.minimum(c, d)
        return jnp.maximum(hab, hcd) + jnp.maximum(jnp.minimum(hab, hcd), jnp.maximum(lab, lcd))

    gs = [top2_sum(*srow[4 * g:4 * g + 4]) for g in range(N_GROUPS)]
    best = jnp.zeros_like(gs[0], dtype=I32)
    bestv = gs[0]
    for g in range(1, N_GROUPS):
        upd = gs[g] > bestv
        best = jnp.where(upd, g, best)
        bestv = jnp.where(upd, gs[g], bestv)

    def pick(rows, j):
        out = rows[j]
        for g in range(1, N_GROUPS):
            out = jnp.where(best == g, rows[4 * g + j], out)
        return out

    sv = [pick(srow, j) for j in range(EXPERTS_PER_GROUP)]
    pv = [pick(prow, j) for j in range(EXPERTS_PER_GROUP)]
    i1 = jnp.zeros_like(best)
    v1 = sv[0]
    for j in range(1, EXPERTS_PER_GROUP):
        upd = sv[j] > v1
        i1 = jnp.where(upd, j, i1)
        v1 = jnp.where(upd, sv[j], v1)
    neg = jnp.float32(-jnp.inf)
    i2 = jnp.where(i1 == 0, 1, 0).astype(I32)
    v2 = jnp.where(i1 == 0, sv[1], sv[0])
    for j in range(1, EXPERTS_PER_GROUP):
        cand = jnp.where(i1 == j, neg, sv[j])
        upd = cand > v2
        i2 = jnp.where(upd, j, i2)
        v2 = jnp.where(upd, cand, v2)
    ilo = jnp.minimum(i1, i2)
    ihi = jnp.maximum(i1, i2)
    pair = jnp.where(ilo == 0, ihi - 1, jnp.where(ilo == 1, 6 - ihi, N_PAIRS - 1))
    cls = best * N_PAIRS + pair

    def take(vals, idx):
        out = vals[0]
        for j in range(1, EXPERTS_PER_GROUP):
            out = jnp.where(idx == j, vals[j], out)
        return out

    wlo = take(pv, ilo)
    whi = take(pv, ihi)
    den = wlo + whi
    a_is_hi = pair == N_PAIRS - 1
    gate_a = jnp.where(a_is_hi, whi, wlo) / den
    gate_b = jnp.where(a_is_hi, wlo, whi) / den
    route_ref[...] = jnp.concatenate(
        [cls.astype(F32), gate_a, gate_b, jnp.zeros((5, cls.shape[1]), F32)], axis=0)


def _finish_sublayer(x_new, mod_ref, nw2_ref, rwh_ref, rwl_ref, rb_ref, x_out_ref, h2p_ref, route_ref):
    x_out_ref[...] = x_new
    h2 = _rms(x_new) * (nw2_ref[...] * (1.0 + mod_ref[4:5, :])) + mod_ref[3:4, :]
    _store_token_major(h2p_ref, h2)
    _route(h2, rwh_ref, rwl_ref, rb_ref, route_ref)


def _mla_out_kernel(attn_c_ref, attn_l_ref, xc_ref, xl_ref, mod_ref, wo_ref, nw2_ref, rwh_ref, rwl_ref,
                    rb_ref, x1_ref, h2p_ref, route_ref, *, n_ctx_tiles):
    is_ctx = pl.program_id(0) < n_ctx_tiles
    attn = jnp.where(is_ctx, attn_c_ref[...], attn_l_ref[...])
    mix = jnp.dot(attn, wo_ref[...], preferred_element_type=F32)
    x1 = jnp.where(is_ctx, xc_ref[...], xl_ref[...]) + mod_ref[2:3, :] * mix
    _finish_sublayer(x1, mod_ref, nw2_ref, rwh_ref, rwl_ref, rb_ref, x1_ref, h2p_ref, route_ref)


def _sublayer_out_specs(t, d):
    nt = t // TM
    specs = [
        pl.BlockSpec((TM, d), lambda i: (i, 0)),
        pl.BlockSpec((TM * ROW_CHUNKS, LANES), lambda i: (i, 0)),
        pl.BlockSpec((None, 8, TM), lambda i: (i, 0, 0)),
    ]
    shapes = [
        jax.ShapeDtypeStruct((t, d), F32),
        jax.ShapeDtypeStruct((t * ROW_CHUNKS, LANES), F32),
        jax.ShapeDtypeStruct((nt, 8, TM), F32),
    ]
    return specs, shapes


def _mla_out(attn_ctx, attn_lat, x_ctx, x_lat, mods, cond_of_tile, wo, nw2, rwh, rwl, rb):
    d = x_ctx.shape[1]
    t = x_ctx.shape[0] + x_lat.shape[0]
    assert d == ROW_CHUNKS * LANES
    const = lambda i: (0, 0)
    n_ctx_tiles = attn_ctx.shape[0] // TM
    out_specs, out_shape = _sublayer_out_specs(t, d)
    return pl.pallas_call(
        functools.partial(_mla_out_kernel, n_ctx_tiles=n_ctx_tiles),
        grid=(t // TM,),
        in_specs=_ctx_lat_specs(n_ctx_tiles, attn_ctx.shape[1]) + _ctx_lat_specs(n_ctx_tiles, d) + [
            pl.BlockSpec((None, 6, d), lambda i: (cond_of_tile(i), 0, 0)),
            pl.BlockSpec(wo.shape, const),
            pl.BlockSpec((1, d), const),
            pl.BlockSpec(rwh.shape, const),
            pl.BlockSpec(rwl.shape, const),
            pl.BlockSpec(rb.shape, const),
        ],
        out_specs=out_specs,
        out_shape=out_shape,
        compiler_params=_params(1),
        name="mla_out",
    )(attn_ctx, attn_lat, x_ctx, x_lat, mods, wo, nw2, rwh, rwl, rb)


def _pool_bands():
    t = jnp.arange(TM, dtype=I32)[:, None]
    e = jnp.arange(TM, dtype=I32)[None, :]
    c = jnp.arange(LANES, dtype=I32)[None, :]
    pos_hal = jnp.where(c < HALO, c - HALO, jnp.where(c < 2 * HALO, TM + c - HALO, 4 * TM))
    mids, hals = [], []
    for w in POOL_WINDOWS:
        half = w // 2
        mids.append(((e - t >= -half) & (e - t <= half - 1)).astype(BF16))
        hals.append(((pos_hal - t >= -half) & (pos_hal - t <= half - 1)).astype(BF16))
    return jnp.stack(mids), jnp.stack(hals)


def _pool_kernel(x_ref, mo_ref, xp_ref, mop_ref, xn_ref, mon_ref, modp_ref, mod_ref, nw_ref, pw_ref,
                 ps_ref, nw2_ref, rwh_ref, rwl_ref, rb_ref, bmid_ref, bhal_ref, x2_ref, h2p_ref, route_ref,
                 *, n_ctx_tiles, ctx_seq_tiles, lat_seq_tiles):
    i = pl.program_id(0)
    is_lat = i >= n_ctx_tiles
    seq_tiles = jnp.where(is_lat, lat_seq_tiles, ctx_seq_tiles)
    in_seq = jnp.where(is_lat, i - n_ctx_tiles, i) % seq_tiles
    has_prev = in_seq > 0
    has_next = in_seq < seq_tiles - 1
    g2p = modp_ref[5:6, :]
    gain1 = nw_ref[...] * (1.0 + mod_ref[1:2, :])
    sh1 = mod_ref[0:1, :]

    def pre(xv, mo):
        xx = xv + g2p * mo
        return xx, _rms(xx) * gain1 + sh1

    xcur, h = pre(x_ref[...], _load_token_major(mo_ref, TM))
    _, hprev = pre(xp_ref[...], _load_token_major(mop_ref, HALO))
    _, hnext = pre(xn_ref[...], _load_token_major(mon_ref, HALO))
    d = h.shape[1]
    gw = d // len(POOL_WINDOWS)
    hb = h.astype(BF16)
    hprev = jnp.where(has_prev, hprev, 0.0)
    hnext = jnp.where(has_next, hnext, 0.0)
    halo = jnp.concatenate(
        [hprev.astype(BF16), hnext.astype(BF16), jnp.zeros((LANES - 2 * HALO, d), BF16)], axis=0)
    t_col = lax.broadcasted_iota(I32, (TM, 1), 0)
    lo_bound = jnp.where(has_prev, -HALO, 0)
    hi_bound = jnp.where(has_next, TM - 1 + HALO, TM - 1)

    ys = []
    for g, w in enumerate(POOL_WINDOWS):
        half = w // 2
        sl = slice(g * gw, (g + 1) * gw)
        wsum = (jnp.dot(bmid_ref[g], hb[:, sl], preferred_element_type=F32)
                + jnp.dot(bhal_ref[g], halo[:, sl], preferred_element_type=F32))
        cnt = (jnp.minimum(t_col + (half - 1), hi_bound)
               - jnp.maximum(t_col - half, lo_bound) + 1).astype(F32)
        pooled = wsum / cnt - h[:, sl]
        ys.append(jnp.dot(pooled.astype(BF16), pw_ref[g], preferred_element_type=F32))
    y = jnp.concatenate(ys, axis=1) * ps_ref[...]
    x2 = xcur + mod_ref[2:3, :] * y
    _finish_sublayer(x2, mod_ref, nw2_ref, rwh_ref, rwl_ref, rb_ref, x2_ref, h2p_ref, route_ref)


def _pool_layer(x1, mo, mods_prev, mods, cond_of_tile, n_ctx_tiles, ctx_seq_tiles, lat_seq_tiles,
                nw, pw, ps, nw2, rwh, rwl, rb):
    t, d = x1.shape
    const = lambda i: (0, 0)
    hb = TM // HALO
    last = t // HALO - 1
    prev_map = lambda i: (jnp.maximum(i * hb - 1, 0), 0)
    next_map = lambda i: (jnp.minimum((i + 1) * hb, last), 0)
    band_mid, band_hal = _pool_bands()
    out_specs, out_shape = _sublayer_out_specs(t, d)
    return pl.pallas_call(
        functools.partial(_pool_kernel, n_ctx_tiles=n_ctx_tiles, ctx_seq_tiles=ctx_seq_tiles,
                          lat_seq_tiles=lat_seq_tiles),
        grid=(t // TM,),
        in_specs=[
            pl.BlockSpec((TM, d), lambda i: (i, 0)),
            pl.BlockSpec((TM * ROW_CHUNKS, LANES), lambda i: (i, 0)),
            pl.BlockSpec((HALO, d), prev_map),
            pl.BlockSpec((HALO * ROW_CHUNKS, LANES), prev_map),
            pl.BlockSpec((HALO, d), next_map),
            pl.BlockSpec((HALO * ROW_CHUNKS, LANES), next_map),
            pl.BlockSpec((None, 6, d), lambda i: (cond_of_tile(i), 0, 0)),
            pl.BlockSpec((None, 6, d), lambda i: (cond_of_tile(i), 0, 0)),
            pl.BlockSpec((1, d), const),
            pl.BlockSpec(pw.shape, lambda i: (0, 0, 0)),
            pl.BlockSpec((1, d), const),
            pl.BlockSpec((1, d), const),
            pl.BlockSpec(rwh.shape, const),
            pl.BlockSpec(rwl.shape, const),
            pl.BlockSpec(rb.shape, const),
            pl.BlockSpec(band_mid.shape, lambda i: (0, 0, 0)),
            pl.BlockSpec(band_hal.shape, lambda i: (0, 0, 0)),
        ],
        out_specs=out_specs,
        out_shape=out_shape,
        compiler_params=_params(1),
        name="pool_layer",
    )(x1, mo, x1, mo, x1, mo, mods_prev, mods, nw, pw, ps, nw2, rwh, rwl, rb, band_mid, band_hal)


def _final_kernel(x_ref, mo_ref, mod_ref, o_ref):
    o_ref[...] = x_ref[...] + mod_ref[5:6, :] * _load_token_major(mo_ref, TM)


def _final(x, mo, mods, cond_of_tile, tile0, n_tiles):
    d = x.shape[1]
    return pl.pallas_call(
        _final_kernel,
        grid=(n_tiles,),
        in_specs=[
            pl.BlockSpec((TM, d), lambda i: (tile0 + i, 0)),
            pl.BlockSpec((TM * ROW_CHUNKS, LANES), lambda i: (tile0 + i, 0)),
            pl.BlockSpec((None, 6, d), lambda i: (cond_of_tile(tile0 + i), 0, 0)),
        ],
        out_specs=pl.BlockSpec((TM, d), lambda i: (i, 0)),
        out_shape=jax.ShapeDtypeStruct((n_tiles * TM, d), F32),
        compiler_params=_params(1),
        name="final_residual",
    )(x, mo, mods)


def _moe_kernel(ea_ref, eb_ref, valid_ref, src_ref, dst_ref, h_hbm, g_ref, wga_ref, wua_ref, wda_ref,
                wgb_ref, wub_ref, wdb_ref, mo_hbm, xbuf, ybuf, sem_in, sem_out):
    i = pl.program_id(0)
    slot = i % 2
    nslot = 1 - slot
    is_valid = valid_ref[i] == 1
    prev_valid = jnp.logical_and(i >= 1, valid_ref[jnp.maximum(i - 1, 0)] == 1)
    prev2_valid = jnp.logical_and(i >= 2, valid_ref[jnp.maximum(i - 2, 0)] == 1)

    def gather_copy(tok, r, slot):
        return pltpu.make_async_copy(
            h_hbm.at[pl.ds(pl.multiple_of(tok * ROW_CHUNKS, ROW_CHUNKS), ROW_CHUNKS)],
            xbuf.at[slot, pl.ds(pl.multiple_of(r * BUF_PITCH, 8), ROW_CHUNKS)], sem_in.at[slot])

    def scatter_copy(tok, r, slot):
        return pltpu.make_async_copy(
            ybuf.at[slot, pl.ds(pl.multiple_of(r * ROW_CHUNKS, ROW_CHUNKS), ROW_CHUNKS)],
            mo_hbm.at[pl.ds(pl.multiple_of(tok * ROW_CHUNKS, ROW_CHUNKS), ROW_CHUNKS)], sem_out.at[slot])

    def for_rows(fn):
        def body(r, carry):
            fn(r)
            return carry
        lax.fori_loop(0, TM, body, 0, unroll=8)

    def gather_start(tile, slot):
        for_rows(lambda r: gather_copy(src_ref[tile * TM + r], r, slot).start())

    def gather_wait(slot):
        for_rows(lambda r: gather_copy(0, 0, slot).wait())

    def scatter_wait(slot):
        for_rows(lambda r: scatter_copy(0, 0, slot).wait())

    @pl.when(i == 0)
    def _():
        gather_start(0, 0)

    @pl.when(prev2_valid)
    def _():
        scatter_wait(slot)

    @pl.when(is_valid)
    def _():
        gather_wait(slot)
        for r in range(TM):
            gather_copy(src_ref[(i + 1) * TM + r], r, nslot).start()
        x = _load_token_major(xbuf.at[slot], TM, BUF_PITCH).astype(BF16)

        def ffn(wg_ref, wu_ref, wd_ref):
            g = jnp.dot(x, wg_ref[...], preferred_element_type=F32)
            u = jnp.dot(x, wu_ref[...], preferred_element_type=F32)
            a = (g * jax.nn.sigmoid(g)) * u
            return jnp.dot(a.astype(BF16), wd_ref[...], preferred_element_type=F32)

        gates = g_ref[...]
        y = gates[:, 0:1] * ffn(wga_ref, wua_ref, wda_ref) + gates[:, 1:2] * ffn(wgb_ref, wub_ref, wdb_ref)
        _store_token_major(ybuf.at[slot], y)
        for r in range(TM):
            scatter_copy(dst_ref[i * TM + r], r, slot).start()

    @pl.when(jnp.logical_not(is_valid))
    def _():
        @pl.when(prev_valid)
        def _():
            gather_wait(slot)

        ybuf[slot] = jnp.zeros(ybuf.shape[1:], F32)
        row0 = pl.multiple_of(dst_ref[i * TM] * ROW_CHUNKS, ROW_CHUNKS)
        fill = pltpu.make_async_copy(ybuf.at[slot], mo_hbm.at[pl.ds(row0, TM * ROW_CHUNKS)],
                                     sem_out.at[slot])
        fill.start()
        fill.wait()


def _moe_sorted(h, src_tok, dst_tok, gates, tile_ea, tile_eb, tile_valid, layer, wg, wu, wd):
    n_rows = src_tok.shape[0]
    d, ff = wg.shape[-2:]
    wa = lambda i, ea, eb, va, src, dst: (layer, ea[i], 0, 0)
    wb = lambda i, ea, eb, va, src, dst: (layer, eb[i], 0, 0)
    return pl.pallas_call(
        _moe_kernel,
        grid_spec=pltpu.PrefetchScalarGridSpec(
            num_scalar_prefetch=5,
            grid=(n_rows // TM,),
            in_specs=[
                pl.BlockSpec(memory_space=pl.ANY),
                pl.BlockSpec((TM, 2), lambda i, ea, eb, va, src, dst: (i, 0)),
                pl.BlockSpec((None, None, d, ff), wa),
                pl.BlockSpec((None, None, d, ff), wa),
                pl.BlockSpec((None, None, ff, d), wa),
                pl.BlockSpec((None, None, d, ff), wb),
                pl.BlockSpec((None, None, d, ff), wb),
                pl.BlockSpec((None, None, ff, d), wb),
            ],
            out_specs=pl.BlockSpec(memory_space=pl.ANY),
            scratch_shapes=[
                pltpu.VMEM((2, TM * BUF_PITCH, LANES), F32),
                pltpu.VMEM((2, TM * ROW_CHUNKS, LANES), F32),
                pltpu.SemaphoreType.DMA((2,)),
                pltpu.SemaphoreType.DMA((2,)),
            ],
        ),
        out_shape=jax.ShapeDtypeStruct((n_rows * ROW_CHUNKS, LANES), F32),
        compiler_params=_params(1),
        name="moe_experts",
    )(tile_ea, tile_eb, tile_valid, src_tok, dst_tok, h, gates, wg, wu, wd, wg, wu, wd)


def _invert_kernel(pos_ref, out_ref, fill_ref, sem):
    fill_ref[...] = jnp.full(fill_ref.shape, -1, I32)
    clear = pltpu.make_async_copy(fill_ref, out_ref, sem)
    clear.start()
    clear.wait()

    def place(k, carry):
        out_ref[pos_ref[k]] = k
        return carry

    lax.fori_loop(0, pos_ref.shape[0], place, 0, unroll=8)


def _invert_positions(pos, n_rows):
    padded = n_rows + (-n_rows) % SMEM_WORDS_TILE
    out = pl.pallas_call(
        _invert_kernel,
        grid_spec=pltpu.PrefetchScalarGridSpec(
            num_scalar_prefetch=1,
            grid=(1,),
            in_specs=[],
            out_specs=pl.BlockSpec(memory_space=pltpu.SMEM),
            scratch_shapes=[pltpu.VMEM((padded,), I32), pltpu.SemaphoreType.DMA(())],
        ),
        out_shape=jax.ShapeDtypeStruct((padded,), I32),
        compiler_params=_params(1),
        name="invert_positions",
    )(pos)
    return out[:n_rows]


def _moe_layer(h2p, route, layer, wg, wu, wd):
    t = h2p.shape[0] // ROW_CHUNKS
    n_tiles = t // TM + N_CLASSES + 2
    cls = route[:, 0, :].reshape(t).astype(I32)
    gate_rows = jnp.swapaxes(route[:, 1:3, :], 1, 2).reshape(t, 2)
    onehot = (cls[:, None] == jnp.arange(N_CLASSES, dtype=I32)[None, :]).astype(I32)
    csum = jnp.cumsum(onehot, axis=0)
    counts = csum[-1]
    tiles_c = (counts + TM - 1) // TM
    tile_end = jnp.cumsum(tiles_c)
    tile_start = tile_end - tiles_c
    pos = jnp.sum(onehot * (csum - 1 + (tile_start * TM)[None, :]), axis=1)
    n_rows = n_tiles * TM
    row_tok = _invert_positions(pos, n_rows)
    is_pad = row_tok < 0
    src_tok = jnp.where(is_pad, jnp.arange(n_rows, dtype=I32) % t, row_tok)
    dst_tok = jnp.where(is_pad, t + jnp.cumsum(is_pad.astype(I32)) - 1, row_tok)
    tile_ids = jnp.arange(n_tiles, dtype=I32)
    total = tile_end[-1]
    tile_valid = (tile_ids < total).astype(I32)
    tile_cls = jnp.sum((tile_end[None, :] <= jnp.minimum(tile_ids, total - 1)[:, None]).astype(I32), axis=1)
    pair = tile_cls % N_PAIRS
    group = tile_cls // N_PAIRS
    tile_ea = group * EXPERTS_PER_GROUP + jnp.asarray(PAIR_A, I32)[pair]
    tile_eb = group * EXPERTS_PER_GROUP + jnp.asarray(PAIR_B, I32)[pair]
    gates = gate_rows[src_tok]
    return _moe_sorted(h2p, src_tok, dst_tok, gates, tile_ea, tile_eb, tile_valid, layer, wg, wu, wd)


def _swap_rope_halves(w):
    shp = w.shape
    return w.reshape(shp[:-1] + (2, 2, AXIS_DIM // 2))[..., ::-1, :].reshape(shp)


def _rope_tables(length, norm_rope, scale):
    rows = length // GRID_W
    row = np.repeat(np.arange(rows, dtype=np.float32), GRID_W)
    col = np.tile(np.arange(GRID_W, dtype=np.float32), rows)
    inv = np.power(np.float32(ROPE_THETA),
                   -np.arange(0, AXIS_DIM, 2, dtype=np.float32) / np.float32(AXIS_DIM)).astype(np.float32)
    ang = np.stack([row[:, None] * inv, col[:, None] * inv], axis=1)
    cos, sin = np.cos(ang), np.sin(ang)
    c_full = jnp.asarray(np.stack([cos, cos], axis=2).reshape(length, ROPE_DIM))
    s_full = jnp.asarray(np.stack([-sin, sin], axis=2).reshape(length, ROPE_DIM))
    lat = jnp.concatenate([norm_rope * c_full, _swap_rope_halves(norm_rope) * s_full], axis=1)
    ctx = jnp.concatenate([norm_rope, jnp.zeros((ROPE_DIM,), F32)])
    ctx = jnp.broadcast_to(ctx[None, :], (TM, 2 * ROPE_DIM))
    return jnp.concatenate([ctx, lat], axis=0) * scale


def kernel(x_prompt, x_sample, cache_ckv, cache_krope, c, c_ctx, ada_w, ada_b, norm_mix_w, norm_ffn_w,
           mla_w_dq, mla_q_lora_norm, mla_w_uq, mla_w_dkv, mla_kv_lora_norm, mla_w_uk, mla_w_uv,
           mla_q_norm, mla_k_norm, mla_w_o, pool_w, pool_scale, router_w, router_bias,
           moe_w_gate, moe_w_up, moe_w_down):
    batch, seq, d = x_prompt.shape
    dec_batch, dec_seq, _ = x_sample.shape
    past = cache_ckv.shape[2]
    depth = ada_w.shape[0]
    assert seq == TM and past == TM and dec_seq % (2 * TM) == 0
    assert dec_batch + 1 <= 8 and depth == 2
    t_ctx = batch * seq
    t_lat = dec_batch * dec_seq
    n_ctx_tiles = t_ctx // TM
    lat_seq_tiles = dec_seq // TM
    ctx_seq_tiles = seq // TM

    def cond_of_tile(i):
        return jnp.where(i < n_ctx_tiles, 0, 1 + (i - n_ctx_tiles) // lat_seq_tiles)

    def table_of_tile(i):
        return jnp.where(i < n_ctx_tiles, 0, 1 + (i - n_ctx_tiles) % lat_seq_tiles)

    x_ctx = x_prompt.reshape(t_ctx, d)
    x_lat = x_sample.reshape(t_lat, d)
    conds = jnp.concatenate([c_ctx[None, :], c, jnp.zeros((7 - dec_batch, d), F32)], axis=0)
    mods = _adaln(conds, ada_w, ada_b).reshape(depth, 8, 6, d)

    rw = jnp.pad(router_w, ((0, 0), (0, LANES - N_EXPERTS)))
    rwh = rw.astype(BF16)
    rwl = (rw - rwh.astype(F32)).astype(BF16)
    rb = jnp.broadcast_to(router_bias.astype(F32)[:, None], (N_EXPERTS, TM))
    wg = moe_w_gate.astype(BF16)
    wu = moe_w_up.astype(BF16)
    wd = moe_w_down.astype(BF16)

    j = 0
    q_norm, k_norm = mla_q_norm[j], mla_k_norm[j]
    w_uqt = mla_w_uq[j].T.astype(BF16)
    w_dkv = mla_w_dkv[j]
    w_dkv = jnp.concatenate([w_dkv, _swap_rope_halves(w_dkv[:, KV_LORA:])], axis=1).astype(BF16)
    q_scale = ATTN_SCALE * math.log2(math.e)
    tq = _rope_tables(dec_seq, q_norm[QK_NOPE:], q_scale).T
    tk = _rope_tables(dec_seq, k_norm[QK_NOPE:], 1.0)
    qnn = jnp.broadcast_to((q_norm[:QK_NOPE] * q_scale)[:, None], (QK_NOPE, TM))
    knn = k_norm[:QK_NOPE][None, :]

    qt, ckv, kr = _mla_in(x_ctx, x_lat, mods[0], cond_of_tile, table_of_tile, norm_mix_w[0][None, :],
                          mla_w_dq[j].astype(BF16), mla_q_lora_norm[j][None, :], w_uqt, w_dkv,
                          mla_kv_lora_norm[j][None, :], qnn, tq)
    state_ckv = ckv[:t_ctx].reshape(batch, 1, seq, KV_LORA)
    state_krope = kr[:t_ctx, :ROPE_DIM].reshape(batch, 1, seq, ROPE_DIM)

    w_uk = mla_w_uk[j].astype(BF16)
    w_uvt = mla_w_uv[j].T.astype(BF16)
    k_ctx, vt_ctx = _kv_expand(ckv[:t_ctx], kr[:t_ctx], tk, lambda i: 0, w_uk, w_uvt, knn)
    kv_len = past + dec_seq
    ckv_lat = jnp.concatenate(
        [cache_ckv[:, j], ckv[t_ctx:].reshape(dec_batch, dec_seq, KV_LORA)], axis=1)
    kr_cache = jnp.pad(cache_krope[:, j], ((0, 0), (0, 0), (0, LANES - ROPE_DIM)))
    kr_lat = jnp.concatenate([kr_cache, kr[t_ctx:].reshape(dec_batch, dec_seq, LANES)], axis=1)
    kv_seq_tiles = kv_len // TM
    k_lat, vt_lat = _kv_expand(ckv_lat.reshape(-1, KV_LORA), kr_lat.reshape(-1, LANES), tk,
                               lambda i: i % kv_seq_tiles, w_uk, w_uvt, knn)

    logit_bound = q_scale * QK_HEAD * jnp.max(jnp.abs(q_norm)) * jnp.max(jnp.abs(k_norm))
    bounded = (logit_bound <= MAX_SAFE_LOGIT).astype(I32).reshape(1)
    attn_ctx = _attention(bounded, qt, k_ctx, vt_ctx, 0, batch, seq, seq, tq=seq, heads=N_HEADS, group=4)
    attn_lat = _attention(bounded, qt, k_lat, vt_lat, t_ctx, dec_batch, dec_seq, kv_len, tq=2 * TM,
                          heads=4, group=8)

    x1, h2p, route = _mla_out(attn_ctx, attn_lat, x_ctx, x_lat, mods[0], cond_of_tile,
                              mla_w_o[j].astype(BF16), norm_ffn_w[0][None, :], rwh, rwl, rb)
    mo = _moe_layer(h2p, route, 0, wg, wu, wd)

    x2, h2p, route = _pool_layer(x1, mo, mods[0], mods[1], cond_of_tile, n_ctx_tiles, ctx_seq_tiles,
                                 lat_seq_tiles, norm_mix_w[1][None, :], pool_w[0].astype(BF16),
                                 pool_scale[0][None, :], norm_ffn_w[1][None, :], rwh, rwl, rb)
    mo = _moe_layer(h2p, route, 1, wg, wu, wd)
    y_ctx = _final(x2, mo, mods[1], cond_of_tile, 0, n_ctx_tiles)
    y_lat = _final(x2, mo, mods[1], cond_of_tile, n_ctx_tiles, t_lat // TM)

    return (y_ctx.reshape(batch, seq, d), y_lat.reshape(dec_batch, dec_seq, d), state_ckv, state_krope)
```

```python
import functools
import math

import jax
import jax.numpy as jnp
import numpy as np
from jax import lax
from jax.experimental import pallas as pl
from jax.experimental.pallas import tpu as pltpu

F32 = jnp.float32
BF16 = jnp.bfloat16
I32 = jnp.int32

GRID_W = 64
N_HEADS = 16
QK_NOPE = 128
ROPE_DIM = 64
QK_HEAD = QK_NOPE + ROPE_DIM
V_HEAD = 128
KV_LORA = 256
ROPE_THETA = 10000.0
AXIS_DIM = ROPE_DIM // 2
ATTN_SCALE = QK_HEAD ** -0.5
POOL_WINDOWS = (2, 4, 8, 16)
N_EXPERTS = 16
N_GROUPS = 4
EXPERTS_PER_GROUP = N_EXPERTS // N_GROUPS
EPS = 1e-6

LANES = 128
SMEM_WORDS_TILE = 1024
HEAD_PAD = 2 * LANES
TM = 256
HALO = 16
N_PAIRS = 6
N_CLASSES = N_GROUPS * N_PAIRS
PAIR_A = (0, 0, 0, 1, 1, 3)
PAIR_B = (1, 2, 3, 3, 2, 2)
ROW_CHUNKS = 16
BUF_PITCH = 24
VMEM_LIMIT = 52 * 1024 * 1024
MAX_SAFE_LOGIT = 64.0


def _params(n_axes):
    return pltpu.CompilerParams(
        dimension_semantics=("arbitrary",) * n_axes, vmem_limit_bytes=VMEM_LIMIT)


def _rms(x):
    return x * lax.rsqrt(jnp.mean(x * x, axis=-1, keepdims=True) + EPS)


def _store_token_major(ref, x, pitch=ROW_CHUNKS):
    rows = x.shape[0]
    for c in range(ROW_CHUNKS):
        ref[pl.ds(c, rows, stride=pitch), :] = x[:, c * LANES:(c + 1) * LANES]


def _load_token_major(ref, rows, pitch=ROW_CHUNKS):
    return jnp.concatenate(
        [ref[pl.ds(c, rows, stride=pitch), :] for c in range(ROW_CHUNKS)], axis=1)


def _ada_kernel(cond_ref, w_ref, b_ref, o_ref):
    c = cond_ref[...]
    s = c * jax.nn.sigmoid(c)
    o_ref[...] = jnp.dot(s, w_ref[...], preferred_element_type=F32,
                         precision=lax.Precision.HIGHEST) + b_ref[...]


def _adaln(conds, ada_w, ada_b):
    depth, d, n = ada_w.shape
    tn = 1024
    return pl.pallas_call(
        _ada_kernel,
        grid=(depth, n // tn),
        in_specs=[
            pl.BlockSpec((8, d), lambda l, j: (0, 0)),
            pl.BlockSpec((None, d, tn), lambda l, j: (l, 0, j)),
            pl.BlockSpec((None, 1, tn), lambda l, j: (l, 0, j)),
        ],
        out_specs=pl.BlockSpec((None, 8, tn), lambda l, j: (l, 0, j)),
        out_shape=jax.ShapeDtypeStruct((depth, 8, n), F32),
        compiler_params=_params(2),
        name="adaln",
    )(conds, ada_w, ada_b.reshape(depth, 1, n))


def _mla_in_kernel(xc_ref, xl_ref, mod_ref, nw_ref, wdq_ref, qln_ref, wuqt_ref, wdkv_ref, kvn_ref,
                   qnn_ref, tq_ref, qt_ref, ckv_ref, kr_ref, *, n_ctx_tiles):
    x = jnp.where(pl.program_id(0) < n_ctx_tiles, xc_ref[...], xl_ref[...])
    h = _rms(x) * (nw_ref[...] * (1.0 + mod_ref[1:2, :])) + mod_ref[0:1, :]
    hb = h.astype(BF16)
    kv = jnp.dot(hb, wdkv_ref[...], preferred_element_type=F32)
    ckv_ref[...] = _rms(kv[:, :KV_LORA]) * kvn_ref[...]
    kr_ref[...] = kv[:, KV_LORA:]
    cq = jnp.dot(hb, wdq_ref[...], preferred_element_type=F32)
    cqn = (_rms(cq) * qln_ref[...]).astype(BF16)
    qt = lax.dot_general(wuqt_ref[...], cqn, (((1,), (1,)), ((), ())),
                         preferred_element_type=F32)
    t_same = tq_ref[:ROPE_DIM, :]
    t_swap = tq_ref[ROPE_DIM:, :]
    qnn = qnn_ref[...]
    half = AXIS_DIM // 2
    for hd in range(N_HEADS):
        a = qt[hd * QK_HEAD: hd * QK_HEAD + QK_NOPE, :]
        b = qt[hd * QK_HEAD + QK_NOPE: (hd + 1) * QK_HEAD, :]
        ss = jnp.sum(a * a, axis=0, keepdims=True) + jnp.sum(b * b, axis=0, keepdims=True)
        r = lax.rsqrt(ss * (1.0 / QK_HEAD) + EPS)
        b_swap = jnp.concatenate([b[half:2 * half], b[:half], b[3 * half:], b[2 * half:3 * half]], axis=0)
        rot = ((b * t_same + b_swap * t_swap) * r).astype(BF16)
        qt_ref[hd * HEAD_PAD: hd * HEAD_PAD + LANES, :] = (a * r * qnn).astype(BF16)
        qt_ref[hd * HEAD_PAD + LANES: hd * HEAD_PAD + LANES + ROPE_DIM, :] = rot
        qt_ref[hd * HEAD_PAD + LANES + ROPE_DIM: (hd + 1) * HEAD_PAD, :] = rot


def _ctx_lat_specs(n_ctx_tiles, width):
    return [pl.BlockSpec((TM, width), lambda i: (jnp.minimum(i, n_ctx_tiles - 1), 0)),
            pl.BlockSpec((TM, width), lambda i: (jnp.maximum(i - n_ctx_tiles, 0), 0))]


def _mla_in(x_ctx, x_lat, mods, cond_of_tile, table_of_tile, nw, wdq, qln, wuqt, wdkv, kvn, qnn, tq):
    d = x_ctx.shape[1]
    n_ctx_tiles = x_ctx.shape[0] // TM
    t = x_ctx.shape[0] + x_lat.shape[0]
    nt = t // TM
    const = lambda i: (0, 0)
    return pl.pallas_call(
        functools.partial(_mla_in_kernel, n_ctx_tiles=n_ctx_tiles),
        grid=(nt,),
        in_specs=_ctx_lat_specs(n_ctx_tiles, d) + [
            pl.BlockSpec((None, 6, d), lambda i: (cond_of_tile(i), 0, 0)),
            pl.BlockSpec((1, d), const),
            pl.BlockSpec(wdq.shape, const),
            pl.BlockSpec((1, wdq.shape[1]), const),
            pl.BlockSpec(wuqt.shape, const),
            pl.BlockSpec(wdkv.shape, const),
            pl.BlockSpec((1, KV_LORA), const),
            pl.BlockSpec((LANES, TM), const),
            pl.BlockSpec((LANES, TM), lambda i: (0, table_of_tile(i))),
        ],
        out_specs=[
            pl.BlockSpec((N_HEADS * HEAD_PAD, TM), lambda i: (0, i)),
            pl.BlockSpec((TM, KV_LORA), lambda i: (i, 0)),
            pl.BlockSpec((TM, LANES), lambda i: (i, 0)),
        ],
        out_shape=[
            jax.ShapeDtypeStruct((N_HEADS * HEAD_PAD, t), BF16),
            jax.ShapeDtypeStruct((t, KV_LORA), F32),
            jax.ShapeDtypeStruct((t, LANES), F32),
        ],
        compiler_params=_params(1),
        name="mla_in",
    )(x_ctx, x_lat, mods, nw, wdq, qln, wuqt, wdkv, kvn, qnn, tq)


def _kv_expand_kernel(ckv_ref, kr_ref, tk_ref, wuk_ref, wuvt_ref, knn_ref, k_ref, vt_ref):
    cb = ckv_ref[...].astype(BF16)
    kn = jnp.dot(cb, wuk_ref[...], preferred_element_type=F32)
    vt_ref[...] = lax.dot_general(wuvt_ref[...], cb, (((1,), (1,)), ((), ())),
                                  preferred_element_type=F32).astype(BF16)
    kr = kr_ref[...]
    rope_lane = lax.broadcasted_iota(I32, (1, LANES), 1) < ROPE_DIM
    ssr = jnp.sum(jnp.where(rope_lane, kr * kr, 0.0), axis=-1, keepdims=True)
    bk = kr * tk_ref[...]
    rot = jnp.where(rope_lane, bk + pltpu.roll(bk, ROPE_DIM, axis=1), 0.0)
    knn = knn_ref[...]
    for hd in range(N_HEADS):
        a = kn[:, hd * LANES: (hd + 1) * LANES]
        r = lax.rsqrt((jnp.sum(a * a, axis=-1, keepdims=True) + ssr) * (1.0 / QK_HEAD) + EPS)
        k_ref[:, hd * HEAD_PAD: hd * HEAD_PAD + LANES] = (a * r * knn).astype(BF16)
        k_ref[:, hd * HEAD_PAD + LANES: (hd + 1) * HEAD_PAD] = (rot * r).astype(BF16)


def _kv_expand(ckv, kr, tk, table_of_tile, wuk, wuvt, knn):
    t = ckv.shape[0]
    const = lambda i: (0, 0)
    return pl.pallas_call(
        _kv_expand_kernel,
        grid=(t // TM,),
        in_specs=[
            pl.BlockSpec((TM, KV_LORA), lambda i: (i, 0)),
            pl.BlockSpec((TM, LANES), lambda i: (i, 0)),
            pl.BlockSpec((TM, LANES), lambda i: (table_of_tile(i), 0)),
            pl.BlockSpec(wuk.shape, const),
            pl.BlockSpec(wuvt.shape, const),
            pl.BlockSpec((1, LANES), const),
        ],
        out_specs=[
            pl.BlockSpec((TM, N_HEADS * HEAD_PAD), lambda i: (i, 0)),
            pl.BlockSpec((None, N_HEADS * V_HEAD, TM), lambda i: (i, 0, 0)),
        ],
        out_shape=[
            jax.ShapeDtypeStruct((t, N_HEADS * HEAD_PAD), BF16),
            jax.ShapeDtypeStruct((t // TM, N_HEADS * V_HEAD, TM), BF16),
        ],
        compiler_params=_params(1),
        name="kv_expand",
    )(ckv, kr, tk, wuk, wuvt, knn)


def _attn_kernel(bounded_ref, qt_ref, k_ref, vt_ref, o_ref, *, heads, nk, group):
    tq = qt_ref.shape[1]
    chains = [(hh, sub) for hh in range(heads) for sub in range(tq // TM)]

    def load_q(part):
        return [qt_ref[hh * HEAD_PAD:(hh + 1) * HEAD_PAD, sub * TM:(sub + 1) * TM] for hh, sub in part]

    def store(part, ls, accs):
        for (hh, sub), l, acc in zip(part, ls, accs):
            o_ref[sub * TM:(sub + 1) * TM, hh * V_HEAD:(hh + 1) * V_HEAD] = (acc / l).T.astype(BF16)

    @pl.when(bounded_ref[0] == 1)
    def _():
        for c0 in range(0, len(chains), group):
            part = chains[c0:c0 + group]
            qts = load_q(part)
            lps = [jnp.zeros((8, TM), F32) for _ in part]
            accs = [None for _ in part]
            p_prev = None
            for j in range(nk + 1):
                p_cur = []
                if j < nk:
                    for n, ((hh, _), qt) in enumerate(zip(part, qts)):
                        ks = k_ref[j * TM:(j + 1) * TM, hh * HEAD_PAD:(hh + 1) * HEAD_PAD]
                        p = jnp.exp2(jnp.dot(ks, qt, preferred_element_type=F32))
                        lps[n] = lps[n] + jnp.sum(p.reshape(TM // 8, 8, TM), axis=0)
                        p_cur.append(p.astype(BF16))
                if j > 0:
                    for n, (hh, _) in enumerate(part):
                        vt = vt_ref[j - 1, hh * V_HEAD:(hh + 1) * V_HEAD, :]
                        pv = jnp.dot(vt, p_prev[n], preferred_element_type=F32)
                        accs[n] = pv if accs[n] is None else accs[n] + pv
                p_prev = p_cur
            store(part, [jnp.sum(lp, axis=0, keepdims=True) for lp in lps], accs)

    @pl.when(bounded_ref[0] == 0)
    def _():
        for chain in chains:
            hh = chain[0]
            qt, = load_q([chain])

            def body(j, carry, hh=hh, qt=qt):
                m, l, acc = carry
                start = pl.multiple_of(j * TM, TM)
                ks = k_ref[pl.ds(start, TM), hh * HEAD_PAD:(hh + 1) * HEAD_PAD]
                s = jnp.dot(ks, qt, preferred_element_type=F32)
                mn = jnp.maximum(m, jnp.max(s, axis=0, keepdims=True))
                alpha = jnp.exp2(m - mn)
                p = jnp.exp2(s - mn)
                l = alpha * l + jnp.sum(p, axis=0, keepdims=True)
                vt = vt_ref[j, hh * V_HEAD:(hh + 1) * V_HEAD, :]
                acc = alpha * acc + jnp.dot(vt, p.astype(BF16), preferred_element_type=F32)
                return mn, l, acc

            init = (jnp.full((1, TM), -jnp.inf, F32), jnp.zeros((1, TM), F32),
                    jnp.zeros((V_HEAD, TM), F32))
            _, l, acc = lax.fori_loop(0, nk, body, init)
            store([chain], [l], [acc])


def _attention(bounded, qt, k, vt, q_col0, n_batch, q_len, kv_len, tq, heads, group):
    nq = q_len // tq
    hb = N_HEADS // heads
    qb0 = q_col0 // tq
    nk = kv_len // TM
    return pl.pallas_call(
        functools.partial(_attn_kernel, heads=heads, nk=nk, group=group),
        grid_spec=pltpu.PrefetchScalarGridSpec(
            num_scalar_prefetch=1,
            grid=(n_batch, hb, nq),
            in_specs=[
                pl.BlockSpec((heads * HEAD_PAD, tq), lambda b, h, i, f: (h, qb0 + b * nq + i)),
                pl.BlockSpec((kv_len, heads * HEAD_PAD), lambda b, h, i, f: (b, h)),
                pl.BlockSpec((nk, heads * V_HEAD, TM), lambda b, h, i, f: (b, h, 0)),
            ],
            out_specs=pl.BlockSpec((tq, heads * V_HEAD), lambda b, h, i, f: (b * nq + i, h)),
        ),
        out_shape=jax.ShapeDtypeStruct((n_batch * q_len, N_HEADS * V_HEAD), BF16),
        compiler_params=_params(3),
        name="attention",
    )(bounded, qt, k, vt)


def _route(h2, rwh_ref, rwl_ref, rb_ref, route_ref):
    hi = h2.astype(BF16)
    lo = (h2 - hi.astype(F32)).astype(BF16)
    rwh = rwh_ref[...]
    logits = (jnp.dot(hi, rwh, preferred_element_type=F32)
              + jnp.dot(lo, rwh, preferred_element_type=F32)
              + jnp.dot(hi, rwl_ref[...], preferred_element_type=F32))
    lt = logits.T[:N_EXPERTS, :]
    scores = jax.nn.sigmoid(lt)
    sel = scores + rb_ref[...]
    srow = [sel[e:e + 1, :] for e in range(N_EXPERTS)]
    prow = [scores[e:e + 1, :] for e in range(N_EXPERTS)]

    def top2_sum(a, b, c, d):
        hab, lab = jnp.maximum(a, b), jnp.minimum(a, b)
        hcd, lcd = jnp.maximum(c, d), jnp.minimum(c, d)
        return jnp.maximum(hab, hcd) + jnp.maximum(jnp.minimum(hab, hcd), jnp.maximum(lab, lcd))

    gs = [top2_sum(*srow[4 * g:4 * g + 4]) for g in range(N_GROUPS)]
    best = jnp.zeros_like(gs[0], dtype=I32)
    bestv = gs[0]
    for g in range(1, N_GROUPS):
        upd = gs[g] > bestv
        best = jnp.where(upd, g, best)
        bestv = jnp.where(upd, gs[g], bestv)

    def pick(rows, j):
        out = rows[j]
        for g in range(1, N_GROUPS):
            out = jnp.where(best == g, rows[4 * g + j], out)
        return out

    sv = [pick(srow, j) for j in range(EXPERTS_PER_GROUP)]
    pv = [pick(prow, j) for j in range(EXPERTS_PER_GROUP)]
    i1 = jnp.zeros_like(best)
    v1 = sv[0]
    for j in range(1, EXPERTS_PER_GROUP):
        upd = sv[j] > v1
        i1 = jnp.where(upd, j, i1)
        v1 = jnp.where(upd, sv[j], v1)
    neg = jnp.float32(-jnp.inf)
    i2 = jnp.where(i1 == 0, 1, 0).astype(I32)
    v2 = jnp.where(i1 == 0, sv[1], sv[0])
    for j in range(1, EXPERTS_PER_GROUP):
        cand = jnp.where(i1 == j, neg, sv[j])
        upd = cand > v2
        i2 = jnp.where(upd, j, i2)
        v2 = jnp.where(upd, cand, v2)
    ilo = jnp.minimum(i1, i2)
    ihi = jnp.maximum(i1, i2)
    pair = jnp.where(ilo == 0, ihi - 1, jnp.where(ilo == 1, 6 - ihi, N_PAIRS - 1))
    cls = best * N_PAIRS + pair

    def take(vals, idx):
        out = vals[0]
        for j in range(1, EXPERTS_PER_GROUP):
            out = jnp.where(idx == j, vals[j], out)
        return out

    wlo = take(pv, ilo)
    whi = take(pv, ihi)
    den = wlo + whi
    a_is_hi = pair == N_PAIRS - 1
    gate_a = jnp.where(a_is_hi, whi, wlo) / den
    gate_b = jnp.where(a_is_hi, wlo, whi) / den
    route_ref[...] = jnp.concatenate(
        [cls.astype(F32), gate_a, gate_b, jnp.zeros((5, cls.shape[1]), F32)], axis=0)


def _finish_sublayer(x_new, mod_ref, nw2_ref, rwh_ref, rwl_ref, rb_ref, x_out_ref, h2p_ref, route_ref):
    x_out_ref[...] = x_new
    h2 = _rms(x_new) * (nw2_ref[...] * (1.0 + mod_ref[4:5, :])) + mod_ref[3:4, :]
    _store_token_major(h2p_ref, h2)
    _route(h2, rwh_ref, rwl_ref, rb_ref, route_ref)


def _mla_out_kernel(attn_c_ref, attn_l_ref, xc_ref, xl_ref, mod_ref, wo_ref, nw2_ref, rwh_ref, rwl_ref,
                    rb_ref, x1_ref, h2p_ref, route_ref, *, n_ctx_tiles):
    is_ctx = pl.program_id(0) < n_ctx_tiles
    attn = jnp.where(is_ctx, attn_c_ref[...], attn_l_ref[...])
    mix = jnp.dot(attn, wo_ref[...], preferred_element_type=F32)
    x1 = jnp.where(is_ctx, xc_ref[...], xl_ref[...]) + mod_ref[2:3, :] * mix
    _finish_sublayer(x1, mod_ref, nw2_ref, rwh_ref, rwl_ref, rb_ref, x1_ref, h2p_ref, route_ref)


def _sublayer_out_specs(t, d):
    nt = t // TM
    specs = [
        pl.BlockSpec((TM, d), lambda i: (i, 0)),
        pl.BlockSpec((TM * ROW_CHUNKS, LANES), lambda i: (i, 0)),
        pl.BlockSpec((None, 8, TM), lambda i: (i, 0, 0)),
    ]
    shapes = [
        jax.ShapeDtypeStruct((t, d), F32),
        jax.ShapeDtypeStruct((t * ROW_CHUNKS, LANES), F32),
        jax.ShapeDtypeStruct((nt, 8, TM), F32),
    ]
    return specs, shapes


def _mla_out(attn_ctx, attn_lat, x_ctx, x_lat, mods, cond_of_tile, wo, nw2, rwh, rwl, rb):
    d = x_ctx.shape[1]
    t = x_ctx.shape[0] + x_lat.shape[0]
    assert d == ROW_CHUNKS * LANES
    const = lambda i: (0, 0)
    n_ctx_tiles = attn_ctx.shape[0] // TM
    out_specs, out_shape = _sublayer_out_specs(t, d)
    return pl.pallas_call(
        functools.partial(_mla_out_kernel, n_ctx_tiles=n_ctx_tiles),
        grid=(t // TM,),
        in_specs=_ctx_lat_specs(n_ctx_tiles, attn_ctx.shape[1]) + _ctx_lat_specs(n_ctx_tiles, d) + [
            pl.BlockSpec((None, 6, d), lambda i: (cond_of_tile(i), 0, 0)),
            pl.BlockSpec(wo.shape, const),
            pl.BlockSpec((1, d), const),
            pl.BlockSpec(rwh.shape, const),
            pl.BlockSpec(rwl.shape, const),
            pl.BlockSpec(rb.shape, const),
        ],
        out_specs=out_specs,
        out_shape=out_shape,
        compiler_params=_params(1),
        name="mla_out",
    )(attn_ctx, attn_lat, x_ctx, x_lat, mods, wo, nw2, rwh, rwl, rb)


def _pool_bands():
    t = jnp.arange(TM, dtype=I32)[:, None]
    e = jnp.arange(TM, dtype=I32)[None, :]
    c = jnp.arange(LANES, dtype=I32)[None, :]
    pos_hal = jnp.where(c < HALO, c - HALO, jnp.where(c < 2 * HALO, TM + c - HALO, 4 * TM))
    mids, hals = [], []
    for w in POOL_WINDOWS:
        half = w // 2
        mids.append(((e - t >= -half) & (e - t <= half - 1)).astype(BF16))
        hals.append(((pos_hal - t >= -half) & (pos_hal - t <= half - 1)).astype(BF16))
    return jnp.stack(mids), jnp.stack(hals)


def _pool_kernel(x_ref, mo_ref, xp_ref, mop_ref, xn_ref, mon_ref, modp_ref, mod_ref, nw_ref, pw_ref,
                 ps_ref, nw2_ref, rwh_ref, rwl_ref, rb_ref, bmid_ref, bhal_ref, x2_ref, h2p_ref, route_ref,
                 *, n_ctx_tiles, ctx_seq_tiles, lat_seq_tiles):
    i = pl.program_id(0)
    is_lat = i >= n_ctx_tiles
    seq_tiles = jnp.where(is_lat, lat_seq_tiles, ctx_seq_tiles)
    in_seq = jnp.where(is_lat, i - n_ctx_tiles, i) % seq_tiles
    has_prev = in_seq > 0
    has_next = in_seq < seq_tiles - 1
    g2p = modp_ref[5:6, :]
    gain1 = nw_ref[...] * (1.0 + mod_ref[1:2, :])
    sh1 = mod_ref[0:1, :]

    def pre(xv, mo):
        xx = xv + g2p * mo
        return xx, _rms(xx) * gain1 + sh1

    xcur, h = pre(x_ref[...], _load_token_major(mo_ref, TM))
    _, hprev = pre(xp_ref[...], _load_token_major(mop_ref, HALO))
    _, hnext = pre(xn_ref[...], _load_token_major(mon_ref, HALO))
    d = h.shape[1]
    gw = d // len(POOL_WINDOWS)
    hb = h.astype(BF16)
    hprev = jnp.where(has_prev, hprev, 0.0)
    hnext = jnp.where(has_next, hnext, 0.0)
    halo = jnp.concatenate(
        [hprev.astype(BF16), hnext.astype(BF16), jnp.zeros((LANES - 2 * HALO, d), BF16)], axis=0)
    t_col = lax.broadcasted_iota(I32, (TM, 1), 0)
    lo_bound = jnp.where(has_prev, -HALO, 0)
    hi_bound = jnp.where(has_next, TM - 1 + HALO, TM - 1)

    ys = []
    for g, w in enumerate(POOL_WINDOWS):
        half = w // 2
        sl = slice(g * gw, (g + 1) * gw)
        wsum = (jnp.dot(bmid_ref[g], hb[:, sl], preferred_element_type=F32)
                + jnp.dot(bhal_ref[g], halo[:, sl], preferred_element_type=F32))
        cnt = (jnp.minimum(t_col + (half - 1), hi_bound)
               - jnp.maximum(t_col - half, lo_bound) + 1).astype(F32)
        pooled = wsum / cnt - h[:, sl]
        ys.append(jnp.dot(pooled.astype(BF16), pw_ref[g], preferred_element_type=F32))
    y = jnp.concatenate(ys, axis=1) * ps_ref[...]
    x2 = xcur + mod_ref[2:3, :] * y
    _finish_sublayer(x2, mod_ref, nw2_ref, rwh_ref, rwl_ref, rb_ref, x2_ref, h2p_ref, route_ref)


def _pool_layer(x1, mo, mods_prev, mods, cond_of_tile, n_ctx_tiles, ctx_seq_tiles, lat_seq_tiles,
                nw, pw, ps, nw2, rwh, rwl, rb):
    t, d = x1.shape
    const = lambda i: (0, 0)
    hb = TM // HALO
    last = t // HALO - 1
    prev_map = lambda i: (jnp.maximum(i * hb - 1, 0), 0)
    next_map = lambda i: (jnp.minimum((i + 1) * hb, last), 0)
    band_mid, band_hal = _pool_bands()
    out_specs, out_shape = _sublayer_out_specs(t, d)
    return pl.pallas_call(
        functools.partial(_pool_kernel, n_ctx_tiles=n_ctx_tiles, ctx_seq_tiles=ctx_seq_tiles,
                          lat_seq_tiles=lat_seq_tiles),
        grid=(t // TM,),
        in_specs=[
            pl.BlockSpec((TM, d), lambda i: (i, 0)),
            pl.BlockSpec((TM * ROW_CHUNKS, LANES), lambda i: (i, 0)),
            pl.BlockSpec((HALO, d), prev_map),
            pl.BlockSpec((HALO * ROW_CHUNKS, LANES), prev_map),
            pl.BlockSpec((HALO, d), next_map),
            pl.BlockSpec((HALO * ROW_CHUNKS, LANES), next_map),
            pl.BlockSpec((None, 6, d), lambda i: (cond_of_tile(i), 0, 0)),
            pl.BlockSpec((None, 6, d), lambda i: (cond_of_tile(i), 0, 0)),
            pl.BlockSpec((1, d), const),
            pl.BlockSpec(pw.shape, lambda i: (0, 0, 0)),
            pl.BlockSpec((1, d), const),
            pl.BlockSpec((1, d), const),
            pl.BlockSpec(rwh.shape, const),
            pl.BlockSpec(rwl.shape, const),
            pl.BlockSpec(rb.shape, const),
            pl.BlockSpec(band_mid.shape, lambda i: (0, 0, 0)),
            pl.BlockSpec(band_hal.shape, lambda i: (0, 0, 0)),
        ],
        out_specs=out_specs,
        out_shape=out_shape,
        compiler_params=_params(1),
        name="pool_layer",
    )(x1, mo, x1, mo, x1, mo, mods_prev, mods, nw, pw, ps, nw2, rwh, rwl, rb, band_mid, band_hal)


def _final_kernel(x_ref, mo_ref, mod_ref, o_ref):
    o_ref[...] = x_ref[...] + mod_ref[5:6, :] * _load_token_major(mo_ref, x_ref.shape[0])


def _final(x, mo, mods, cond_of_tile, tile0, n_tiles):
    d = x.shape[1]
    per_step = 2
    assert tile0 % per_step == 0 and n_tiles % per_step == 0
    rows = per_step * TM
    blk0 = tile0 // per_step
    return pl.pallas_call(
        _final_kernel,
        grid=(n_tiles // per_step,),
        in_specs=[
            pl.BlockSpec((rows, d), lambda i: (blk0 + i, 0)),
            pl.BlockSpec((rows * ROW_CHUNKS, LANES), lambda i: (blk0 + i, 0)),
            pl.BlockSpec((None, 6, d), lambda i: (cond_of_tile(tile0 + per_step * i), 0, 0)),
        ],
        out_specs=pl.BlockSpec((rows, d), lambda i: (i, 0)),
        out_shape=jax.ShapeDtypeStruct((n_tiles * TM, d), F32),
        compiler_params=_params(1),
        name="final_residual",
    )(x, mo, mods)


def _moe_kernel(ea_ref, eb_ref, valid_ref, src_ref, dst_ref, h_hbm, g_ref, wga_ref, wua_ref, wda_ref,
                wgb_ref, wub_ref, wdb_ref, mo_hbm, xbuf, ybuf, sem_in, sem_out):
    i = pl.program_id(0)
    slot = i % 2
    nslot = 1 - slot
    is_valid = valid_ref[i] == 1
    prev_valid = jnp.logical_and(i >= 1, valid_ref[jnp.maximum(i - 1, 0)] == 1)
    prev2_valid = jnp.logical_and(i >= 2, valid_ref[jnp.maximum(i - 2, 0)] == 1)

    def gather_copy(tok, r, slot):
        return pltpu.make_async_copy(
            h_hbm.at[pl.ds(pl.multiple_of(tok * ROW_CHUNKS, ROW_CHUNKS), ROW_CHUNKS)],
            xbuf.at[slot, pl.ds(pl.multiple_of(r * BUF_PITCH, 8), ROW_CHUNKS)], sem_in.at[slot])

    def scatter_copy(tok, r, slot):
        return pltpu.make_async_copy(
            ybuf.at[slot, pl.ds(pl.multiple_of(r * ROW_CHUNKS, ROW_CHUNKS), ROW_CHUNKS)],
            mo_hbm.at[pl.ds(pl.multiple_of(tok * ROW_CHUNKS, ROW_CHUNKS), ROW_CHUNKS)], sem_out.at[slot])

    def for_rows(fn):
        def body(r, carry):
            fn(r)
            return carry
        lax.fori_loop(0, TM, body, 0, unroll=8)

    def gather_start(tile, slot):
        for_rows(lambda r: gather_copy(src_ref[tile * TM + r], r, slot).start())

    def gather_wait(slot):
        for_rows(lambda r: gather_copy(0, 0, slot).wait())

    def scatter_wait(slot):
        for_rows(lambda r: scatter_copy(0, 0, slot).wait())

    @pl.when(i == 0)
    def _():
        gather_start(0, 0)

    @pl.when(prev2_valid)
    def _():
        scatter_wait(slot)

    @pl.when(is_valid)
    def _():
        gather_wait(slot)
        for r in range(TM):
            gather_copy(src_ref[(i + 1) * TM + r], r, nslot).start()
        x = _load_token_major(xbuf.at[slot], TM, BUF_PITCH).astype(BF16)

        def ffn(wg_ref, wu_ref, wd_ref):
            g = jnp.dot(x, wg_ref[...], preferred_element_type=F32)
            u = jnp.dot(x, wu_ref[...], preferred_element_type=F32)
            a = (g * jax.nn.sigmoid(g)) * u
            return jnp.dot(a.astype(BF16), wd_ref[...], preferred_element_type=F32)

        gates = g_ref[...]
        y = gates[:, 0:1] * ffn(wga_ref, wua_ref, wda_ref) + gates[:, 1:2] * ffn(wgb_ref, wub_ref, wdb_ref)
        _store_token_major(ybuf.at[slot], y)
        for r in range(TM):
            scatter_copy(dst_ref[i * TM + r], r, slot).start()

    @pl.when(jnp.logical_not(is_valid))
    def _():
        @pl.when(prev_valid)
        def _():
            gather_wait(slot)

        ybuf[slot] = jnp.zeros(ybuf.shape[1:], F32)
        row0 = pl.multiple_of(dst_ref[i * TM] * ROW_CHUNKS, ROW_CHUNKS)
        fill = pltpu.make_async_copy(ybuf.at[slot], mo_hbm.at[pl.ds(row0, TM * ROW_CHUNKS)],
                                     sem_out.at[slot])
        fill.start()
        fill.wait()


def _moe_sorted(h, src_tok, dst_tok, gates, tile_ea, tile_eb, tile_valid, layer, wg, wu, wd):
    n_rows = src_tok.shape[0]
    d, ff = wg.shape[-2:]
    wa = lambda i, ea, eb, va, src, dst: (layer, ea[i], 0, 0)
    wb = lambda i, ea, eb, va, src, dst: (layer, eb[i], 0, 0)
    return pl.pallas_call(
        _moe_kernel,
        grid_spec=pltpu.PrefetchScalarGridSpec(
            num_scalar_prefetch=5,
            grid=(n_rows // TM,),
            in_specs=[
                pl.BlockSpec(memory_space=pl.ANY),
                pl.BlockSpec((TM, 2), lambda i, ea, eb, va, src, dst: (i, 0)),
                pl.BlockSpec((None, None, d, ff), wa),
                pl.BlockSpec((None, None, d, ff), wa),
                pl.BlockSpec((None, None, ff, d), wa),
                pl.BlockSpec((None, None, d, ff), wb),
                pl.BlockSpec((None, None, d, ff), wb),
                pl.BlockSpec((None, None, ff, d), wb),
            ],
            out_specs=pl.BlockSpec(memory_space=pl.ANY),
            scratch_shapes=[
                pltpu.VMEM((2, TM * BUF_PITCH, LANES), F32),
                pltpu.VMEM((2, TM * ROW_CHUNKS, LANES), F32),
                pltpu.SemaphoreType.DMA((2,)),
                pltpu.SemaphoreType.DMA((2,)),
            ],
        ),
        out_shape=jax.ShapeDtypeStruct((n_rows * ROW_CHUNKS, LANES), F32),
        compiler_params=_params(1),
        name="moe_experts",
    )(tile_ea, tile_eb, tile_valid, src_tok, dst_tok, h, gates, wg, wu, wd, wg, wu, wd)


def _invert_kernel(pos_ref, out_ref, fill_ref, sem):
    fill_ref[...] = jnp.full(fill_ref.shape, -1, I32)
    clear = pltpu.make_async_copy(fill_ref, out_ref, sem)
    clear.start()
    clear.wait()

    def place(k, carry):
        out_ref[pos_ref[k]] = k
        return carry

    lax.fori_loop(0, pos_ref.shape[0], place, 0, unroll=8)


def _invert_positions(pos, n_rows):
    padded = n_rows + (-n_rows) % SMEM_WORDS_TILE
    out = pl.pallas_call(
        _invert_kernel,
        grid_spec=pltpu.PrefetchScalarGridSpec(
            num_scalar_prefetch=1,
            grid=(1,),
            in_specs=[],
            out_specs=pl.BlockSpec(memory_space=pltpu.SMEM),
            scratch_shapes=[pltpu.VMEM((padded,), I32), pltpu.SemaphoreType.DMA(())],
        ),
        out_shape=jax.ShapeDtypeStruct((padded,), I32),
        compiler_params=_params(1),
        name="invert_positions",
    )(pos)
    return out[:n_rows]


def _moe_layer(h2p, route, layer, wg, wu, wd):
    t = h2p.shape[0] // ROW_CHUNKS
    n_tiles = t // TM + N_CLASSES + 2
    cls = route[:, 0, :].reshape(t).astype(I32)
    gate_rows = jnp.swapaxes(route[:, 1:3, :], 1, 2).reshape(t, 2)
    onehot = (cls[:, None] == jnp.arange(N_CLASSES, dtype=I32)[None, :]).astype(I32)
    csum = jnp.cumsum(onehot, axis=0)
    counts = csum[-1]
    tiles_c = (counts + TM - 1) // TM
    tile_end = jnp.cumsum(tiles_c)
    tile_start = tile_end - tiles_c
    pos = jnp.sum(onehot * (csum - 1 + (tile_start * TM)[None, :]), axis=1)
    n_rows = n_tiles * TM
    row_tok = _invert_positions(pos, n_rows)
    is_pad = row_tok < 0
    src_tok = jnp.where(is_pad, jnp.arange(n_rows, dtype=I32) % t, row_tok)
    dst_tok = jnp.where(is_pad, t + jnp.cumsum(is_pad.astype(I32)) - 1, row_tok)
    tile_ids = jnp.arange(n_tiles, dtype=I32)
    total = tile_end[-1]
    tile_valid = (tile_ids < total).astype(I32)
    tile_cls = jnp.sum((tile_end[None, :] <= jnp.minimum(tile_ids, total - 1)[:, None]).astype(I32), axis=1)
    pair = tile_cls % N_PAIRS
    group = tile_cls // N_PAIRS
    tile_ea = group * EXPERTS_PER_GROUP + jnp.asarray(PAIR_A, I32)[pair]
    tile_eb = group * EXPERTS_PER_GROUP + jnp.asarray(PAIR_B, I32)[pair]
    gates = gate_rows[src_tok]
    return _moe_sorted(h2p, src_tok, dst_tok, gates, tile_ea, tile_eb, tile_valid, layer, wg, wu, wd)


def _swap_rope_halves(w):
    shp = w.shape
    return w.reshape(shp[:-1] + (2, 2, AXIS_DIM // 2))[..., ::-1, :].reshape(shp)


def _rope_tables(length, norm_rope, scale):
    rows = length // GRID_W
    row = np.repeat(np.arange(rows, dtype=np.float32), GRID_W)
    col = np.tile(np.arange(GRID_W, dtype=np.float32), rows)
    inv = np.power(np.float32(ROPE_THETA),
                   -np.arange(0, AXIS_DIM, 2, dtype=np.float32) / np.float32(AXIS_DIM)).astype(np.float32)
    ang = np.stack([row[:, None] * inv, col[:, None] * inv], axis=1)
    cos, sin = np.cos(ang), np.sin(ang)
    c_full = jnp.asarray(np.stack([cos, cos], axis=2).reshape(length, ROPE_DIM))
    s_full = jnp.asarray(np.stack([-sin, sin], axis=2).reshape(length, ROPE_DIM))
    lat = jnp.concatenate([norm_rope * c_full, _swap_rope_halves(norm_rope) * s_full], axis=1)
    ctx = jnp.concatenate([norm_rope, jnp.zeros((ROPE_DIM,), F32)])
    ctx = jnp.broadcast_to(ctx[None, :], (TM, 2 * ROPE_DIM))
    return jnp.concatenate([ctx, lat], axis=0) * scale


def kernel(x_prompt, x_sample, cache_ckv, cache_krope, c, c_ctx, ada_w, ada_b, norm_mix_w, norm_ffn_w,
           mla_w_dq, mla_q_lora_norm, mla_w_uq, mla_w_dkv, mla_kv_lora_norm, mla_w_uk, mla_w_uv,
           mla_q_norm, mla_k_norm, mla_w_o, pool_w, pool_scale, router_w, router_bias,
           moe_w_gate, moe_w_up, moe_w_down):
    batch, seq, d = x_prompt.shape
    dec_batch, dec_seq, _ = x_sample.shape
    past = cache_ckv.shape[2]
    depth = ada_w.shape[0]
    assert seq == TM and past == TM and dec_seq % (2 * TM) == 0
    assert dec_batch + 1 <= 8 and depth == 2
    t_ctx = batch * seq
    t_lat = dec_batch * dec_seq
    n_ctx_tiles = t_ctx // TM
    lat_seq_tiles = dec_seq // TM
    ctx_seq_tiles = seq // TM

    def cond_of_tile(i):
        return jnp.where(i < n_ctx_tiles, 0, 1 + (i - n_ctx_tiles) // lat_seq_tiles)

    def table_of_tile(i):
        return jnp.where(i < n_ctx_tiles, 0, 1 + (i - n_ctx_tiles) % lat_seq_tiles)

    x_ctx = x_prompt.reshape(t_ctx, d)
    x_lat = x_sample.reshape(t_lat, d)
    conds = jnp.concatenate([c_ctx[None, :], c, jnp.zeros((7 - dec_batch, d), F32)], axis=0)
    mods = _adaln(conds, ada_w, ada_b).reshape(depth, 8, 6, d)

    rw = jnp.pad(router_w, ((0, 0), (0, LANES - N_EXPERTS)))
    rwh = rw.astype(BF16)
    rwl = (rw - rwh.astype(F32)).astype(BF16)
    rb = jnp.broadcast_to(router_bias.astype(F32)[:, None], (N_EXPERTS, TM))
    wg = moe_w_gate.astype(BF16)
    wu = moe_w_up.astype(BF16)
    wd = moe_w_down.astype(BF16)

    j = 0
    q_norm, k_norm = mla_q_norm[j], mla_k_norm[j]
    w_uqt = mla_w_uq[j].T.astype(BF16)
    w_dkv = mla_w_dkv[j]
    w_dkv = jnp.concatenate([w_dkv, _swap_rope_halves(w_dkv[:, KV_LORA:])], axis=1).astype(BF16)
    q_scale = ATTN_SCALE * math.log2(math.e)
    tq = _rope_tables(dec_seq, q_norm[QK_NOPE:], q_scale).T
    tk = _rope_tables(dec_seq, k_norm[QK_NOPE:], 1.0)
    qnn = jnp.broadcast_to((q_norm[:QK_NOPE] * q_scale)[:, None], (QK_NOPE, TM))
    knn = k_norm[:QK_NOPE][None, :]

    qt, ckv, kr = _mla_in(x_ctx, x_lat, mods[0], cond_of_tile, table_of_tile, norm_mix_w[0][None, :],
                          mla_w_dq[j].astype(BF16), mla_q_lora_norm[j][None, :], w_uqt, w_dkv,
                          mla_kv_lora_norm[j][None, :], qnn, tq)
    state_ckv = ckv[:t_ctx].reshape(batch, 1, seq, KV_LORA)
    state_krope = kr[:t_ctx, :ROPE_DIM].reshape(batch, 1, seq, ROPE_DIM)

    w_uk = mla_w_uk[j].astype(BF16)
    w_uvt = mla_w_uv[j].T.astype(BF16)
    k_ctx, vt_ctx = _kv_expand(ckv[:t_ctx], kr[:t_ctx], tk, lambda i: 0, w_uk, w_uvt, knn)
    kv_len = past + dec_seq
    ckv_lat = jnp.concatenate(
        [cache_ckv[:, j], ckv[t_ctx:].reshape(dec_batch, dec_seq, KV_LORA)], axis=1)
    kr_cache = jnp.pad(cache_krope[:, j], ((0, 0), (0, 0), (0, LANES - ROPE_DIM)))
    kr_lat = jnp.concatenate([kr_cache, kr[t_ctx:].reshape(dec_batch, dec_seq, LANES)], axis=1)
    kv_seq_tiles = kv_len // TM
    k_lat, vt_lat = _kv_expand(ckv_lat.reshape(-1, KV_LORA), kr_lat.reshape(-1, LANES), tk,
                               lambda i: i % kv_seq_tiles, w_uk, w_uvt, knn)

    logit_bound = q_scale * QK_HEAD * jnp.max(jnp.abs(q_norm)) * jnp.max(jnp.abs(k_norm))
    bounded = (logit_bound <= MAX_SAFE_LOGIT).astype(I32).reshape(1)
    attn_ctx = _attention(bounded, qt, k_ctx, vt_ctx, 0, batch, seq, seq, tq=seq, heads=N_HEADS, group=8)
    attn_lat = _attention(bounded, qt, k_lat, vt_lat, t_ctx, dec_batch, dec_seq, kv_len, tq=2 * TM,
                          heads=4, group=8)

    x1, h2p, route = _mla_out(attn_ctx, attn_lat, x_ctx, x_lat, mods[0], cond_of_tile,
                              mla_w_o[j].astype(BF16), norm_ffn_w[0][None, :], rwh, rwl, rb)
    mo = _moe_layer(h2p, route, 0, wg, wu, wd)

    x2, h2p, route = _pool_layer(x1, mo, mods[0], mods[1], cond_of_tile, n_ctx_tiles, ctx_seq_tiles,
                                 lat_seq_tiles, norm_mix_w[1][None, :], pool_w[0].astype(BF16),
                                 pool_scale[0][None, :], norm_ffn_w[1][None, :], rwh, rwl, rb)
    mo = _moe_layer(h2p, route, 1, wg, wu, wd)
    y_ctx = _final(x2, mo, mods[1], cond_of_tile, 0, n_ctx_tiles)
    y_lat = _final(x2, mo, mods[1], cond_of_tile, n_ctx_tiles, t_lat // TM)

    return (y_ctx.reshape(batch, seq, d), y_lat.reshape(dec_batch, dec_seq, d), state_ckv, state_krope)
```

```python
import functools
import math

import jax
import jax.numpy as jnp
import numpy as np
from jax import lax
from jax.experimental import pallas as pl
from jax.experimental.pallas import tpu as pltpu

F32 = jnp.float32
BF16 = jnp.bfloat16
I32 = jnp.int32

GRID_W = 64
N_HEADS = 16
QK_NOPE = 128
ROPE_DIM = 64
QK_HEAD = QK_NOPE + ROPE_DIM
V_HEAD = 128
KV_LORA = 256
ROPE_THETA = 10000.0
AXIS_DIM = ROPE_DIM // 2
ATTN_SCALE = QK_HEAD ** -0.5
POOL_WINDOWS = (2, 4, 8, 16)
N_EXPERTS = 16
N_GROUPS = 4
EXPERTS_PER_GROUP = N_EXPERTS // N_GROUPS
EPS = 1e-6

LANES = 128
SMEM_WORDS_TILE = 1024
HEAD_PAD = 2 * LANES
TM = 256
HALO = 16
N_PAIRS = 6
N_CLASSES = N_GROUPS * N_PAIRS
PAIR_A = (0, 0, 0, 1, 1, 3)
PAIR_B = (1, 2, 3, 3, 2, 2)
ROW_CHUNKS = 16
BUF_PITCH = 24
VMEM_LIMIT = 52 * 1024 * 1024
MAX_SAFE_LOGIT = 64.0


def _params(n_axes):
    return pltpu.CompilerParams(
        dimension_semantics=("arbitrary",) * n_axes, vmem_limit_bytes=VMEM_LIMIT)


def _rms(x):
    return x * lax.rsqrt(jnp.mean(x * x, axis=-1, keepdims=True) + EPS)


def _store_token_major(ref, x, pitch=ROW_CHUNKS):
    rows = x.shape[0]
    for c in range(ROW_CHUNKS):
        ref[pl.ds(c, rows, stride=pitch), :] = x[:, c * LANES:(c + 1) * LANES]


def _load_token_major(ref, rows, pitch=ROW_CHUNKS):
    return jnp.concatenate(
        [ref[pl.ds(c, rows, stride=pitch), :] for c in range(ROW_CHUNKS)], axis=1)


def _ada_kernel(cond_ref, w_ref, b_ref, o_ref):
    c = cond_ref[...]
    s = c * jax.nn.sigmoid(c)
    o_ref[...] = jnp.dot(s, w_ref[...], preferred_element_type=F32,
                         precision=lax.Precision.HIGHEST) + b_ref[...]


def _adaln(conds, ada_w, ada_b):
    depth, d, n = ada_w.shape
    tn = 1024
    return pl.pallas_call(
        _ada_kernel,
        grid=(depth, n // tn),
        in_specs=[
            pl.BlockSpec((8, d), lambda l, j: (0, 0)),
            pl.BlockSpec((None, d, tn), lambda l, j: (l, 0, j)),
            pl.BlockSpec((None, 1, tn), lambda l, j: (l, 0, j)),
        ],
        out_specs=pl.BlockSpec((None, 8, tn), lambda l, j: (l, 0, j)),
        out_shape=jax.ShapeDtypeStruct((depth, 8, n), F32),
        compiler_params=_params(2),
        name="adaln",
    )(conds, ada_w, ada_b.reshape(depth, 1, n))


def _mla_in_kernel(xc_ref, xl_ref, mod_ref, nw_ref, wdq_ref, qln_ref, wuqt_ref, wdkv_ref, kvn_ref,
                   qnn_ref, tq_ref, qt_ref, ckv_ref, kr_ref, *, n_ctx_tiles):
    x = jnp.where(pl.program_id(0) < n_ctx_tiles, xc_ref[...], xl_ref[...])
    h = _rms(x) * (nw_ref[...] * (1.0 + mod_ref[1:2, :])) + mod_ref[0:1, :]
    hb = h.astype(BF16)
    kv = jnp.dot(hb, wdkv_ref[...], preferred_element_type=F32)
    ckv_ref[...] = _rms(kv[:, :KV_LORA]) * kvn_ref[...]
    kr_ref[...] = kv[:, KV_LORA:]
    cq = jnp.dot(hb, wdq_ref[...], preferred_element_type=F32)
    cqn = (_rms(cq) * qln_ref[...]).astype(BF16)
    qt = lax.dot_general(wuqt_ref[...], cqn, (((1,), (1,)), ((), ())),
                         preferred_element_type=F32)
    t_same = tq_ref[:ROPE_DIM, :]
    t_swap = tq_ref[ROPE_DIM:, :]
    qnn = qnn_ref[...]
    half = AXIS_DIM // 2
    for hd in range(N_HEADS):
        a = qt[hd * QK_HEAD: hd * QK_HEAD + QK_NOPE, :]
        b = qt[hd * QK_HEAD + QK_NOPE: (hd + 1) * QK_HEAD, :]
        ss = jnp.sum(a * a, axis=0, keepdims=True) + jnp.sum(b * b, axis=0, keepdims=True)
        r = lax.rsqrt(ss * (1.0 / QK_HEAD) + EPS)
        b_swap = jnp.concatenate([b[half:2 * half], b[:half], b[3 * half:], b[2 * half:3 * half]], axis=0)
        rot = ((b * t_same + b_swap * t_swap) * r).astype(BF16)
        qt_ref[hd * HEAD_PAD: hd * HEAD_PAD + LANES, :] = (a * r * qnn).astype(BF16)
        qt_ref[hd * HEAD_PAD + LANES: hd * HEAD_PAD + LANES + ROPE_DIM, :] = rot
        qt_ref[hd * HEAD_PAD + LANES + ROPE_DIM: (hd + 1) * HEAD_PAD, :] = rot


def _ctx_lat_specs(n_ctx_tiles, width):
    return [pl.BlockSpec((TM, width), lambda i: (jnp.minimum(i, n_ctx_tiles - 1), 0)),
            pl.BlockSpec((TM, width), lambda i: (jnp.maximum(i - n_ctx_tiles, 0), 0))]


def _mla_in(x_ctx, x_lat, mods, cond_of_tile, table_of_tile, nw, wdq, qln, wuqt, wdkv, kvn, qnn, tq):
    d = x_ctx.shape[1]
    n_ctx_tiles = x_ctx.shape[0] // TM
    t = x_ctx.shape[0] + x_lat.shape[0]
    nt = t // TM
    const = lambda i: (0, 0)
    return pl.pallas_call(
        functools.partial(_mla_in_kernel, n_ctx_tiles=n_ctx_tiles),
        grid=(nt,),
        in_specs=_ctx_lat_specs(n_ctx_tiles, d) + [
            pl.BlockSpec((None, 6, d), lambda i: (cond_of_tile(i), 0, 0)),
            pl.BlockSpec((1, d), const),
            pl.BlockSpec(wdq.shape, const),
            pl.BlockSpec((1, wdq.shape[1]), const),
            pl.BlockSpec(wuqt.shape, const),
            pl.BlockSpec(wdkv.shape, const),
            pl.BlockSpec((1, KV_LORA), const),
            pl.BlockSpec((LANES, TM), const),
            pl.BlockSpec((LANES, TM), lambda i: (0, table_of_tile(i))),
        ],
        out_specs=[
            pl.BlockSpec((N_HEADS * HEAD_PAD, TM), lambda i: (0, i)),
            pl.BlockSpec((TM, KV_LORA), lambda i: (i, 0)),
            pl.BlockSpec((TM, LANES), lambda i: (i, 0)),
        ],
        out_shape=[
            jax.ShapeDtypeStruct((N_HEADS * HEAD_PAD, t), BF16),
            jax.ShapeDtypeStruct((t, KV_LORA), F32),
            jax.ShapeDtypeStruct((t, LANES), F32),
        ],
        compiler_params=_params(1),
        name="mla_in",
    )(x_ctx, x_lat, mods, nw, wdq, qln, wuqt, wdkv, kvn, qnn, tq)


def _kv_expand_kernel(ckv_ref, kr_ref, tk_ref, wuk_ref, wuvt_ref, knn_ref, k_ref, vt_ref):
    cb = ckv_ref[...].astype(BF16)
    kn = jnp.dot(cb, wuk_ref[...], preferred_element_type=F32)
    vt_ref[...] = lax.dot_general(wuvt_ref[...], cb, (((1,), (1,)), ((), ())),
                                  preferred_element_type=F32).astype(BF16)
    kr = kr_ref[...]
    rope_lane = lax.broadcasted_iota(I32, (1, LANES), 1) < ROPE_DIM
    ssr = jnp.sum(jnp.where(rope_lane, kr * kr, 0.0), axis=-1, keepdims=True)
    bk = kr * tk_ref[...]
    rot = jnp.where(rope_lane, bk + pltpu.roll(bk, ROPE_DIM, axis=1), 0.0)
    knn = knn_ref[...]
    for hd in range(N_HEADS):
        a = kn[:, hd * LANES: (hd + 1) * LANES]
        r = lax.rsqrt((jnp.sum(a * a, axis=-1, keepdims=True) + ssr) * (1.0 / QK_HEAD) + EPS)
        k_ref[:, hd * HEAD_PAD: hd * HEAD_PAD + LANES] = (a * r * knn).astype(BF16)
        k_ref[:, hd * HEAD_PAD + LANES: (hd + 1) * HEAD_PAD] = (rot * r).astype(BF16)


def _kv_expand(ckv, kr, tk, table_of_tile, wuk, wuvt, knn):
    t = ckv.shape[0]
    const = lambda i: (0, 0)
    return pl.pallas_call(
        _kv_expand_kernel,
        grid=(t // TM,),
        in_specs=[
            pl.BlockSpec((TM, KV_LORA), lambda i: (i, 0)),
            pl.BlockSpec((TM, LANES), lambda i: (i, 0)),
            pl.BlockSpec((TM, LANES), lambda i: (table_of_tile(i), 0)),
            pl.BlockSpec(wuk.shape, const),
            pl.BlockSpec(wuvt.shape, const),
            pl.BlockSpec((1, LANES), const),
        ],
        out_specs=[
            pl.BlockSpec((TM, N_HEADS * HEAD_PAD), lambda i: (i, 0)),
            pl.BlockSpec((None, N_HEADS * V_HEAD, TM), lambda i: (i, 0, 0)),
        ],
        out_shape=[
            jax.ShapeDtypeStruct((t, N_HEADS * HEAD_PAD), BF16),
            jax.ShapeDtypeStruct((t // TM, N_HEADS * V_HEAD, TM), BF16),
        ],
        compiler_params=_params(1),
        name="kv_expand",
    )(ckv, kr, tk, wuk, wuvt, knn)


def _attn_kernel(bounded_ref, qt_ref, k_ref, vt_ref, o_ref, *, heads, nk, group):
    tq = qt_ref.shape[1]
    chains = [(hh, sub) for hh in range(heads) for sub in range(tq // TM)]

    def load_q(part):
        return [qt_ref[hh * HEAD_PAD:(hh + 1) * HEAD_PAD, sub * TM:(sub + 1) * TM] for hh, sub in part]

    def store(part, ls, accs):
        for (hh, sub), l, acc in zip(part, ls, accs):
            o_ref[sub * TM:(sub + 1) * TM, hh * V_HEAD:(hh + 1) * V_HEAD] = (acc / l).T.astype(BF16)

    @pl.when(bounded_ref[0] == 1)
    def _():
        for c0 in range(0, len(chains), group):
            part = chains[c0:c0 + group]
            qts = load_q(part)
            lps = [jnp.zeros((8, TM), F32) for _ in part]
            accs = [None for _ in part]
            p_prev = None
            for j in range(nk + 1):
                p_cur = []
                if j < nk:
                    for n, ((hh, _), qt) in enumerate(zip(part, qts)):
                        ks = k_ref[j * TM:(j + 1) * TM, hh * HEAD_PAD:(hh + 1) * HEAD_PAD]
                        p = jnp.exp2(jnp.dot(ks, qt, preferred_element_type=F32))
                        lps[n] = lps[n] + jnp.sum(p.reshape(TM // 8, 8, TM), axis=0)
                        p_cur.append(p.astype(BF16))
                if j > 0:
                    for n, (hh, _) in enumerate(part):
                        vt = vt_ref[j - 1, hh * V_HEAD:(hh + 1) * V_HEAD, :]
                        pv = jnp.dot(vt, p_prev[n], preferred_element_type=F32)
                        accs[n] = pv if accs[n] is None else accs[n] + pv
                p_prev = p_cur
            store(part, [jnp.sum(lp, axis=0, keepdims=True) for lp in lps], accs)

    @pl.when(bounded_ref[0] == 0)
    def _():
        for chain in chains:
            hh = chain[0]
            qt, = load_q([chain])

            def body(j, carry, hh=hh, qt=qt):
                m, l, acc = carry
                start = pl.multiple_of(j * TM, TM)
                ks = k_ref[pl.ds(start, TM), hh * HEAD_PAD:(hh + 1) * HEAD_PAD]
                s = jnp.dot(ks, qt, preferred_element_type=F32)
                mn = jnp.maximum(m, jnp.max(s, axis=0, keepdims=True))
                alpha = jnp.exp2(m - mn)
                p = jnp.exp2(s - mn)
                l = alpha * l + jnp.sum(p, axis=0, keepdims=True)
                vt = vt_ref[j, hh * V_HEAD:(hh + 1) * V_HEAD, :]
                acc = alpha * acc + jnp.dot(vt, p.astype(BF16), preferred_element_type=F32)
                return mn, l, acc

            init = (jnp.full((1, TM), -jnp.inf, F32), jnp.zeros((1, TM), F32),
                    jnp.zeros((V_HEAD, TM), F32))
            _, l, acc = lax.fori_loop(0, nk, body, init)
            store([chain], [l], [acc])


def _attention(bounded, qt, k, vt, q_col0, n_batch, q_len, kv_len, tq, heads, group):
    nq = q_len // tq
    hb = N_HEADS // heads
    qb0 = q_col0 // tq
    nk = kv_len // TM
    return pl.pallas_call(
        functools.partial(_attn_kernel, heads=heads, nk=nk, group=group),
        grid_spec=pltpu.PrefetchScalarGridSpec(
            num_scalar_prefetch=1,
            grid=(n_batch, hb, nq),
            in_specs=[
                pl.BlockSpec((heads * HEAD_PAD, tq), lambda b, h, i, f: (h, qb0 + b * nq + i)),
                pl.BlockSpec((kv_len, heads * HEAD_PAD), lambda b, h, i, f: (b, h)),
                pl.BlockSpec((nk, heads * V_HEAD, TM), lambda b, h, i, f: (b, h, 0)),
            ],
            out_specs=pl.BlockSpec((tq, heads * V_HEAD), lambda b, h, i, f: (b * nq + i, h)),
        ),
        out_shape=jax.ShapeDtypeStruct((n_batch * q_len, N_HEADS * V_HEAD), BF16),
        compiler_params=_params(3),
        name="attention",
    )(bounded, qt, k, vt)


def _route(h2, rwh_ref, rwl_ref, rb_ref, route_ref):
    hi = h2.astype(BF16)
    lo = (h2 - hi.astype(F32)).astype(BF16)
    rwh = rwh_ref[...]
    logits = (jnp.dot(hi, rwh, preferred_element_type=F32)
              + jnp.dot(lo, rwh, preferred_element_type=F32)
              + jnp.dot(hi, rwl_ref[...], preferred_element_type=F32))
    lt = logits.T[:N_EXPERTS, :]
    scores = jax.nn.sigmoid(lt)
    sel = scores + rb_ref[...]
    srow = [sel[e:e + 1, :] for e in range(N_EXPERTS)]
    prow = [scores[e:e + 1, :] for e in range(N_EXPERTS)]

    def top2_sum(a, b, c, d):
        hab, lab = jnp.maximum(a, b), jnp.minimum(a, b)
        hcd, lcd = jnp.maximum(c, d), jnp.minimum(c, d)
        return jnp.maximum(hab, hcd) + jnp.maximum(jnp.minimum(hab, hcd), jnp.maximum(lab, lcd))

    gs = [top2_sum(*srow[4 * g:4 * g + 4]) for g in range(N_GROUPS)]
    best = jnp.zeros_like(gs[0], dtype=I32)
    bestv = gs[0]
    for g in range(1, N_GROUPS):
        upd = gs[g] > bestv
        best = jnp.where(upd, g, best)
        bestv = jnp.where(upd, gs[g], bestv)

    def pick(rows, j):
        out = rows[j]
        for g in range(1, N_GROUPS):
            out = jnp.where(best == g, rows[4 * g + j], out)
        return out

    sv = [pick(srow, j) for j in range(EXPERTS_PER_GROUP)]
    pv = [pick(prow, j) for j in range(EXPERTS_PER_GROUP)]
    i1 = jnp.zeros_like(best)
    v1 = sv[0]
    for j in range(1, EXPERTS_PER_GROUP):
        upd = sv[j] > v1
        i1 = jnp.where(upd, j, i1)
        v1 = jnp.where(upd, sv[j], v1)
    neg = jnp.float32(-jnp.inf)
    i2 = jnp.where(i1 == 0, 1, 0).astype(I32)
    v2 = jnp.where(i1 == 0, sv[1], sv[0])
    for j in range(1, EXPERTS_PER_GROUP):
        cand = jnp.where(i1 == j, neg, sv[j])
        upd = cand > v2
        i2 = jnp.where(upd, j, i2)
        v2 = jnp.where(upd, cand, v2)
    ilo = jnp.minimum(i1, i2)
    ihi = jnp.maximum(i1, i2)
    pair = jnp.where(ilo == 0, ihi - 1, jnp.where(ilo == 1, 6 - ihi, N_PAIRS - 1))
    cls = best * N_PAIRS + pair

    def take(vals, idx):
        out = vals[0]
        for j in range(1, EXPERTS_PER_GROUP):
            out = jnp.where(idx == j, vals[j], out)
        return out

    wlo = take(pv, ilo)
    whi = take(pv, ihi)
    den = wlo + whi
    a_is_hi = pair == N_PAIRS - 1
    gate_a = jnp.where(a_is_hi, whi, wlo) / den
    gate_b = jnp.where(a_is_hi, wlo, whi) / den
    route_ref[...] = jnp.concatenate(
        [cls.astype(F32), gate_a, gate_b, jnp.zeros((5, cls.shape[1]), F32)], axis=0)


def _finish_sublayer(x_new, mod_ref, nw2_ref, rwh_ref, rwl_ref, rb_ref, x_out_ref, h2p_ref, route_ref):
    x_out_ref[...] = x_new
    h2 = _rms(x_new) * (nw2_ref[...] * (1.0 + mod_ref[4:5, :])) + mod_ref[3:4, :]
    _store_token_major(h2p_ref, h2)
    _route(h2, rwh_ref, rwl_ref, rb_ref, route_ref)


def _mla_out_kernel(attn_c_ref, attn_l_ref, xc_ref, xl_ref, mod_ref, wo_ref, nw2_ref, rwh_ref, rwl_ref,
                    rb_ref, x1_ref, h2p_ref, route_ref, *, n_ctx_tiles):
    is_ctx = pl.program_id(0) < n_ctx_tiles
    attn = jnp.where(is_ctx, attn_c_ref[...], attn_l_ref[...])
    mix = jnp.dot(attn, wo_ref[...], preferred_element_type=F32)
    x1 = jnp.where(is_ctx, xc_ref[...], xl_ref[...]) + mod_ref[2:3, :] * mix
    _finish_sublayer(x1, mod_ref, nw2_ref, rwh_ref, rwl_ref, rb_ref, x1_ref, h2p_ref, route_ref)


def _sublayer_out_specs(t, d):
    nt = t // TM
    specs = [
        pl.BlockSpec((TM, d), lambda i: (i, 0)),
        pl.BlockSpec((TM * ROW_CHUNKS, LANES), lambda i: (i, 0)),
        pl.BlockSpec((None, 8, TM), lambda i: (i, 0, 0)),
    ]
    shapes = [
        jax.ShapeDtypeStruct((t, d), F32),
        jax.ShapeDtypeStruct((t * ROW_CHUNKS, LANES), F32),
        jax.ShapeDtypeStruct((nt, 8, TM), F32),
    ]
    return specs, shapes


def _mla_out(attn_ctx, attn_lat, x_ctx, x_lat, mods, cond_of_tile, wo, nw2, rwh, rwl, rb):
    d = x_ctx.shape[1]
    t = x_ctx.shape[0] + x_lat.shape[0]
    assert d == ROW_CHUNKS * LANES
    const = lambda i: (0, 0)
    n_ctx_tiles = attn_ctx.shape[0] // TM
    out_specs, out_shape = _sublayer_out_specs(t, d)
    return pl.pallas_call(
        functools.partial(_mla_out_kernel, n_ctx_tiles=n_ctx_tiles),
        grid=(t // TM,),
        in_specs=_ctx_lat_specs(n_ctx_tiles, attn_ctx.shape[1]) + _ctx_lat_specs(n_ctx_tiles, d) + [
            pl.BlockSpec((None, 6, d), lambda i: (cond_of_tile(i), 0, 0)),
            pl.BlockSpec(wo.shape, const),
            pl.BlockSpec((1, d), const),
            pl.BlockSpec(rwh.shape, const),
            pl.BlockSpec(rwl.shape, const),
            pl.BlockSpec(rb.shape, const),
        ],
        out_specs=out_specs,
        out_shape=out_shape,
        compiler_params=_params(1),
        name="mla_out",
    )(attn_ctx, attn_lat, x_ctx, x_lat, mods, wo, nw2, rwh, rwl, rb)


def _pool_bands():
    t = jnp.arange(TM, dtype=I32)[:, None]
    e = jnp.arange(TM, dtype=I32)[None, :]
    c = jnp.arange(LANES, dtype=I32)[None, :]
    pos_hal = jnp.where(c < HALO, c - HALO, jnp.where(c < 2 * HALO, TM + c - HALO, 4 * TM))
    mids, hals = [], []
    for w in POOL_WINDOWS:
        half = w // 2
        mids.append(((e - t >= -half) & (e - t <= half - 1)).astype(BF16))
        hals.append(((pos_hal - t >= -half) & (pos_hal - t <= half - 1)).astype(BF16))
    return jnp.stack(mids), jnp.stack(hals)


def _pool_kernel(x_ref, mo_ref, xp_ref, mop_ref, xn_ref, mon_ref, modp_ref, mod_ref, nw_ref, pw_ref,
                 ps_ref, nw2_ref, rwh_ref, rwl_ref, rb_ref, bmid_ref, bhal_ref, x2_ref, h2p_ref, route_ref,
                 *, n_ctx_tiles, ctx_seq_tiles, lat_seq_tiles):
    i = pl.program_id(0)
    is_lat = i >= n_ctx_tiles
    seq_tiles = jnp.where(is_lat, lat_seq_tiles, ctx_seq_tiles)
    in_seq = jnp.where(is_lat, i - n_ctx_tiles, i) % seq_tiles
    has_prev = in_seq > 0
    has_next = in_seq < seq_tiles - 1
    g2p = modp_ref[5:6, :]
    gain1 = nw_ref[...] * (1.0 + mod_ref[1:2, :])
    sh1 = mod_ref[0:1, :]

    def pre(xv, mo):
        xx = xv + g2p * mo
        return xx, _rms(xx) * gain1 + sh1

    xcur, h = pre(x_ref[...], _load_token_major(mo_ref, TM))
    _, hprev = pre(xp_ref[...], _load_token_major(mop_ref, HALO))
    _, hnext = pre(xn_ref[...], _load_token_major(mon_ref, HALO))
    d = h.shape[1]
    gw = d // len(POOL_WINDOWS)
    hb = h.astype(BF16)
    hprev = jnp.where(has_prev, hprev, 0.0)
    hnext = jnp.where(has_next, hnext, 0.0)
    halo = jnp.concatenate(
        [hprev.astype(BF16), hnext.astype(BF16), jnp.zeros((LANES - 2 * HALO, d), BF16)], axis=0)
    t_col = lax.broadcasted_iota(I32, (TM, 1), 0)
    lo_bound = jnp.where(has_prev, -HALO, 0)
    hi_bound = jnp.where(has_next, TM - 1 + HALO, TM - 1)

    ys = []
    for g, w in enumerate(POOL_WINDOWS):
        half = w // 2
        sl = slice(g * gw, (g + 1) * gw)
        wsum = (jnp.dot(bmid_ref[g], hb[:, sl], preferred_element_type=F32)
                + jnp.dot(bhal_ref[g], halo[:, sl], preferred_element_type=F32))
        cnt = (jnp.minimum(t_col + (half - 1), hi_bound)
               - jnp.maximum(t_col - half, lo_bound) + 1).astype(F32)
        pooled = wsum / cnt - h[:, sl]
        ys.append(jnp.dot(pooled.astype(BF16), pw_ref[g], preferred_element_type=F32))
    y = jnp.concatenate(ys, axis=1) * ps_ref[...]
    x2 = xcur + mod_ref[2:3, :] * y
    _finish_sublayer(x2, mod_ref, nw2_ref, rwh_ref, rwl_ref, rb_ref, x2_ref, h2p_ref, route_ref)


def _pool_layer(x1, mo, mods_prev, mods, cond_of_tile, n_ctx_tiles, ctx_seq_tiles, lat_seq_tiles,
                nw, pw, ps, nw2, rwh, rwl, rb):
    t, d = x1.shape
    const = lambda i: (0, 0)
    hb = TM // HALO
    last = t // HALO - 1
    prev_map = lambda i: (jnp.maximum(i * hb - 1, 0), 0)
    next_map = lambda i: (jnp.minimum((i + 1) * hb, last), 0)
    band_mid, band_hal = _pool_bands()
    out_specs, out_shape = _sublayer_out_specs(t, d)
    return pl.pallas_call(
        functools.partial(_pool_kernel, n_ctx_tiles=n_ctx_tiles, ctx_seq_tiles=ctx_seq_tiles,
                          lat_seq_tiles=lat_seq_tiles),
        grid=(t // TM,),
        in_specs=[
            pl.BlockSpec((TM, d), lambda i: (i, 0)),
            pl.BlockSpec((TM * ROW_CHUNKS, LANES), lambda i: (i, 0)),
            pl.BlockSpec((HALO, d), prev_map),
            pl.BlockSpec((HALO * ROW_CHUNKS, LANES), prev_map),
            pl.BlockSpec((HALO, d), next_map),
            pl.BlockSpec((HALO * ROW_CHUNKS, LANES), next_map),
            pl.BlockSpec((None, 6, d), lambda i: (cond_of_tile(i), 0, 0)),
            pl.BlockSpec((None, 6, d), lambda i: (cond_of_tile(i), 0, 0)),
            pl.BlockSpec((1, d), const),
            pl.BlockSpec(pw.shape, lambda i: (0, 0, 0)),
            pl.BlockSpec((1, d), const),
            pl.BlockSpec((1, d), const),
            pl.BlockSpec(rwh.shape, const),
            pl.BlockSpec(rwl.shape, const),
            pl.BlockSpec(rb.shape, const),
            pl.BlockSpec(band_mid.shape, lambda i: (0, 0, 0)),
            pl.BlockSpec(band_hal.shape, lambda i: (0, 0, 0)),
        ],
        out_specs=out_specs,
        out_shape=out_shape,
        compiler_params=_params(1),
        name="pool_layer",
    )(x1, mo, x1, mo, x1, mo, mods_prev, mods, nw, pw, ps, nw2, rwh, rwl, rb, band_mid, band_hal)


def _final_kernel(x_ref, mo_ref, mod_ref, o_ref):
    o_ref[...] = x_ref[...] + mod_ref[5:6, :] * _load_token_major(mo_ref, x_ref.shape[0])


def _final(x, mo, mods, cond_of_tile, tile0, n_tiles):
    d = x.shape[1]
    per_step = 2
    assert tile0 % per_step == 0 and n_tiles % per_step == 0
    rows = per_step * TM
    blk0 = tile0 // per_step
    return pl.pallas_call(
        _final_kernel,
        grid=(n_tiles // per_step,),
        in_specs=[
            pl.BlockSpec((rows, d), lambda i: (blk0 + i, 0)),
            pl.BlockSpec((rows * ROW_CHUNKS, LANES), lambda i: (blk0 + i, 0)),
            pl.BlockSpec((None, 6, d), lambda i: (cond_of_tile(tile0 + per_step * i), 0, 0)),
        ],
        out_specs=pl.BlockSpec((rows, d), lambda i: (i, 0)),
        out_shape=jax.ShapeDtypeStruct((n_tiles * TM, d), F32),
        compiler_params=_params(1),
        name="final_residual",
    )(x, mo, mods)


def _moe_kernel(ea_ref, eb_ref, valid_ref, src_ref, dst_ref, h_hbm, g_ref, wga_ref, wua_ref, wda_ref,
                wgb_ref, wub_ref, wdb_ref, mo_hbm, xbuf, ybuf, sem_in, sem_out):
    i = pl.program_id(0)
    slot = i % 2
    nslot = 1 - slot
    is_valid = valid_ref[i] == 1
    prev_valid = jnp.logical_and(i >= 1, valid_ref[jnp.maximum(i - 1, 0)] == 1)
    prev2_valid = jnp.logical_and(i >= 2, valid_ref[jnp.maximum(i - 2, 0)] == 1)

    def gather_copy(tok, r, slot):
        return pltpu.make_async_copy(
            h_hbm.at[pl.ds(pl.multiple_of(tok * ROW_CHUNKS, ROW_CHUNKS), ROW_CHUNKS)],
            xbuf.at[slot, pl.ds(pl.multiple_of(r * BUF_PITCH, 8), ROW_CHUNKS)], sem_in.at[slot])

    def scatter_copy(tok, r, slot):
        return pltpu.make_async_copy(
            ybuf.at[slot, pl.ds(pl.multiple_of(r * ROW_CHUNKS, ROW_CHUNKS), ROW_CHUNKS)],
            mo_hbm.at[pl.ds(pl.multiple_of(tok * ROW_CHUNKS, ROW_CHUNKS), ROW_CHUNKS)], sem_out.at[slot])

    def for_rows(fn):
        def body(r, carry):
            fn(r)
            return carry
        lax.fori_loop(0, TM, body, 0, unroll=8)

    def gather_start(tile, slot):
        for_rows(lambda r: gather_copy(src_ref[tile * TM + r], r, slot).start())

    def gather_wait(slot):
        for_rows(lambda r: gather_copy(0, 0, slot).wait())

    def scatter_wait(slot):
        for_rows(lambda r: scatter_copy(0, 0, slot).wait())

    @pl.when(i == 0)
    def _():
        gather_start(0, 0)

    @pl.when(prev2_valid)
    def _():
        scatter_wait(slot)

    @pl.when(is_valid)
    def _():
        gather_wait(slot)
        for r in range(TM):
            gather_copy(src_ref[(i + 1) * TM + r], r, nslot).start(priority=r % 2)
        x = _load_token_major(xbuf.at[slot], TM, BUF_PITCH).astype(BF16)

        def ffn(wg_ref, wu_ref, wd_ref):
            g = jnp.dot(x, wg_ref[...], preferred_element_type=F32)
            u = jnp.dot(x, wu_ref[...], preferred_element_type=F32)
            a = (g * jax.nn.sigmoid(g)) * u
            return jnp.dot(a.astype(BF16), wd_ref[...], preferred_element_type=F32)

        gates = g_ref[...]
        y = gates[:, 0:1] * ffn(wga_ref, wua_ref, wda_ref) + gates[:, 1:2] * ffn(wgb_ref, wub_ref, wdb_ref)
        _store_token_major(ybuf.at[slot], y)
        for r in range(TM):
            scatter_copy(dst_ref[i * TM + r], r, slot).start(priority=r % 2)

    @pl.when(jnp.logical_not(is_valid))
    def _():
        @pl.when(prev_valid)
        def _():
            gather_wait(slot)

        ybuf[slot] = jnp.zeros(ybuf.shape[1:], F32)
        row0 = pl.multiple_of(dst_ref[i * TM] * ROW_CHUNKS, ROW_CHUNKS)
        fill = pltpu.make_async_copy(ybuf.at[slot], mo_hbm.at[pl.ds(row0, TM * ROW_CHUNKS)],
                                     sem_out.at[slot])
        fill.start()
        fill.wait()


def _moe_sorted(h, src_tok, dst_tok, gates, tile_ea, tile_eb, tile_valid, layer, wg, wu, wd):
    n_rows = src_tok.shape[0]
    d, ff = wg.shape[-2:]
    wa = lambda i, ea, eb, va, src, dst: (layer, ea[i], 0, 0)
    wb = lambda i, ea, eb, va, src, dst: (layer, eb[i], 0, 0)
    return pl.pallas_call(
        _moe_kernel,
        grid_spec=pltpu.PrefetchScalarGridSpec(
            num_scalar_prefetch=5,
            grid=(n_rows // TM,),
            in_specs=[
                pl.BlockSpec(memory_space=pl.ANY),
                pl.BlockSpec((TM, 2), lambda i, ea, eb, va, src, dst: (i, 0)),
                pl.BlockSpec((None, None, d, ff), wa),
                pl.BlockSpec((None, None, d, ff), wa),
                pl.BlockSpec((None, None, ff, d), wa),
                pl.BlockSpec((None, None, d, ff), wb),
                pl.BlockSpec((None, None, d, ff), wb),
                pl.BlockSpec((None, None, ff, d), wb),
            ],
            out_specs=pl.BlockSpec(memory_space=pl.ANY),
            scratch_shapes=[
                pltpu.VMEM((2, TM * BUF_PITCH, LANES), F32),
                pltpu.VMEM((2, TM * ROW_CHUNKS, LANES), F32),
                pltpu.SemaphoreType.DMA((2,)),
                pltpu.SemaphoreType.DMA((2,)),
            ],
        ),
        out_shape=jax.ShapeDtypeStruct((n_rows * ROW_CHUNKS, LANES), F32),
        compiler_params=_params(1),
        name="moe_experts",
    )(tile_ea, tile_eb, tile_valid, src_tok, dst_tok, h, gates, wg, wu, wd, wg, wu, wd)


def _invert_kernel(pos_ref, out_ref, fill_ref, sem):
    fill_ref[...] = jnp.full(fill_ref.shape, -1, I32)
    clear = pltpu.make_async_copy(fill_ref, out_ref, sem)
    clear.start()
    clear.wait()

    def place(k, carry):
        out_ref[pos_ref[k]] = k
        return carry

    lax.fori_loop(0, pos_ref.shape[0], place, 0, unroll=8)


def _invert_positions(pos, n_rows):
    padded = n_rows + (-n_rows) % SMEM_WORDS_TILE
    out = pl.pallas_call(
        _invert_kernel,
        grid_spec=pltpu.PrefetchScalarGridSpec(
            num_scalar_prefetch=1,
            grid=(1,),
            in_specs=[],
            out_specs=pl.BlockSpec(memory_space=pltpu.SMEM),
            scratch_shapes=[pltpu.VMEM((padded,), I32), pltpu.SemaphoreType.DMA(())],
        ),
        out_shape=jax.ShapeDtypeStruct((padded,), I32),
        compiler_params=_params(1),
        name="invert_positions",
    )(pos)
    return out[:n_rows]


def _moe_layer(h2p, route, layer, wg, wu, wd):
    t = h2p.shape[0] // ROW_CHUNKS
    n_tiles = t // TM + N_CLASSES + 2
    cls = route[:, 0, :].reshape(t).astype(I32)
    gate_rows = jnp.swapaxes(route[:, 1:3, :], 1, 2).reshape(t, 2)
    onehot = (cls[:, None] == jnp.arange(N_CLASSES, dtype=I32)[None, :]).astype(I32)
    csum = jnp.cumsum(onehot, axis=0)
    counts = csum[-1]
    tiles_c = (counts + TM - 1) // TM
    tile_end = jnp.cumsum(tiles_c)
    tile_start = tile_end - tiles_c
    pos = jnp.sum(onehot * (csum - 1 + (tile_start * TM)[None, :]), axis=1)
    n_rows = n_tiles * TM
    row_tok = _invert_positions(pos, n_rows)
    is_pad = row_tok < 0
    src_tok = jnp.where(is_pad, jnp.arange(n_rows, dtype=I32) % t, row_tok)
    dst_tok = jnp.where(is_pad, t + jnp.cumsum(is_pad.astype(I32)) - 1, row_tok)
    tile_ids = jnp.arange(n_tiles, dtype=I32)
    total = tile_end[-1]
    tile_valid = (tile_ids < total).astype(I32)
    tile_cls = jnp.sum((tile_end[None, :] <= jnp.minimum(tile_ids, total - 1)[:, None]).astype(I32), axis=1)
    pair = tile_cls % N_PAIRS
    group = tile_cls // N_PAIRS
    tile_ea = group * EXPERTS_PER_GROUP + jnp.asarray(PAIR_A, I32)[pair]
    tile_eb = group * EXPERTS_PER_GROUP + jnp.asarray(PAIR_B, I32)[pair]
    gates = gate_rows[src_tok]
    return _moe_sorted(h2p, src_tok, dst_tok, gates, tile_ea, tile_eb, tile_valid, layer, wg, wu, wd)


def _swap_rope_halves(w):
    shp = w.shape
    return w.reshape(shp[:-1] + (2, 2, AXIS_DIM // 2))[..., ::-1, :].reshape(shp)


def _rope_tables(length, norm_rope, scale):
    rows = length // GRID_W
    row = np.repeat(np.arange(rows, dtype=np.float32), GRID_W)
    col = np.tile(np.arange(GRID_W, dtype=np.float32), rows)
    inv = np.power(np.float32(ROPE_THETA),
                   -np.arange(0, AXIS_DIM, 2, dtype=np.float32) / np.float32(AXIS_DIM)).astype(np.float32)
    ang = np.stack([row[:, None] * inv, col[:, None] * inv], axis=1)
    cos, sin = np.cos(ang), np.sin(ang)
    c_full = jnp.asarray(np.stack([cos, cos], axis=2).reshape(length, ROPE_DIM))
    s_full = jnp.asarray(np.stack([-sin, sin], axis=2).reshape(length, ROPE_DIM))
    lat = jnp.concatenate([norm_rope * c_full, _swap_rope_halves(norm_rope) * s_full], axis=1)
    ctx = jnp.concatenate([norm_rope, jnp.zeros((ROPE_DIM,), F32)])
    ctx = jnp.broadcast_to(ctx[None, :], (TM, 2 * ROPE_DIM))
    return jnp.concatenate([ctx, lat], axis=0) * scale


def kernel(x_prompt, x_sample, cache_ckv, cache_krope, c, c_ctx, ada_w, ada_b, norm_mix_w, norm_ffn_w,
           mla_w_dq, mla_q_lora_norm, mla_w_uq, mla_w_dkv, mla_kv_lora_norm, mla_w_uk, mla_w_uv,
           mla_q_norm, mla_k_norm, mla_w_o, pool_w, pool_scale, router_w, router_bias,
           moe_w_gate, moe_w_up, moe_w_down):
    batch, seq, d = x_prompt.shape
    dec_batch, dec_seq, _ = x_sample.shape
    past = cache_ckv.shape[2]
    depth = ada_w.shape[0]
    assert seq == TM and past == TM and dec_seq % (2 * TM) == 0
    assert dec_batch + 1 <= 8 and depth == 2
    t_ctx = batch * seq
    t_lat = dec_batch * dec_seq
    n_ctx_tiles = t_ctx // TM
    lat_seq_tiles = dec_seq // TM
    ctx_seq_tiles = seq // TM

    def cond_of_tile(i):
        return jnp.where(i < n_ctx_tiles, 0, 1 + (i - n_ctx_tiles) // lat_seq_tiles)

    def table_of_tile(i):
        return jnp.where(i < n_ctx_tiles, 0, 1 + (i - n_ctx_tiles) % lat_seq_tiles)

    x_ctx = x_prompt.reshape(t_ctx, d)
    x_lat = x_sample.reshape(t_lat, d)
    conds = jnp.concatenate([c_ctx[None, :], c, jnp.zeros((7 - dec_batch, d), F32)], axis=0)
    mods = _adaln(conds, ada_w, ada_b).reshape(depth, 8, 6, d)

    rw = jnp.pad(router_w, ((0, 0), (0, LANES - N_EXPERTS)))
    rwh = rw.astype(BF16)
    rwl = (rw - rwh.astype(F32)).astype(BF16)
    rb = jnp.broadcast_to(router_bias.astype(F32)[:, None], (N_EXPERTS, TM))
    wg = moe_w_gate.astype(BF16)
    wu = moe_w_up.astype(BF16)
    wd = moe_w_down.astype(BF16)

    j = 0
    q_norm, k_norm = mla_q_norm[j], mla_k_norm[j]
    w_uqt = mla_w_uq[j].T.astype(BF16)
    w_dkv = mla_w_dkv[j]
    w_dkv = jnp.concatenate([w_dkv, _swap_rope_halves(w_dkv[:, KV_LORA:])], axis=1).astype(BF16)
    q_scale = ATTN_SCALE * math.log2(math.e)
    tq = _rope_tables(dec_seq, q_norm[QK_NOPE:], q_scale).T
    tk = _rope_tables(dec_seq, k_norm[QK_NOPE:], 1.0)
    qnn = jnp.broadcast_to((q_norm[:QK_NOPE] * q_scale)[:, None], (QK_NOPE, TM))
    knn = k_norm[:QK_NOPE][None, :]

    qt, ckv, kr = _mla_in(x_ctx, x_lat, mods[0], cond_of_tile, table_of_tile, norm_mix_w[0][None, :],
                          mla_w_dq[j].astype(BF16), mla_q_lora_norm[j][None, :], w_uqt, w_dkv,
                          mla_kv_lora_norm[j][None, :], qnn, tq)
    state_ckv = ckv[:t_ctx].reshape(batch, 1, seq, KV_LORA)
    state_krope = kr[:t_ctx, :ROPE_DIM].reshape(batch, 1, seq, ROPE_DIM)

    w_uk = mla_w_uk[j].astype(BF16)
    w_uvt = mla_w_uv[j].T.astype(BF16)
    k_ctx, vt_ctx = _kv_expand(ckv[:t_ctx], kr[:t_ctx], tk, lambda i: 0, w_uk, w_uvt, knn)
    kv_len = past + dec_seq
    ckv_lat = jnp.concatenate(
        [cache_ckv[:, j], ckv[t_ctx:].reshape(dec_batch, dec_seq, KV_LORA)], axis=1)
    kr_cache = jnp.pad(cache_krope[:, j], ((0, 0), (0, 0), (0, LANES - ROPE_DIM)))
    kr_lat = jnp.concatenate([kr_cache, kr[t_ctx:].reshape(dec_batch, dec_seq, LANES)], axis=1)
    kv_seq_tiles = kv_len // TM
    k_lat, vt_lat = _kv_expand(ckv_lat.reshape(-1, KV_LORA), kr_lat.reshape(-1, LANES), tk,
                               lambda i: i % kv_seq_tiles, w_uk, w_uvt, knn)

    logit_bound = q_scale * QK_HEAD * jnp.max(jnp.abs(q_norm)) * jnp.max(jnp.abs(k_norm))
    bounded = (logit_bound <= MAX_SAFE_LOGIT).astype(I32).reshape(1)
    attn_ctx = _attention(bounded, qt, k_ctx, vt_ctx, 0, batch, seq, seq, tq=seq, heads=N_HEADS, group=8)
    attn_lat = _attention(bounded, qt, k_lat, vt_lat, t_ctx, dec_batch, dec_seq, kv_len, tq=2 * TM,
                          heads=4, group=8)

    x1, h2p, route = _mla_out(attn_ctx, attn_lat, x_ctx, x_lat, mods[0], cond_of_tile,
                              mla_w_o[j].astype(BF16), norm_ffn_w[0][None, :], rwh, rwl, rb)
    mo = _moe_layer(h2p, route, 0, wg, wu, wd)

    x2, h2p, route = _pool_layer(x1, mo, mods[0], mods[1], cond_of_tile, n_ctx_tiles, ctx_seq_tiles,
                                 lat_seq_tiles, norm_mix_w[1][None, :], pool_w[0].astype(BF16),
                                 pool_scale[0][None, :], norm_ffn_w[1][None, :], rwh, rwl, rb)
    mo = _moe_layer(h2p, route, 1, wg, wu, wd)
    y_ctx = _final(x2, mo, mods[1], cond_of_tile, 0, n_ctx_tiles)
    y_lat = _final(x2, mo, mods[1], cond_of_tile, n_ctx_tiles, t_lat // TM)

    return (y_ctx.reshape(batch, seq, d), y_lat.reshape(dec_batch, dec_seq, d), state_ckv, state_krope)
```
